```python
import jax, jax.numpy as jnp
from jax import lax
import numpy as np

D_MODEL = 1024
BATCH = 16
SEQ = 2048
DEPTH = 1

GDN_HEADS = 8
GDN_DK = 128
GDN_DV = 128
GDN_CONV = 4
GDN_CHUNK = 64
FOX_HEADS = 8
FOX_DH = 128
FOX_BLOCK = 128
D_FF = 4 * D_MODEL
EPS = 1e-6

GDN_QK_W = GDN_HEADS * GDN_DK
GDN_V_W = GDN_HEADS * GDN_DV
FOX_W = FOX_HEADS * FOX_DH
IN_SPLITS = (GDN_QK_W, GDN_QK_W, GDN_V_W, GDN_V_W, GDN_HEADS, GDN_HEADS,
             FOX_W, FOX_W, FOX_W, FOX_HEADS, D_MODEL, D_MODEL)
N_IN = sum(IN_SPLITS)

kernel_name = "hybrid_gdn_fox_gated_merge_block"


def rmsnorm(x, g):
    xf = x.astype(jnp.float32)
    y = xf * lax.rsqrt(jnp.mean(xf * xf, axis=-1, keepdims=True) + EPS)
    return (y * g.astype(jnp.float32)).astype(x.dtype)


def l2norm(x):
    xf = x.astype(jnp.float32)
    return xf * lax.rsqrt(jnp.sum(xf * xf, axis=-1, keepdims=True) + EPS)


def to_heads(x, n_heads):
    b, t, w = x.shape
    return x.reshape(b, t, n_heads, w // n_heads).transpose(0, 2, 1, 3)


def causal_depthwise_conv(x, w):
    k_width = w.shape[0]
    t = x.shape[1]
    xp = jnp.pad(x, ((0, 0), (k_width - 1, 0), (0, 0)))
    y = xp[:, 0:t] * w[0]
    for i in range(1, k_width):
        y = y + xp[:, i:i + t] * w[i]
    return y


def gated_delta_rule_chunked(q, k, v, g, beta):
    b, h, t, dk = q.shape
    dv = v.shape[-1]
    c = GDN_CHUNK
    n = t // c
    q = q.reshape(b, h, n, c, dk)
    k = k.reshape(b, h, n, c, dk)
    v = v.reshape(b, h, n, c, dv)
    g = g.reshape(b, h, n, c)
    beta = beta.reshape(b, h, n, c)

    G = jnp.cumsum(g, axis=-1)
    diff = G[..., :, None] - G[..., None, :]
    tri_incl = jnp.tril(jnp.ones((c, c), dtype=bool))
    tri_strict = jnp.tril(jnp.ones((c, c), dtype=bool), -1)
    decay = jnp.where(tri_incl, jnp.exp(jnp.where(tri_incl, diff, 0.0)), 0.0)

    kk = jnp.einsum('bhncd,bhnsd->bhncs', k, k)
    a_mat = jnp.where(tri_strict, beta[..., None] * kk * decay, 0.0)
    eye = jnp.eye(c, dtype=jnp.float32)
    rhs = jnp.concatenate([v * beta[..., None], k * (beta * jnp.exp(G))[..., None]], axis=-1)
    a_full = jnp.broadcast_to(a_mat + eye, (b, h, n, c, c))
    sol = lax.linalg.triangular_solve(a_full, rhs, left_side=True, lower=True,
                                      unit_diagonal=True)
    value, k_cum = sol[..., :dv], sol[..., dv:]

    attn_intra = jnp.einsum('bhncd,bhnsd->bhncs', q, k) * decay
    q_dec = q * jnp.exp(G)[..., None]
    g_last = G[..., -1:]
    k_dec = k * jnp.exp(g_last - G)[..., None]
    chunk_decay = jnp.exp(g_last[..., 0])

    def step(s, inp):
        value_c, kcum_c, attn_c, qdec_c, kdec_c, dec_c = inp
        v_new = value_c - jnp.einsum('bhcd,bhde->bhce', kcum_c, s)
        o = (jnp.einsum('bhcd,bhde->bhce', qdec_c, s)
             + jnp.einsum('bhcs,bhse->bhce', attn_c, v_new))
        s = s * dec_c[..., None, None] + jnp.einsum('bhcd,bhce->bhde', kdec_c, v_new)
        return s, o

    xs = tuple(jnp.moveaxis(a, 2, 0) for a in (value, k_cum, attn_intra, q_dec, k_dec, chunk_decay))
    s0 = jnp.zeros((b, h, dk, dv), jnp.float32)
    _, o = lax.scan(step, s0, xs)
    return jnp.moveaxis(o, 0, 2).reshape(b, h, t, dv)


def forgetting_attention(q, k, v, log_f):
    t = q.shape[2]
    cum = jnp.cumsum(log_f, axis=-1)
    scale = FOX_DH ** -0.5
    outs = []
    for i in range(t // FOX_BLOCK):
        q0 = i * FOX_BLOCK
        q1 = q0 + FOX_BLOCK
        s = (jnp.einsum('bhqd,bhkd->bhqk', q[:, :, q0:q1], k[:, :, :q1]) * scale
             + cum[:, :, q0:q1, None] - cum[:, :, None, :q1])
        mask = (q0 + jnp.arange(FOX_BLOCK))[:, None] >= jnp.arange(q1)[None, :]
        p = jax.nn.softmax(jnp.where(mask, s, -jnp.inf), axis=-1)
        outs.append(jnp.einsum('bhqk,bhkd->bhqd', p, v[:, :, :q1]))
    return jnp.concatenate(outs, axis=2)


def setup_inputs(seed: int = 0) -> dict:
    key = jax.random.key(seed)
    ks = jax.random.split(key, 20)
    f32 = jnp.float32

    def nrm(k, shape, fan_in):
        return jax.random.normal(k, shape, f32) * (fan_in ** -0.5)

    def gain(k, shape):
        return 1.0 + 0.02 * jax.random.normal(k, shape, f32)

    dt = jnp.exp(jax.random.uniform(ks[5], (DEPTH, GDN_HEADS), f32,
                                    minval=float(np.log(1e-3)), maxval=float(np.log(1e-1))))
    return {
        "x": jax.random.normal(ks[0], (BATCH, SEQ, D_MODEL), f32),
        "norm_mix_g": gain(ks[1], (DEPTH, D_MODEL)),
        "w_in": nrm(ks[2], (DEPTH, D_MODEL, N_IN), D_MODEL),
        "gdn_conv_w": nrm(ks[3], (DEPTH, GDN_CONV, 2 * GDN_QK_W + GDN_V_W), GDN_CONV),
        "gdn_a_log": jnp.log(jax.random.uniform(ks[4], (DEPTH, GDN_HEADS), f32, minval=1.0, maxval=16.0)),
        "gdn_dt_bias": dt + jnp.log(-jnp.expm1(-dt)),
        "gdn_norm_g": gain(ks[6], (DEPTH, GDN_DV)),
        "fox_q_norm_g": gain(ks[7], (DEPTH, FOX_DH)),
        "fox_k_norm_g": gain(ks[8], (DEPTH, FOX_DH)),
        "fox_f_bias": jax.random.uniform(ks[9], (DEPTH, FOX_HEADS), f32, minval=2.0, maxval=6.0),
        "w_proj_gdn": nrm(ks[10], (DEPTH, GDN_V_W, D_MODEL), GDN_V_W),
        "w_proj_fox": nrm(ks[11], (DEPTH, FOX_W, D_MODEL), FOX_W),
        "w_out": nrm(ks[12], (DEPTH, D_MODEL, D_MODEL), D_MODEL),
        "norm_mlp_g": gain(ks[13], (DEPTH, D_MODEL)),
        "w_up": nrm(ks[14], (DEPTH, D_MODEL, D_FF), D_MODEL),
        "w_down": nrm(ks[15], (DEPTH, D_FF, D_MODEL), D_FF),
    }


def reference(x, norm_mix_g, w_in, gdn_conv_w, gdn_a_log, gdn_dt_bias, gdn_norm_g,
              fox_q_norm_g, fox_k_norm_g, fox_f_bias, w_proj_gdn, w_proj_fox, w_out,
              norm_mlp_g, w_up, w_down):
    f32 = jnp.float32
    b, t, _ = x.shape
    split_idx = [int(i) for i in np.cumsum(IN_SPLITS)[:-1]]
    for l in range(DEPTH):
        u = rmsnorm(x, norm_mix_g[l])
        proj = u @ w_in[l]
        (gq, gk, gv, gz, ga, gb, fq, fk, fv, ff, gate_a, gate_b) = jnp.split(proj, split_idx, axis=-1)

        qkv = jax.nn.silu(causal_depthwise_conv(jnp.concatenate([gq, gk, gv], axis=-1), gdn_conv_w[l]))
        cq, ck, cv = jnp.split(qkv, [GDN_QK_W, 2 * GDN_QK_W], axis=-1)
        qh = l2norm(to_heads(cq, GDN_HEADS)) * (GDN_DK ** -0.5)
        kh = l2norm(to_heads(ck, GDN_HEADS))
        vh = to_heads(cv, GDN_HEADS).astype(f32)
        beta = jax.nn.sigmoid(gb.astype(f32)).transpose(0, 2, 1)
        g = (-jnp.exp(gdn_a_log[l].astype(f32))
             * jax.nn.softplus(ga.astype(f32) + gdn_dt_bias[l].astype(f32))).transpose(0, 2, 1)
        o_a = gated_delta_rule_chunked(qh, kh, vh, g, beta).transpose(0, 2, 1, 3)
        z = gz.reshape(b, t, GDN_HEADS, GDN_DV).astype(f32)
        o_a = rmsnorm(o_a, gdn_norm_g[l]) * jax.nn.silu(z)
        y_a = o_a.reshape(b, t, GDN_V_W).astype(x.dtype) @ w_proj_gdn[l]

        fqh = rmsnorm(to_heads(fq, FOX_HEADS), fox_q_norm_g[l]).astype(f32)
        fkh = rmsnorm(to_heads(fk, FOX_HEADS), fox_k_norm_g[l]).astype(f32)
        fvh = to_heads(fv, FOX_HEADS).astype(f32)
        log_f = jax.nn.log_sigmoid(ff.astype(f32) + fox_f_bias[l].astype(f32)).transpose(0, 2, 1)
        o_b = forgetting_attention(fqh, fkh, fvh, log_f).transpose(0, 2, 1, 3)
        y_b = o_b.reshape(b, t, FOX_W).astype(x.dtype) @ w_proj_fox[l]

        merged = jax.nn.sigmoid(gate_a) * y_a + jax.nn.sigmoid(gate_b) * y_b
        h = x + merged @ w_out[l]

        hn = rmsnorm(h, norm_mlp_g[l])
        x = h + jnp.square(jax.nn.relu(hn @ w_up[l])) @ w_down[l]
    return x
```

```python
import functools

import jax
import jax.numpy as jnp
from jax import lax
from jax.experimental import pallas as pl
from jax.experimental.pallas import tpu as pltpu

F32 = jnp.float32
BF16 = jnp.bfloat16

LANES = 128
VMEM_LIMIT_BYTES = 56 * 1024 * 1024

EPS = 1e-6
GDN_HEADS = 8
GDN_D = 128
GDN_CONV = 4
GDN_CHUNK = 64
GDN_INV_BLOCK = 16
FOX_HEADS = 8
FOX_D = 128
FOX_TQ = 256
SMALL_GA, SMALL_GB, SMALL_FF = 0, 8, 16
NEG_BIG = -1e30


def _dot(a, b):
    return jnp.dot(a, b, preferred_element_type=F32)


def _dot_nt(a, b):
    return lax.dot_general(a, b, (((1,), (1,)), ((), ())), preferred_element_type=F32)


def _sigmoid(x):
    return 1.0 / (1.0 + jnp.exp(-x))


def _softplus(x):
    return jnp.maximum(x, 0.0) + jnp.log1p(jnp.exp(-jnp.abs(x)))


def _rmsnorm(x, g):
    return x * lax.rsqrt(jnp.mean(x * x, axis=-1, keepdims=True) + EPS) * g


def _pick_lane(x, lane):
    ids = lax.broadcasted_iota(jnp.int32, x.shape, 1)
    col = jnp.sum(jnp.where(ids == lane, x, 0.0), axis=-1, keepdims=True)
    return jnp.broadcast_to(col, x.shape)


def _segment_cumsum(x, seg):
    pos = lax.broadcasted_iota(jnp.int32, x.shape, 0) % seg
    s = 1
    while s < seg:
        x = x + jnp.where(pos >= s, pltpu.roll(x, s, axis=0), 0.0)
        s *= 2
    return x


def _proj_kernel(x_ref, g_ref, w_ref, ws_hi_ref, ws_lo_ref, out_ref, small_ref):
    u = _rmsnorm(x_ref[...], g_ref[...])
    u_hi = u.astype(BF16)
    out_ref[...] = _dot(u_hi, w_ref[...])
    u_lo = (u - u_hi.astype(F32)).astype(BF16)
    ws_hi = ws_hi_ref[...]
    small_ref[...] = _dot(u_hi, ws_hi) + _dot(u_lo, ws_hi) + _dot(u_hi, ws_lo_ref[...])


def _in_projection(x2d, gain, w_main, ws_hi, ws_lo, *, tm, tn):
    m, d = x2d.shape
    n = w_main.shape[1]
    grid = (n // tn, m // tm)
    return pl.pallas_call(
        _proj_kernel,
        grid=grid,
        in_specs=[
            pl.BlockSpec((tm, d), lambda j, i: (i, 0)),
            pl.BlockSpec((1, d), lambda j, i: (0, 0)),
            pl.BlockSpec((d, tn), lambda j, i: (0, j)),
            pl.BlockSpec((d, LANES), lambda j, i: (0, 0)),
            pl.BlockSpec((d, LANES), lambda j, i: (0, 0)),
        ],
        out_specs=[
            pl.BlockSpec((tm, tn), lambda j, i: (i, j)),
            pl.BlockSpec((tm, LANES), lambda j, i: (i, 0)),
        ],
        out_shape=[
            jax.ShapeDtypeStruct((m, n), F32),
            jax.ShapeDtypeStruct((m, LANES), F32),
        ],
        compiler_params=pltpu.CompilerParams(
            dimension_semantics=("arbitrary", "arbitrary"),
            vmem_limit_bytes=VMEM_LIMIT_BYTES),
        name="in_projection",
    )(x2d, gain, w_main, ws_hi, ws_lo)


def _conv_silu(x, w):
    row = lax.broadcasted_iota(jnp.int32, x.shape, 0)
    y = None
    for i in range(GDN_CONV):
        s = GDN_CONV - 1 - i
        xs = x if s == 0 else jnp.where(row >= s, pltpu.roll(x, s, axis=0), 0.0)
        term = xs * w[i:i + 1, :]
        y = term if y is None else y + term
    return y * _sigmoid(y)


def _l2norm(x):
    return x * lax.rsqrt(jnp.sum(x * x, axis=-1, keepdims=True) + EPS)


def _unit_lower_inverse(a, eye, same_block):
    d = jnp.where(same_block, a, 0.0)
    off = a - d
    x = eye - d
    p = _dot(d.astype(BF16), d.astype(BF16))
    for step in range(3):
        p_bf = p.astype(BF16)
        x = x + _dot(x.astype(BF16), p_bf)
        if step < 2:
            p = _dot(p_bf, p_bf)
    x_bf = x.astype(BF16)
    m = _dot(x_bf, off.astype(BF16))
    m_bf = m.astype(BF16)
    m2 = _dot(m_bf, m_bf)
    im = eye - m
    y = im + _dot(im.astype(BF16), m2.astype(BF16))
    return _dot(y.astype(BF16), x_bf)


def _gdn_kernel(q_ref, k_ref, v_ref, z_ref, sm_ref, cwq_ref, cwk_ref, cwv_ref,
                alog_ref, dtb_ref, ng_ref, o_ref,
                qn_s, kn_s, vc_s, gcum_s, beta_s, val_s, kcum_s, qdec_s, kdect_s, attn_s,
                *, unroll_a):
    h = pl.program_id(1)
    t = q_ref.shape[1]
    c = GDN_CHUNK
    n_chunks = t // c

    qn_s[...] = _l2norm(_conv_silu(q_ref[0], cwq_ref[...])) * (GDN_D ** -0.5)
    kn_s[...] = _l2norm(_conv_silu(k_ref[0], cwk_ref[...]))
    vc_s[...] = _conv_silu(v_ref[0], cwv_ref[...])
    sm = sm_ref[0]
    g_all = -jnp.exp(alog_ref[...]) * _softplus(sm + dtb_ref[...])
    gcum_s[...] = _segment_cumsum(_pick_lane(g_all, SMALL_GA + h), c)
    beta_s[...] = _pick_lane(_sigmoid(sm), SMALL_GB + h)

    ri = lax.broadcasted_iota(jnp.int32, (c, c), 0)
    ci = lax.broadcasted_iota(jnp.int32, (c, c), 1)
    tri_incl = ri >= ci
    tri_strict = ri > ci
    same_block = (ri // GDN_INV_BLOCK) == (ci // GDN_INV_BLOCK)
    eye = jnp.where(ri == ci, 1.0, 0.0).astype(F32)

    def chunk_prepare(ch):
        r0 = pl.multiple_of(ch * c, c)
        rows = pl.ds(r0, c)
        q = qn_s[rows, :]
        k = kn_s[rows, :]
        v = vc_s[rows, :]
        gb = gcum_s[rows, :]
        bb = beta_s[rows, :]
        g_row = jnp.transpose(gb)[:c, :]
        diff = gb[:, :c] - g_row
        decay = jnp.where(tri_incl, jnp.exp(jnp.where(tri_incl, diff, 0.0)), 0.0)
        k_bf = k.astype(BF16)
        qk_kk = _dot_nt(jnp.concatenate([q.astype(BF16), k_bf], axis=0), k_bf)
        qk = qk_kk[:c, :]
        kk = qk_kk[c:, :]
        a_mat = jnp.where(tri_strict, bb[:, :c] * kk * decay, 0.0)
        t_inv = _unit_lower_inverse(a_mat, eye, same_block)
        e_g = jnp.exp(gb)
        rhs = jnp.concatenate([v * bb, k * (bb * e_g)], axis=1).astype(BF16)
        sol = _dot(t_inv.astype(BF16), rhs)
        val_s[rows, :] = sol[:, :GDN_D]
        kcum_s[rows, :] = sol[:, GDN_D:].astype(BF16)
        qdec_s[rows, :] = (q * e_g).astype(BF16)
        g_last = gb[c - 1:c, :]
        k_dec = k * jnp.exp(g_last - gb)
        kdect_s[ch] = jnp.transpose(k_dec).astype(BF16)
        attn_s[rows, :] = (qk * decay).astype(BF16)

    def prepare_body(i, carry):
        for u in range(unroll_a):
            chunk_prepare(i * unroll_a + u)
        return carry

    lax.fori_loop(0, n_chunks // unroll_a, prepare_body, 0)

    ng = ng_ref[...]

    def scan_body(ch, s):
        r0 = pl.multiple_of(ch * c, c)
        rows = pl.ds(r0, c)
        s_bf = s.astype(BF16)
        v_new = val_s[rows, :] - _dot(kcum_s[rows, :], s_bf)
        v_new_bf = v_new.astype(BF16)
        o = _dot(qdec_s[rows, :], s_bf) + _dot(attn_s[rows, :], v_new_bf)
        dec = jnp.exp(gcum_s[pl.ds(r0 + c - 1, 1), :])
        s = s * dec + _dot(kdect_s[ch], v_new_bf)
        z = z_ref[0, rows, :]
        o_ref[0, rows, :] = (_rmsnorm(o, ng) * (z * _sigmoid(z))).astype(o_ref.dtype)
        return s

    lax.fori_loop(0, n_chunks, scan_body, jnp.zeros((GDN_D, GDN_D), F32))


def _gdn_branch(proj3, small3, conv_w, alog_row, dtb_row, norm_g, *, col_q, col_k, col_v, col_z):
    b, t, _ = proj3.shape
    d = GDN_D
    n_chunks = t // GDN_CHUNK
    seq_spec = lambda col: pl.BlockSpec((1, t, d), lambda bi, hi: (bi, 0, col + hi))
    cw_spec = lambda col: pl.BlockSpec((GDN_CONV, d), lambda bi, hi: (0, col + hi))
    row_spec = pl.BlockSpec((1, LANES), lambda bi, hi: (0, 0))
    return pl.pallas_call(
        functools.partial(_gdn_kernel, unroll_a=4),
        grid=(b, GDN_HEADS),
        in_specs=[
            seq_spec(col_q), seq_spec(col_k), seq_spec(col_v), seq_spec(col_z),
            pl.BlockSpec((1, t, LANES), lambda bi, hi: (bi, 0, 0)),
            cw_spec(0), cw_spec(GDN_HEADS), cw_spec(2 * GDN_HEADS),
            row_spec, row_spec, row_spec,
        ],
        out_specs=pl.BlockSpec((1, t, d), lambda bi, hi: (bi, 0, hi)),
        out_shape=jax.ShapeDtypeStruct((b, t, GDN_HEADS * d), BF16),
        scratch_shapes=[
            pltpu.VMEM((t, d), F32),
            pltpu.VMEM((t, d), F32),
            pltpu.VMEM((t, d), F32),
            pltpu.VMEM((t, LANES), F32),
            pltpu.VMEM((t, LANES), F32),
            pltpu.VMEM((t, d), F32),
            pltpu.VMEM((t, d), BF16),
            pltpu.VMEM((t, d), BF16),
            pltpu.VMEM((n_chunks, d, GDN_CHUNK), BF16),
            pltpu.VMEM((t, GDN_CHUNK), BF16),
        ],
        compiler_params=pltpu.CompilerParams(
            dimension_semantics=("arbitrary", "arbitrary"),
            vmem_limit_bytes=VMEM_LIMIT_BYTES),
        name="gdn_branch",
    )(proj3, proj3, proj3, proj3, small3, conv_w, conv_w, conv_w, alog_row, dtb_row, norm_g)


def _fox_kernel(q_ref, k_ref, v_ref, sm_ref, fb_ref, gq_ref, gk_ref, o_ref,
                qn_s, kn_s, vb_s, ccol_s, crow_s, m_s, l_s, acc_s):
    h = pl.program_id(1)
    t = q_ref.shape[1]
    tq = FOX_TQ
    nq = t // tq

    qn_s[...] = (_rmsnorm(q_ref[0], gq_ref[...]) * (FOX_D ** -0.5)).astype(BF16)
    kn_s[...] = _rmsnorm(k_ref[0], gk_ref[...]).astype(BF16)
    vb_s[...] = v_ref[0].astype(BF16)
    log_f = -_softplus(-(sm_ref[0] + fb_ref[...]))
    cum = _segment_cumsum(_pick_lane(log_f, SMALL_FF + h), t)
    ccol_s[...] = cum
    cum_t = jnp.transpose(cum)
    for j in range(nq):
        crow_s[j] = cum_t[:8, j * tq:(j + 1) * tq]

    ri = lax.broadcasted_iota(jnp.int32, (tq, tq), 0)
    ci = lax.broadcasted_iota(jnp.int32, (tq, tq), 1)
    causal = ri >= ci

    def kv_step(qi, kj, masked):
        q0 = pl.multiple_of(qi * tq, tq)
        k0 = pl.multiple_of(kj * tq, tq)
        q = qn_s[pl.ds(q0, tq), :]
        k = kn_s[pl.ds(k0, tq), :]
        v = vb_s[pl.ds(k0, tq), :]
        cq = ccol_s[pl.ds(q0, tq), :]
        ck = crow_s[kj][0:1, :]
        s = _dot_nt(q, k) + jnp.concatenate([cq] * (tq // LANES), axis=1) - ck
        if masked:
            s = jnp.where(causal, s, NEG_BIG)
        m_prev = m_s[...]
        m_new = jnp.maximum(m_prev, jnp.max(s, axis=-1, keepdims=True))
        alpha = jnp.exp(m_prev - m_new)
        p = jnp.exp(s - jnp.concatenate([m_new] * (tq // LANES), axis=1))
        l_s[...] = alpha * l_s[...] + jnp.sum(p, axis=-1, keepdims=True)
        acc_s[...] = alpha * acc_s[...] + _dot(p.astype(BF16), v)
        m_s[...] = m_new

    def q_body(qi, carry):
        m_s[...] = jnp.full(m_s.shape, NEG_BIG, F32)
        l_s[...] = jnp.zeros(l_s.shape, F32)
        acc_s[...] = jnp.zeros(acc_s.shape, F32)

        def k_body(kj, c2):
            kv_step(qi, kj, masked=False)
            return c2

        lax.fori_loop(0, qi, k_body, 0)
        kv_step(qi, qi, masked=True)
        q0 = pl.multiple_of(qi * tq, tq)
        o_ref[0, pl.ds(q0, tq), :] = (acc_s[...] / l_s[...]).astype(o_ref.dtype)
        return carry

    lax.fori_loop(0, nq, q_body, 0)


def _fox_branch(proj3, small3, fb_row, gq, gk, *, col_q, col_k, col_v):
    b, t, _ = proj3.shape
    d = FOX_D
    nq = t // FOX_TQ
    seq_spec = lambda col: pl.BlockSpec((1, t, d), lambda bi, hi: (bi, 0, col + hi))
    row_spec = pl.BlockSpec((1, LANES), lambda bi, hi: (0, 0))
    return pl.pallas_call(
        _fox_kernel,
        grid=(b, FOX_HEADS),
        in_specs=[
            seq_spec(col_q), seq_spec(col_k), seq_spec(col_v),
            pl.BlockSpec((1, t, LANES), lambda bi, hi: (bi, 0, 0)),
            row_spec, row_spec, row_spec,
        ],
        out_specs=pl.BlockSpec((1, t, d), lambda bi, hi: (bi, 0, hi)),
        out_shape=jax.ShapeDtypeStruct((b, t, FOX_HEADS * d), BF16),
        scratch_shapes=[
            pltpu.VMEM((t, d), BF16),
            pltpu.VMEM((t, d), BF16),
            pltpu.VMEM((t, d), BF16),
            pltpu.VMEM((t, LANES), F32),
            pltpu.VMEM((nq, 8, FOX_TQ), F32),
            pltpu.VMEM((FOX_TQ, LANES), F32),
            pltpu.VMEM((FOX_TQ, LANES), F32),
            pltpu.VMEM((FOX_TQ, d), F32),
        ],
        compiler_params=pltpu.CompilerParams(
            dimension_semantics=("arbitrary", "arbitrary"),
            vmem_limit_bytes=VMEM_LIMIT_BYTES),
        name="fox_branch",
    )(proj3, proj3, proj3, small3, fb_row, gq, gk)


def _merge_mlp_kernel(oa_ref, ob_ref, ga_ref, gb_ref, x_ref, pa_ref, pb_ref, wo_ref, ng_ref,
                      wu_ref, wd_ref, out_ref, *, ff_chunk):
    ya = _dot(oa_ref[...], pa_ref[...])
    yb = _dot(ob_ref[...], pb_ref[...])
    merged = _sigmoid(ga_ref[...]) * ya + _sigmoid(gb_ref[...]) * yb
    hid = x_ref[...] + _dot(merged.astype(BF16), wo_ref[...])
    hn = _rmsnorm(hid, ng_ref[...]).astype(BF16)
    acc = hid
    d_ff = wu_ref.shape[1]
    for c0 in range(0, d_ff, ff_chunk):
        up = _dot(hn, wu_ref[:, c0:c0 + ff_chunk])
        act = jnp.square(jnp.maximum(up, 0.0)).astype(BF16)
        acc = acc + _dot(act, wd_ref[c0:c0 + ff_chunk, :])
    out_ref[...] = acc


def _merge_mlp(oa, ob, proj, x2d, pa, pb, wo, ng, wu, wd, *, tm, col_gate_a, col_gate_b):
    m, d = x2d.shape
    d_ff = wu.shape[1]
    tile = lambda col: pl.BlockSpec((tm, d), lambda i: (i, col))
    const = lambda shape: pl.BlockSpec(shape, lambda i: (0, 0), pipeline_mode=pl.Buffered(1))
    return pl.pallas_call(
        functools.partial(_merge_mlp_kernel, ff_chunk=1024),
        grid=(m // tm,),
        in_specs=[
            tile(0), tile(0), tile(col_gate_a), tile(col_gate_b), tile(0),
            const((d, d)), const((d, d)), const((d, d)), const((1, d)),
            const((d, d_ff)), const((d_ff, d)),
        ],
        out_specs=tile(0),
        out_shape=jax.ShapeDtypeStruct((m, d), F32),
        compiler_params=pltpu.CompilerParams(
            dimension_semantics=("arbitrary",),
            vmem_limit_bytes=VMEM_LIMIT_BYTES),
        name="merge_mlp",
    )(oa, ob, proj, proj, x2d, pa, pb, wo, ng, wu, wd)


def _lane_row(values, offset):
    row = jnp.zeros((1, LANES), F32)
    return row.at[0, offset:offset + values.shape[0]].set(values.astype(F32))


def kernel(x, norm_mix_g, w_in, gdn_conv_w, gdn_a_log, gdn_dt_bias, gdn_norm_g, fox_q_norm_g,
           fox_k_norm_g, fox_f_bias, w_proj_gdn, w_proj_fox, w_out, norm_mlp_g, w_up, w_down):
    b, t, d = x.shape
    depth = w_in.shape[0]
    qk_w = GDN_HEADS * GDN_D
    fox_w = FOX_HEADS * FOX_D
    o_ga = 4 * qk_w
    o_gb = o_ga + GDN_HEADS
    o_fq = o_gb + GDN_HEADS
    o_ff = o_fq + 3 * fox_w
    o_gate = o_ff + FOX_HEADS
    blocks = lambda width: width // LANES
    cb_fox = blocks(4 * qk_w)
    cb_gate = cb_fox + blocks(3 * fox_w)

    for l in range(depth):
        w = w_in[l]
        w_main = jnp.concatenate([w[:, :o_ga], w[:, o_fq:o_ff], w[:, o_gate:]], axis=1).astype(BF16)
        w_small = jnp.concatenate(
            [w[:, o_ga:o_fq], w[:, o_ff:o_gate],
             jnp.zeros((d, LANES - 2 * GDN_HEADS - FOX_HEADS), F32)], axis=1)
        ws_hi = w_small.astype(BF16)
        ws_lo = (w_small - ws_hi.astype(F32)).astype(BF16)

        x2d = x.reshape(b * t, d)
        proj, small = _in_projection(x2d, norm_mix_g[l][None, :], w_main, ws_hi, ws_lo,
                                     tm=512, tn=3072)
        proj3 = proj.reshape(b, t, -1)
        small3 = small.reshape(b, t, LANES)

        o_a = _gdn_branch(
            proj3, small3, gdn_conv_w[l],
            _lane_row(gdn_a_log[l], SMALL_GA), _lane_row(gdn_dt_bias[l], SMALL_GA),
            gdn_norm_g[l][None, :],
            col_q=0, col_k=GDN_HEADS, col_v=2 * GDN_HEADS, col_z=3 * GDN_HEADS)
        o_b = _fox_branch(
            proj3, small3, _lane_row(fox_f_bias[l], SMALL_FF),
            fox_q_norm_g[l][None, :], fox_k_norm_g[l][None, :],
            col_q=cb_fox, col_k=cb_fox + FOX_HEADS, col_v=cb_fox + 2 * FOX_HEADS)

        out = _merge_mlp(
            o_a.reshape(b * t, qk_w), o_b.reshape(b * t, fox_w), proj, x2d,
            w_proj_gdn[l].astype(BF16), w_proj_fox[l].astype(BF16), w_out[l].astype(BF16),
            norm_mlp_g[l][None, :], w_up[l].astype(BF16), w_down[l].astype(BF16),
            tm=512, col_gate_a=cb_gate // blocks(d), col_gate_b=cb_gate // blocks(d) + 1)
        x = out.reshape(b, t, d)
    return x
```

```python
import functools

import jax
import jax.numpy as jnp
from jax import lax
from jax.experimental import pallas as pl
from jax.experimental.pallas import tpu as pltpu

F32 = jnp.float32
BF16 = jnp.bfloat16

LANES = 128
VMEM_LIMIT_BYTES = 56 * 1024 * 1024

EPS = 1e-6
GDN_HEADS = 8
GDN_D = 128
GDN_CONV = 4
GDN_CHUNK = 64
GDN_INV_BLOCK = 16
FOX_HEADS = 8
FOX_D = 128
FOX_TQ = 256
SMALL_GA, SMALL_GB, SMALL_FF = 0, 8, 16
NEG_BIG = -1e30


def _dot(a, b):
    return jnp.dot(a, b, preferred_element_type=F32)


def _dot_nt(a, b):
    return lax.dot_general(a, b, (((1,), (1,)), ((), ())), preferred_element_type=F32)


def _sigmoid(x):
    return 1.0 / (1.0 + jnp.exp(-x))


def _softplus(x):
    return jnp.maximum(x, 0.0) + jnp.log1p(jnp.exp(-jnp.abs(x)))


def _rmsnorm(x, g):
    return x * lax.rsqrt(jnp.mean(x * x, axis=-1, keepdims=True) + EPS) * g


def _pick_lane(x, lane):
    ids = lax.broadcasted_iota(jnp.int32, x.shape, 1)
    col = jnp.sum(jnp.where(ids == lane, x, 0.0), axis=-1, keepdims=True)
    return jnp.broadcast_to(col, x.shape)


def _segment_cumsum(x, seg):
    pos = lax.broadcasted_iota(jnp.int32, x.shape, 0) % seg
    s = 1
    while s < seg:
        x = x + jnp.where(pos >= s, pltpu.roll(x, s, axis=0), 0.0)
        s *= 2
    return x


def _proj_kernel(x_ref, g_ref, w_ref, out_ref):
    u = _rmsnorm(x_ref[...], g_ref[...])
    out_ref[...] = _dot(u.astype(BF16), w_ref[...])


def _in_projection(x2d, gain, w_main, *, tm, tn):
    m, d = x2d.shape
    n = w_main.shape[1]
    return pl.pallas_call(
        _proj_kernel,
        grid=(n // tn, m // tm),
        in_specs=[
            pl.BlockSpec((tm, d), lambda j, i: (i, 0)),
            pl.BlockSpec((1, d), lambda j, i: (0, 0)),
            pl.BlockSpec((d, tn), lambda j, i: (0, j)),
        ],
        out_specs=pl.BlockSpec((tm, tn), lambda j, i: (i, j)),
        out_shape=jax.ShapeDtypeStruct((m, n), F32),
        compiler_params=pltpu.CompilerParams(
            dimension_semantics=("arbitrary", "arbitrary"),
            vmem_limit_bytes=VMEM_LIMIT_BYTES),
        name="in_projection",
    )(x2d, gain, w_main)


def _gate_proj_kernel(x_ref, g_ref, ws_hi_ref, ws_lo_ref, alog_ref, bias_ref, out_ref):
    u = _rmsnorm(x_ref[...], g_ref[...])
    u_hi = u.astype(BF16)
    u_lo = (u - u_hi.astype(F32)).astype(BF16)
    ws_hi = ws_hi_ref[...]
    logits = _dot(u_hi, ws_hi) + _dot(u_lo, ws_hi) + _dot(u_hi, ws_lo_ref[...])
    shifted = logits + bias_ref[...]
    log_decay = -jnp.exp(alog_ref[...]) * _softplus(shifted)
    beta = _sigmoid(logits)
    log_forget = -_softplus(-shifted)
    lane = lax.broadcasted_iota(jnp.int32, logits.shape, 1)
    out_ref[...] = jnp.where(lane < SMALL_GB, log_decay, jnp.where(lane < SMALL_FF, beta, log_forget))


def _gate_projection(x2d, gain, ws_hi, ws_lo, alog_row, bias_row, *, tm):
    m, d = x2d.shape
    const = lambda shape: pl.BlockSpec(shape, lambda i: (0, 0))
    return pl.pallas_call(
        _gate_proj_kernel,
        grid=(m // tm,),
        in_specs=[
            pl.BlockSpec((tm, d), lambda i: (i, 0)),
            const((1, d)), const((d, LANES)), const((d, LANES)), const((1, LANES)), const((1, LANES)),
        ],
        out_specs=pl.BlockSpec((tm, LANES), lambda i: (i, 0)),
        out_shape=jax.ShapeDtypeStruct((m, LANES), F32),
        compiler_params=pltpu.CompilerParams(
            dimension_semantics=("arbitrary",),
            vmem_limit_bytes=VMEM_LIMIT_BYTES),
        name="gate_projection",
    )(x2d, gain, ws_hi, ws_lo, alog_row, bias_row)


def _conv_silu(x, w):
    row = lax.broadcasted_iota(jnp.int32, x.shape, 0)
    y = None
    for i in range(GDN_CONV):
        s = GDN_CONV - 1 - i
        xs = x if s == 0 else jnp.where(row >= s, pltpu.roll(x, s, axis=0), 0.0)
        term = xs * w[i:i + 1, :]
        y = term if y is None else y + term
    return y * _sigmoid(y)


def _l2norm(x):
    return x * lax.rsqrt(jnp.sum(x * x, axis=-1, keepdims=True) + EPS)


def _unit_lower_inverse(a, eye, same_block):
    d = jnp.where(same_block, a, 0.0)
    off = a - d
    x = eye - d
    p = _dot(d.astype(BF16), d.astype(BF16))
    yield
    for step in range(3):
        p_bf = p.astype(BF16)
        x = x + _dot(x.astype(BF16), p_bf)
        if step < 2:
            p = _dot(p_bf, p_bf)
        yield
    x_bf = x.astype(BF16)
    m = _dot(x_bf, off.astype(BF16))
    yield
    m_bf = m.astype(BF16)
    m2 = _dot(m_bf, m_bf)
    yield
    im = eye - m
    y = im + _dot(im.astype(BF16), m2.astype(BF16))
    yield
    return _dot(y.astype(BF16), x_bf)


def _run_in_lockstep(generators):
    results = [None] * len(generators)
    live = list(enumerate(generators))
    while live:
        still_live = []
        for idx, gen in live:
            try:
                next(gen)
                still_live.append((idx, gen))
            except StopIteration as stop:
                results[idx] = stop.value
        live = still_live
    return results


def _gdn_kernel(q_ref, k_ref, v_ref, z_ref, sm_ref, cwq_ref, cwk_ref, cwv_ref, ng_ref, o_ref,
                qn_s, kn_s, vc_s, gcum_s, beta_s, pm_s, qm_s, r_s, u_s,
                *, unroll_a, unroll_b):
    h = pl.program_id(1)
    t = q_ref.shape[1]
    c = GDN_CHUNK
    n_chunks = t // c

    qn_s[...] = _l2norm(_conv_silu(q_ref[0], cwq_ref[...])) * (GDN_D ** -0.5)
    kn_s[...] = _l2norm(_conv_silu(k_ref[0], cwk_ref[...]))
    vc_s[...] = _conv_silu(v_ref[0], cwv_ref[...])
    sm = sm_ref[0]
    gcum_s[...] = _segment_cumsum(_pick_lane(sm, SMALL_GA + h), c)
    beta_s[...] = _pick_lane(sm, SMALL_GB + h)

    ri = lax.broadcasted_iota(jnp.int32, (c, c), 0)
    ci = lax.broadcasted_iota(jnp.int32, (c, c), 1)
    tri_incl = ri >= ci
    tri_strict = ri > ci
    same_block = (ri // GDN_INV_BLOCK) == (ci // GDN_INV_BLOCK)
    eye = jnp.where(ri == ci, 1.0, 0.0).astype(F32)

    def chunk_prepare(q, k, v, gb, bb):
        g_row = jnp.transpose(gb)[:c, :]
        diff = gb[:, :c] - g_row
        decay = jnp.where(tri_incl, jnp.exp(jnp.where(tri_incl, diff, 0.0)), 0.0)
        k_bf = k.astype(BF16)
        qk_kk = _dot_nt(jnp.concatenate([q.astype(BF16), k_bf], axis=0), k_bf)
        yield
        qk = qk_kk[:c, :]
        kk = qk_kk[c:, :]
        a_mat = jnp.where(tri_strict, bb[:, :c] * kk * decay, 0.0)
        t_inv = yield from _unit_lower_inverse(a_mat, eye, same_block)
        e_g = jnp.exp(gb)
        rhs = jnp.concatenate([v * bb, k * (bb * e_g)], axis=1).astype(BF16)
        sol = _dot(t_inv.astype(BF16), rhs).astype(BF16)
        yield
        g_last = gb[c - 1:c, :]
        k_dec = k * jnp.exp(g_last - gb)
        attn = qk * decay
        lhs = jnp.concatenate([jnp.transpose(k_dec), attn], axis=0).astype(BF16)
        fused = _dot(lhs, sol)
        q_mat = fused[:GDN_D, :GDN_D]
        p_mat = fused[:GDN_D, GDN_D:]
        u_mat = fused[GDN_D:, :GDN_D]
        r_mat = q * e_g - fused[GDN_D:, GDN_D:]
        return p_mat.astype(BF16), q_mat, r_mat.astype(BF16), u_mat

    def prepare_body(i, carry):
        chunks = [i * unroll_a + u for u in range(unroll_a)]
        rows = [pl.ds(pl.multiple_of(ch * c, c), c) for ch in chunks]
        loaded = [(qn_s[r, :], kn_s[r, :], vc_s[r, :], gcum_s[r, :], beta_s[r, :]) for r in rows]
        results = _run_in_lockstep([chunk_prepare(*args) for args in loaded])
        for ch, r, (p_mat, q_mat, r_mat, u_mat) in zip(chunks, rows, results):
            pm_s[ch] = p_mat
            qm_s[ch] = q_mat
            r_s[r, :] = r_mat
            u_s[r, :] = u_mat
        return carry

    lax.fori_loop(0, n_chunks // unroll_a, prepare_body, 0)

    ng = ng_ref[...]

    def scan_body(i, s):
        for u in range(unroll_b):
            ch = i * unroll_b + u
            r0 = pl.multiple_of(ch * c, c)
            rows = pl.ds(r0, c)
            s_bf = s.astype(BF16)
            o = _dot(r_s[rows, :], s_bf) + u_s[rows, :]
            dec = jnp.exp(gcum_s[pl.ds(r0 + c - 1, 1), :])
            s = s * dec - _dot(pm_s[ch], s_bf) + qm_s[ch]
            z = z_ref[0, rows, :]
            o_ref[0, rows, :] = (_rmsnorm(o, ng) * (z * _sigmoid(z))).astype(o_ref.dtype)
        return s

    lax.fori_loop(0, n_chunks // unroll_b, scan_body, jnp.zeros((GDN_D, GDN_D), F32))


def _gdn_branch(proj3, small3, conv_w, norm_g, *, col_q, col_k, col_v, col_z):
    b, t, _ = proj3.shape
    d = GDN_D
    n_chunks = t // GDN_CHUNK
    seq_spec = lambda col: pl.BlockSpec((1, t, d), lambda bi, hi: (bi, 0, col + hi))
    cw_spec = lambda col: pl.BlockSpec((GDN_CONV, d), lambda bi, hi: (0, col + hi))
    row_spec = pl.BlockSpec((1, LANES), lambda bi, hi: (0, 0))
    return pl.pallas_call(
        functools.partial(_gdn_kernel, unroll_a=16, unroll_b=2),
        grid=(b, GDN_HEADS),
        in_specs=[
            seq_spec(col_q), seq_spec(col_k), seq_spec(col_v), seq_spec(col_z),
            pl.BlockSpec((1, t, LANES), lambda bi, hi: (bi, 0, 0)),
            cw_spec(0), cw_spec(GDN_HEADS), cw_spec(2 * GDN_HEADS),
            row_spec,
        ],
        out_specs=pl.BlockSpec((1, t, d), lambda bi, hi: (bi, 0, hi)),
        out_shape=jax.ShapeDtypeStruct((b, t, GDN_HEADS * d), BF16),
        scratch_shapes=[
            pltpu.VMEM((t, d), F32),
            pltpu.VMEM((t, d), F32),
            pltpu.VMEM((t, d), F32),
            pltpu.VMEM((t, LANES), F32),
            pltpu.VMEM((t, LANES), F32),
            pltpu.VMEM((n_chunks, d, d), BF16),
            pltpu.VMEM((n_chunks, d, d), F32),
            pltpu.VMEM((t, d), BF16),
            pltpu.VMEM((t, d), F32),
        ],
        compiler_params=pltpu.CompilerParams(
            dimension_semantics=("arbitrary", "arbitrary"),
            vmem_limit_bytes=VMEM_LIMIT_BYTES),
        name="gdn_branch",
    )(proj3, proj3, proj3, proj3, small3, conv_w, conv_w, conv_w, norm_g)


def _fox_kernel(q_ref, k_ref, v_ref, sm_ref, gq_ref, gk_ref, o_ref,
                qn_s, kn_s, vb_s, ccol_s, crow_s, m_s, l_s, acc_s):
    h = pl.program_id(1)
    t = q_ref.shape[1]
    tq = FOX_TQ
    nq = t // tq

    qn_s[...] = (_rmsnorm(q_ref[0], gq_ref[...]) * (FOX_D ** -0.5)).astype(BF16)
    kn_s[...] = _rmsnorm(k_ref[0], gk_ref[...]).astype(BF16)
    vb_s[...] = v_ref[0].astype(BF16)
    cum = _segment_cumsum(_pick_lane(sm_ref[0], SMALL_FF + h), t)
    ccol_s[...] = cum
    cum_t = jnp.transpose(cum)
    for j in range(nq):
        crow_s[j] = cum_t[:8, j * tq:(j + 1) * tq]

    ri = lax.broadcasted_iota(jnp.int32, (tq, tq), 0)
    ci = lax.broadcasted_iota(jnp.int32, (tq, tq), 1)
    causal = ri >= ci

    def kv_step(qi, kj, masked):
        q0 = pl.multiple_of(qi * tq, tq)
        k0 = pl.multiple_of(kj * tq, tq)
        q = qn_s[pl.ds(q0, tq), :]
        k = kn_s[pl.ds(k0, tq), :]
        v = vb_s[pl.ds(k0, tq), :]
        cq = ccol_s[pl.ds(q0, tq), :]
        ck = crow_s[kj][0:1, :]
        s = _dot_nt(q, k) + jnp.concatenate([cq] * (tq // LANES), axis=1) - ck
        if masked:
            s = jnp.where(causal, s, NEG_BIG)
        m_prev = m_s[...]
        m_new = jnp.maximum(m_prev, jnp.max(s, axis=-1, keepdims=True))
        alpha = jnp.exp(m_prev - m_new)
        p = jnp.exp(s - jnp.concatenate([m_new] * (tq // LANES), axis=1))
        l_s[...] = alpha * l_s[...] + jnp.sum(p, axis=-1, keepdims=True)
        acc_s[...] = alpha * acc_s[...] + _dot(p.astype(BF16), v)
        m_s[...] = m_new

    def q_body(qi, carry):
        m_s[...] = jnp.full(m_s.shape, NEG_BIG, F32)
        l_s[...] = jnp.zeros(l_s.shape, F32)
        acc_s[...] = jnp.zeros(acc_s.shape, F32)

        def k_body(kj, c2):
            kv_step(qi, kj, masked=False)
            return c2

        lax.fori_loop(0, qi, k_body, 0)
        kv_step(qi, qi, masked=True)
        q0 = pl.multiple_of(qi * tq, tq)
        o_ref[0, pl.ds(q0, tq), :] = (acc_s[...] / l_s[...]).astype(o_ref.dtype)
        return carry

    lax.fori_loop(0, nq, q_body, 0)


def _fox_branch(proj3, small3, gq, gk, *, col_q, col_k, col_v):
    b, t, _ = proj3.shape
    d = FOX_D
    nq = t // FOX_TQ
    seq_spec = lambda col: pl.BlockSpec((1, t, d), lambda bi, hi: (bi, 0, col + hi))
    row_spec = pl.BlockSpec((1, LANES), lambda bi, hi: (0, 0))
    return pl.pallas_call(
        _fox_kernel,
        grid=(b, FOX_HEADS),
        in_specs=[
            seq_spec(col_q), seq_spec(col_k), seq_spec(col_v),
            pl.BlockSpec((1, t, LANES), lambda bi, hi: (bi, 0, 0)),
            row_spec, row_spec,
        ],
        out_specs=pl.BlockSpec((1, t, d), lambda bi, hi: (bi, 0, hi)),
        out_shape=jax.ShapeDtypeStruct((b, t, FOX_HEADS * d), BF16),
        scratch_shapes=[
            pltpu.VMEM((t, d), BF16),
            pltpu.VMEM((t, d), BF16),
            pltpu.VMEM((t, d), BF16),
            pltpu.VMEM((t, LANES), F32),
            pltpu.VMEM((nq, 8, FOX_TQ), F32),
            pltpu.VMEM((FOX_TQ, LANES), F32),
            pltpu.VMEM((FOX_TQ, LANES), F32),
            pltpu.VMEM((FOX_TQ, d), F32),
        ],
        compiler_params=pltpu.CompilerParams(
            dimension_semantics=("arbitrary", "arbitrary"),
            vmem_limit_bytes=VMEM_LIMIT_BYTES),
        name="fox_branch",
    )(proj3, proj3, proj3, small3, gq, gk)


def _merge_mlp_kernel(oa_ref, ob_ref, ga_ref, gb_ref, x_ref, pa_ref, pb_ref, wo_ref, ng_ref,
                      wu_ref, wd_ref, out_ref, *, ff_chunk):
    ya = _dot(oa_ref[...], pa_ref[...])
    yb = _dot(ob_ref[...], pb_ref[...])
    merged = _sigmoid(ga_ref[...]) * ya + _sigmoid(gb_ref[...]) * yb
    hid = x_ref[...] + _dot(merged.astype(BF16), wo_ref[...])
    hn = _rmsnorm(hid, ng_ref[...]).astype(BF16)
    acc = hid
    d_ff = wu_ref.shape[1]
    for c0 in range(0, d_ff, ff_chunk):
        up = _dot(hn, wu_ref[:, c0:c0 + ff_chunk])
        act = jnp.square(jnp.maximum(up, 0.0)).astype(BF16)
        acc = acc + _dot(act, wd_ref[c0:c0 + ff_chunk, :])
    out_ref[...] = acc


def _merge_mlp(oa, ob, proj, x2d, pa, pb, wo, ng, wu, wd, *, tm, col_gate_a, col_gate_b):
    m, d = x2d.shape
    d_ff = wu.shape[1]
    tile = lambda col: pl.BlockSpec((tm, d), lambda i: (i, col))
    const = lambda shape: pl.BlockSpec(shape, lambda i: (0, 0), pipeline_mode=pl.Buffered(1))
    return pl.pallas_call(
        functools.partial(_merge_mlp_kernel, ff_chunk=1024),
        grid=(m // tm,),
        in_specs=[
            tile(0), tile(0), tile(col_gate_a), tile(col_gate_b), tile(0),
            const((d, d)), const((d, d)), const((d, d)), const((1, d)),
            const((d, d_ff)), const((d_ff, d)),
        ],
        out_specs=tile(0),
        out_shape=jax.ShapeDtypeStruct((m, d), F32),
        compiler_params=pltpu.CompilerParams(
            dimension_semantics=("arbitrary",),
            vmem_limit_bytes=VMEM_LIMIT_BYTES),
        name="merge_mlp",
    )(oa, ob, proj, proj, x2d, pa, pb, wo, ng, wu, wd)


def _lane_row(values, offset):
    row = jnp.zeros((1, LANES), F32)
    return row.at[0, offset:offset + values.shape[0]].set(values.astype(F32))


def kernel(x, norm_mix_g, w_in, gdn_conv_w, gdn_a_log, gdn_dt_bias, gdn_norm_g, fox_q_norm_g,
           fox_k_norm_g, fox_f_bias, w_proj_gdn, w_proj_fox, w_out, norm_mlp_g, w_up, w_down):
    b, t, d = x.shape
    depth = w_in.shape[0]
    qk_w = GDN_HEADS * GDN_D
    fox_w = FOX_HEADS * FOX_D
    o_ga = 4 * qk_w
    o_gb = o_ga + GDN_HEADS
    o_fq = o_gb + GDN_HEADS
    o_ff = o_fq + 3 * fox_w
    o_gate = o_ff + FOX_HEADS
    blocks = lambda width: width // LANES
    cb_fox = blocks(4 * qk_w)
    cb_gate = cb_fox + blocks(3 * fox_w)

    for l in range(depth):
        w = w_in[l]
        w_main = jnp.concatenate([w[:, :o_ga], w[:, o_fq:o_ff], w[:, o_gate:]], axis=1).astype(BF16)
        w_small = jnp.concatenate(
            [w[:, o_ga:o_fq], w[:, o_ff:o_gate],
             jnp.zeros((d, LANES - 2 * GDN_HEADS - FOX_HEADS), F32)], axis=1)
        ws_hi = w_small.astype(BF16)
        ws_lo = (w_small - ws_hi.astype(F32)).astype(BF16)

        x2d = x.reshape(b * t, d)
        gain = norm_mix_g[l][None, :]
        proj = _in_projection(x2d, gain, w_main, tm=512, tn=3072)
        bias_row = _lane_row(gdn_dt_bias[l], SMALL_GA) + _lane_row(fox_f_bias[l], SMALL_FF)
        small = _gate_projection(x2d, gain, ws_hi, ws_lo, _lane_row(gdn_a_log[l], SMALL_GA), bias_row,
                                 tm=1024)
        proj3 = proj.reshape(b, t, -1)
        small3 = small.reshape(b, t, LANES)

        o_a = _gdn_branch(
            proj3, small3, gdn_conv_w[l], gdn_norm_g[l][None, :],
            col_q=0, col_k=GDN_HEADS, col_v=2 * GDN_HEADS, col_z=3 * GDN_HEADS)
        o_b = _fox_branch(
            proj3, small3, fox_q_norm_g[l][None, :], fox_k_norm_g[l][None, :],
            col_q=cb_fox, col_k=cb_fox + FOX_HEADS, col_v=cb_fox + 2 * FOX_HEADS)

        out = _merge_mlp(
            o_a.reshape(b * t, qk_w), o_b.reshape(b * t, fox_w), proj, x2d,
            w_proj_gdn[l].astype(BF16), w_proj_fox[l].astype(BF16), w_out[l].astype(BF16),
            norm_mlp_g[l][None, :], w_up[l].astype(BF16), w_down[l].astype(BF16),
            tm=512, col_gate_a=cb_gate // blocks(d), col_gate_b=cb_gate // blocks(d) + 1)
        x = out.reshape(b, t, d)
    return x
```

```python
import functools

import jax
import jax.numpy as jnp
from jax import lax
from jax.experimental import pallas as pl
from jax.experimental.pallas import tpu as pltpu

F32 = jnp.float32
BF16 = jnp.bfloat16

LANES = 128
VMEM_LIMIT_BYTES = 56 * 1024 * 1024

EPS = 1e-6
GDN_HEADS = 8
GDN_D = 128
GDN_CONV = 4
GDN_CHUNK = 64
GDN_INV_BLOCK = 16
FOX_HEADS = 8
FOX_D = 128
FOX_TQ = 256
SMALL_GA, SMALL_GB, SMALL_FF = 0, 8, 16
NEG_BIG = -1e30


def _dot(a, b):
    return jnp.dot(a, b, preferred_element_type=F32)


def _dot_nt(a, b):
    return lax.dot_general(a, b, (((1,), (1,)), ((), ())), preferred_element_type=F32)


def _sigmoid(x):
    return 1.0 / (1.0 + jnp.exp(-x))


def _softplus(x):
    return jnp.maximum(x, 0.0) + jnp.log1p(jnp.exp(-jnp.abs(x)))


def _rmsnorm(x, g):
    return x * lax.rsqrt(jnp.mean(x * x, axis=-1, keepdims=True) + EPS) * g


def _pick_lane(x, lane):
    ids = lax.broadcasted_iota(jnp.int32, x.shape, 1)
    col = jnp.sum(jnp.where(ids == lane, x, 0.0), axis=-1, keepdims=True)
    return jnp.broadcast_to(col, x.shape)


def _segment_cumsum(x, seg):
    pos = lax.broadcasted_iota(jnp.int32, x.shape, 0) % seg
    s = 1
    while s < seg:
        x = x + jnp.where(pos >= s, pltpu.roll(x, s, axis=0), 0.0)
        s *= 2
    return x


def _proj_kernel(x_ref, g_ref, w_ref, out_ref):
    u = _rmsnorm(x_ref[...], g_ref[...])
    out_ref[...] = _dot(u.astype(BF16), w_ref[...])


def _in_projection(x2d, gain, w_main, *, tm, tn):
    m, d = x2d.shape
    n = w_main.shape[1]
    return pl.pallas_call(
        _proj_kernel,
        grid=(n // tn, m // tm),
        in_specs=[
            pl.BlockSpec((tm, d), lambda j, i: (i, 0)),
            pl.BlockSpec((1, d), lambda j, i: (0, 0)),
            pl.BlockSpec((d, tn), lambda j, i: (0, j)),
        ],
        out_specs=pl.BlockSpec((tm, tn), lambda j, i: (i, j)),
        out_shape=jax.ShapeDtypeStruct((m, n), F32),
        compiler_params=pltpu.CompilerParams(
            dimension_semantics=("arbitrary", "arbitrary"),
            vmem_limit_bytes=VMEM_LIMIT_BYTES),
        name="in_projection",
    )(x2d, gain, w_main)


def _gate_proj_kernel(x_ref, g_ref, ws_hi_ref, ws_lo_ref, alog_ref, bias_ref, out_ref, out_t_ref, loc_s):
    t = x_ref.shape[0]
    n_chunks = t // GDN_CHUNK
    u = _rmsnorm(x_ref[...], g_ref[...])
    u_hi = u.astype(BF16)
    u_lo = (u - u_hi.astype(F32)).astype(BF16)
    ws_hi = ws_hi_ref[...]
    logits = _dot(u_hi, ws_hi) + _dot(u_lo, ws_hi) + _dot(u_hi, ws_lo_ref[...])
    shifted = logits + bias_ref[...]
    log_decay = -jnp.exp(alog_ref[...]) * _softplus(shifted)
    beta = _sigmoid(logits)
    log_forget = -_softplus(-shifted)
    lane = lax.broadcasted_iota(jnp.int32, logits.shape, 1)
    vals = jnp.where(lane < SMALL_GB, log_decay, jnp.where(lane < SMALL_FF, beta, log_forget))
    local = _segment_cumsum(vals, GDN_CHUNK)
    loc_s[...] = local
    totals = loc_s[pl.ds(GDN_CHUNK - 1, n_chunks, stride=GDN_CHUNK), :]
    carried = _segment_cumsum(totals, n_chunks) - totals
    full = local + jnp.broadcast_to(carried[:, None, :], (n_chunks, GDN_CHUNK, LANES)).reshape(t, LANES)
    out = jnp.where(lane < SMALL_GB, local, jnp.where(lane < SMALL_FF, vals, full))
    out_ref[0] = out
    out_t_ref[0] = jnp.transpose(out)


def _gate_projection(x3, gain, ws_hi, ws_lo, alog_row, bias_row):
    b, t, d = x3.shape
    const = lambda shape: pl.BlockSpec(shape, lambda i: (0, 0))
    return pl.pallas_call(
        _gate_proj_kernel,
        grid=(b,),
        in_specs=[
            pl.BlockSpec((None, t, d), lambda i: (i, 0, 0)),
            const((1, d)), const((d, LANES)), const((d, LANES)), const((1, LANES)), const((1, LANES)),
        ],
        out_specs=[
            pl.BlockSpec((1, t, LANES), lambda i: (i, 0, 0)),
            pl.BlockSpec((1, LANES, t), lambda i: (i, 0, 0)),
        ],
        out_shape=[
            jax.ShapeDtypeStruct((b, t, LANES), F32),
            jax.ShapeDtypeStruct((b, LANES, t), F32),
        ],
        scratch_shapes=[pltpu.VMEM((t, LANES), F32)],
        compiler_params=pltpu.CompilerParams(
            dimension_semantics=("arbitrary",),
            vmem_limit_bytes=VMEM_LIMIT_BYTES),
        name="gate_projection",
    )(x3, gain, ws_hi, ws_lo, alog_row, bias_row)


def _conv_silu(x, w):
    row = lax.broadcasted_iota(jnp.int32, x.shape, 0)
    y = None
    for i in range(GDN_CONV):
        s = GDN_CONV - 1 - i
        xs = x if s == 0 else jnp.where(row >= s, pltpu.roll(x, s, axis=0), 0.0)
        term = xs * w[i:i + 1, :]
        y = term if y is None else y + term
    return y * _sigmoid(y)


def _l2norm(x):
    return x * lax.rsqrt(jnp.sum(x * x, axis=-1, keepdims=True) + EPS)


def _unit_lower_inverse(a, eye, same_block):
    d = jnp.where(same_block, a, 0.0)
    off = a - d
    x = eye - d
    p = _dot(d.astype(BF16), d.astype(BF16))
    yield
    for step in range(3):
        p_bf = p.astype(BF16)
        x = x + _dot(x.astype(BF16), p_bf)
        if step < 2:
            p = _dot(p_bf, p_bf)
        yield
    x_bf = x.astype(BF16)
    m = _dot(x_bf, off.astype(BF16))
    yield
    m_bf = m.astype(BF16)
    m2 = _dot(m_bf, m_bf)
    yield
    im = eye - m
    y = im + _dot(im.astype(BF16), m2.astype(BF16))
    yield
    return _dot(y.astype(BF16), x_bf)


def _run_in_lockstep(generators):
    results = [None] * len(generators)
    live = list(enumerate(generators))
    while live:
        still_live = []
        for idx, gen in live:
            try:
                next(gen)
                still_live.append((idx, gen))
            except StopIteration as stop:
                results[idx] = stop.value
        live = still_live
    return results


def _gdn_kernel(q_ref, k_ref, v_ref, z_ref, sm_ref, gt_ref, cwq_ref, cwk_ref, cwv_ref, ng_ref, o_ref,
                qn_s, kn_s, vc_s, gcum_s, beta_s, pm_s, qm_s, r_s, u_s,
                *, unroll_a, unroll_b):
    h = pl.program_id(1)
    t = q_ref.shape[1]
    c = GDN_CHUNK
    n_chunks = t // c

    qn_s[...] = _l2norm(_conv_silu(q_ref[0], cwq_ref[...])) * (GDN_D ** -0.5)
    kn_s[...] = _l2norm(_conv_silu(k_ref[0], cwk_ref[...]))
    vc_s[...] = _conv_silu(v_ref[0], cwv_ref[...])
    sm = sm_ref[0]
    gcum_s[...] = _pick_lane(sm, SMALL_GA + h)
    beta_s[...] = _pick_lane(sm, SMALL_GB + h)

    ri = lax.broadcasted_iota(jnp.int32, (c, c), 0)
    ci = lax.broadcasted_iota(jnp.int32, (c, c), 1)
    tri_incl = ri >= ci
    tri_strict = ri > ci
    same_block = (ri // GDN_INV_BLOCK) == (ci // GDN_INV_BLOCK)
    eye = jnp.where(ri == ci, 1.0, 0.0).astype(F32)

    def chunk_prepare(q, k, v, gb, bb, g_row):
        diff = gb[:, :c] - g_row
        decay = jnp.where(tri_incl, jnp.exp(jnp.where(tri_incl, diff, 0.0)), 0.0)
        k_bf = k.astype(BF16)
        qk_kk = _dot_nt(jnp.concatenate([q.astype(BF16), k_bf], axis=0), k_bf)
        yield
        qk = qk_kk[:c, :]
        kk = qk_kk[c:, :]
        a_mat = jnp.where(tri_strict, bb[:, :c] * kk * decay, 0.0)
        t_inv = yield from _unit_lower_inverse(a_mat, eye, same_block)
        e_g = jnp.exp(gb)
        rhs = jnp.concatenate([v * bb, k * (bb * e_g)], axis=1).astype(BF16)
        sol = _dot(t_inv.astype(BF16), rhs).astype(BF16)
        yield
        g_last = gb[c - 1:c, :]
        k_dec = k * jnp.exp(g_last - gb)
        attn = qk * decay
        lhs = jnp.concatenate([jnp.transpose(k_dec), attn], axis=0).astype(BF16)
        fused = _dot(lhs, sol)
        q_mat = fused[:GDN_D, :GDN_D]
        p_mat = fused[:GDN_D, GDN_D:]
        u_mat = fused[GDN_D:, :GDN_D]
        r_mat = q * e_g - fused[GDN_D:, GDN_D:]
        return p_mat.astype(BF16), q_mat, r_mat.astype(BF16), u_mat

    def prepare_body(i, carry):
        chunks = [i * unroll_a + u for u in range(unroll_a)]
        rows = [pl.ds(pl.multiple_of(ch * c, c), c) for ch in chunks]
        span = unroll_a * c
        g_rows = gt_ref[0, pl.ds(h, 1), pl.ds(pl.multiple_of(i * span, span), span)]
        loaded = [(qn_s[r, :], kn_s[r, :], vc_s[r, :], gcum_s[r, :], beta_s[r, :],
                   g_rows[:, u * c:(u + 1) * c]) for u, r in enumerate(rows)]
        results = _run_in_lockstep([chunk_prepare(*args) for args in loaded])
        for ch, r, (p_mat, q_mat, r_mat, u_mat) in zip(chunks, rows, results):
            pm_s[ch] = p_mat
            qm_s[ch] = q_mat
            r_s[r, :] = r_mat
            u_s[r, :] = u_mat
        return carry

    lax.fori_loop(0, n_chunks // unroll_a, prepare_body, 0)

    ng = ng_ref[...]

    def scan_body(i, s):
        for u in range(unroll_b):
            ch = i * unroll_b + u
            r0 = pl.multiple_of(ch * c, c)
            rows = pl.ds(r0, c)
            s_bf = s.astype(BF16)
            o = _dot(r_s[rows, :], s_bf) + u_s[rows, :]
            dec = jnp.exp(gcum_s[pl.ds(r0 + c - 1, 1), :])
            s = s * dec - _dot(pm_s[ch], s_bf) + qm_s[ch]
            z = z_ref[0, rows, :]
            o_ref[0, rows, :] = (_rmsnorm(o, ng) * (z * _sigmoid(z))).astype(o_ref.dtype)
        return s

    lax.fori_loop(0, n_chunks // unroll_b, scan_body, jnp.zeros((GDN_D, GDN_D), F32))


def _gdn_branch(proj3, small3, small_t, conv_w, norm_g, *, col_q, col_k, col_v, col_z):
    b, t, _ = proj3.shape
    d = GDN_D
    n_chunks = t // GDN_CHUNK
    seq_spec = lambda col: pl.BlockSpec((1, t, d), lambda bi, hi: (bi, 0, col + hi))
    cw_spec = lambda col: pl.BlockSpec((GDN_CONV, d), lambda bi, hi: (0, col + hi))
    row_spec = pl.BlockSpec((1, LANES), lambda bi, hi: (0, 0))
    return pl.pallas_call(
        functools.partial(_gdn_kernel, unroll_a=16, unroll_b=2),
        grid=(b, GDN_HEADS),
        in_specs=[
            seq_spec(col_q), seq_spec(col_k), seq_spec(col_v), seq_spec(col_z),
            pl.BlockSpec((1, t, LANES), lambda bi, hi: (bi, 0, 0)),
            pl.BlockSpec((1, GDN_HEADS, t), lambda bi, hi: (bi, SMALL_GA // GDN_HEADS, 0)),
            cw_spec(0), cw_spec(GDN_HEADS), cw_spec(2 * GDN_HEADS),
            row_spec,
        ],
        out_specs=pl.BlockSpec((1, t, d), lambda bi, hi: (bi, 0, hi)),
        out_shape=jax.ShapeDtypeStruct((b, t, GDN_HEADS * d), BF16),
        scratch_shapes=[
            pltpu.VMEM((t, d), F32),
            pltpu.VMEM((t, d), F32),
            pltpu.VMEM((t, d), F32),
            pltpu.VMEM((t, LANES), F32),
            pltpu.VMEM((t, LANES), F32),
            pltpu.VMEM((n_chunks, d, d), BF16),
            pltpu.VMEM((n_chunks, d, d), F32),
            pltpu.VMEM((t, d), BF16),
            pltpu.VMEM((t, d), F32),
        ],
        compiler_params=pltpu.CompilerParams(
            dimension_semantics=("arbitrary", "arbitrary"),
            vmem_limit_bytes=VMEM_LIMIT_BYTES),
        name="gdn_branch",
    )(proj3, proj3, proj3, proj3, small3, small_t, conv_w, conv_w, conv_w, norm_g)


def _fox_kernel(q_ref, k_ref, v_ref, ct_ref, gq_ref, gk_ref, o_ref, qn_s, kn_s, vb_s):
    h = pl.program_id(1)
    t = q_ref.shape[1]
    tq = FOX_TQ
    nq = t // tq

    qn_s[...] = (_rmsnorm(q_ref[0], gq_ref[...]) * (FOX_D ** -0.5)).astype(BF16)
    kn_s[...] = _rmsnorm(k_ref[0], gk_ref[...]).astype(BF16)
    vb_s[...] = v_ref[0].astype(BF16)
    c_row = ct_ref[0, pl.ds(h, 1), :]

    ri = lax.broadcasted_iota(jnp.int32, (tq, tq), 0)
    ci = lax.broadcasted_iota(jnp.int32, (tq, tq), 1)
    causal = ri >= ci

    def scores(qi):
        return _dot_nt(qn_s[qi * tq:(qi + 1) * tq, :], kn_s[:(qi + 1) * tq, :])

    def finish(qi, s):
        n_keys = (qi + 1) * tq
        s = s - c_row[:, :n_keys]
        diag = jnp.where(causal, s[:, n_keys - tq:], NEG_BIG)
        s = diag if qi == 0 else jnp.concatenate([s[:, :n_keys - tq], diag], axis=1)
        p = jnp.exp(s - jnp.max(s, axis=-1, keepdims=True))
        denom = jnp.sum(p, axis=-1, keepdims=True)
        o = _dot(p.astype(BF16), vb_s[:n_keys, :]) / denom
        o_ref[0, qi * tq:(qi + 1) * tq, :] = o.astype(o_ref.dtype)

    s_next = scores(0)
    for qi in range(nq):
        s_cur = s_next
        if qi + 1 < nq:
            s_next = scores(qi + 1)
        finish(qi, s_cur)


def _fox_branch(proj3, small_t, gq, gk, *, col_q, col_k, col_v):
    b, t, _ = proj3.shape
    d = FOX_D
    seq_spec = lambda col: pl.BlockSpec((1, t, d), lambda bi, hi: (bi, 0, col + hi))
    row_spec = pl.BlockSpec((1, LANES), lambda bi, hi: (0, 0))
    return pl.pallas_call(
        _fox_kernel,
        grid=(b, FOX_HEADS),
        in_specs=[
            seq_spec(col_q), seq_spec(col_k), seq_spec(col_v),
            pl.BlockSpec((1, FOX_HEADS, t), lambda bi, hi: (bi, SMALL_FF // FOX_HEADS, 0)),
            row_spec, row_spec,
        ],
        out_specs=pl.BlockSpec((1, t, d), lambda bi, hi: (bi, 0, hi)),
        out_shape=jax.ShapeDtypeStruct((b, t, FOX_HEADS * d), BF16),
        scratch_shapes=[
            pltpu.VMEM((t, d), BF16),
            pltpu.VMEM((t, d), BF16),
            pltpu.VMEM((t, d), BF16),
        ],
        compiler_params=pltpu.CompilerParams(
            dimension_semantics=("arbitrary", "arbitrary"),
            vmem_limit_bytes=VMEM_LIMIT_BYTES),
        name="fox_branch",
    )(proj3, proj3, proj3, small_t, gq, gk)


def _merge_mlp_kernel(oa_ref, ob_ref, ga_ref, gb_ref, x_ref, pa_ref, pb_ref, wo_ref, ng_ref,
                      wu_ref, wd_ref, out_ref, *, ff_chunk):
    ya = _dot(oa_ref[...], pa_ref[...])
    yb = _dot(ob_ref[...], pb_ref[...])
    merged = _sigmoid(ga_ref[...]) * ya + _sigmoid(gb_ref[...]) * yb
    hid = x_ref[...] + _dot(merged.astype(BF16), wo_ref[...])
    hn = _rmsnorm(hid, ng_ref[...]).astype(BF16)
    acc = hid
    d_ff = wu_ref.shape[1]
    for c0 in range(0, d_ff, ff_chunk):
        up = _dot(hn, wu_ref[:, c0:c0 + ff_chunk])
        act = jnp.square(jnp.maximum(up, 0.0)).astype(BF16)
        acc = acc + _dot(act, wd_ref[c0:c0 + ff_chunk, :])
    out_ref[...] = acc


def _merge_mlp(oa, ob, proj, x2d, pa, pb, wo, ng, wu, wd, *, tm, col_gate_a, col_gate_b):
    m, d = x2d.shape
    d_ff = wu.shape[1]
    tile = lambda col: pl.BlockSpec((tm, d), lambda i: (i, col))
    const = lambda shape: pl.BlockSpec(shape, lambda i: (0, 0), pipeline_mode=pl.Buffered(1))
    return pl.pallas_call(
        functools.partial(_merge_mlp_kernel, ff_chunk=1024),
        grid=(m // tm,),
        in_specs=[
            tile(0), tile(0), tile(col_gate_a), tile(col_gate_b), tile(0),
            const((d, d)), const((d, d)), const((d, d)), const((1, d)),
            const((d, d_ff)), const((d_ff, d)),
        ],
        out_specs=tile(0),
        out_shape=jax.ShapeDtypeStruct((m, d), F32),
        compiler_params=pltpu.CompilerParams(
            dimension_semantics=("arbitrary",),
            vmem_limit_bytes=VMEM_LIMIT_BYTES),
        name="merge_mlp",
    )(oa, ob, proj, proj, x2d, pa, pb, wo, ng, wu, wd)


def _lane_row(values, offset):
    row = jnp.zeros((1, LANES), F32)
    return row.at[0, offset:offset + values.shape[0]].set(values.astype(F32))


def kernel(x, norm_mix_g, w_in, gdn_conv_w, gdn_a_log, gdn_dt_bias, gdn_norm_g, fox_q_norm_g,
           fox_k_norm_g, fox_f_bias, w_proj_gdn, w_proj_fox, w_out, norm_mlp_g, w_up, w_down):
    b, t, d = x.shape
    depth = w_in.shape[0]
    qk_w = GDN_HEADS * GDN_D
    fox_w = FOX_HEADS * FOX_D
    o_ga = 4 * qk_w
    o_gb = o_ga + GDN_HEADS
    o_fq = o_gb + GDN_HEADS
    o_ff = o_fq + 3 * fox_w
    o_gate = o_ff + FOX_HEADS
    blocks = lambda width: width // LANES
    cb_fox = blocks(4 * qk_w)
    cb_gate = cb_fox + blocks(3 * fox_w)

    for l in range(depth):
        w = w_in[l]
        w_main = jnp.concatenate([w[:, :o_ga], w[:, o_fq:o_ff], w[:, o_gate:]], axis=1).astype(BF16)
        w_small = jnp.concatenate(
            [w[:, o_ga:o_fq], w[:, o_ff:o_gate],
             jnp.zeros((d, LANES - 2 * GDN_HEADS - FOX_HEADS), F32)], axis=1)
        ws_hi = w_small.astype(BF16)
        ws_lo = (w_small - ws_hi.astype(F32)).astype(BF16)

        x2d = x.reshape(b * t, d)
        gain = norm_mix_g[l][None, :]
        proj = _in_projection(x2d, gain, w_main, tm=512, tn=3072)
        bias_row = _lane_row(gdn_dt_bias[l], SMALL_GA) + _lane_row(fox_f_bias[l], SMALL_FF)
        small3, small_t = _gate_projection(x, gain, ws_hi, ws_lo, _lane_row(gdn_a_log[l], SMALL_GA),
                                           bias_row)
        proj3 = proj.reshape(b, t, -1)

        o_a = _gdn_branch(
            proj3, small3, small_t, gdn_conv_w[l], gdn_norm_g[l][None, :],
            col_q=0, col_k=GDN_HEADS, col_v=2 * GDN_HEADS, col_z=3 * GDN_HEADS)
        o_b = _fox_branch(
            proj3, small_t, fox_q_norm_g[l][None, :], fox_k_norm_g[l][None, :],
            col_q=cb_fox, col_k=cb_fox + FOX_HEADS, col_v=cb_fox + 2 * FOX_HEADS)

        out = _merge_mlp(
            o_a.reshape(b * t, qk_w), o_b.reshape(b * t, fox_w), proj, x2d,
            w_proj_gdn[l].astype(BF16), w_proj_fox[l].astype(BF16), w_out[l].astype(BF16),
            norm_mlp_g[l][None, :], w_up[l].astype(BF16), w_down[l].astype(BF16),
            tm=512, col_gate_a=cb_gate // blocks(d), col_gate_b=cb_gate // blocks(d) + 1)
        x = out.reshape(b, t, d)
    return x
```

```python
import functools

import jax
import jax.numpy as jnp
from jax import lax
from jax.experimental import pallas as pl
from jax.experimental.pallas import tpu as pltpu

F32 = jnp.float32
BF16 = jnp.bfloat16

LANES = 128
VMEM_LIMIT_BYTES = 56 * 1024 * 1024

EPS = 1e-6
GDN_HEADS = 8
GDN_D = 128
GDN_CONV = 4
GDN_CHUNK = 64
GDN_INV_BLOCK = 16
FOX_HEADS = 8
FOX_D = 128
FOX_TQ = 256
SMALL_GA, SMALL_GB, SMALL_FF = 0, 8, 16
NEG_BIG = -1e30


def _dot(a, b):
    return jnp.dot(a, b, preferred_element_type=F32)


def _dot_nt(a, b):
    return lax.dot_general(a, b, (((1,), (1,)), ((), ())), preferred_element_type=F32)


def _sigmoid(x):
    return 1.0 / (1.0 + jnp.exp(-x))


def _softplus(x):
    return jnp.maximum(x, 0.0) + jnp.log1p(jnp.exp(-jnp.abs(x)))


def _rmsnorm(x, g):
    return x * lax.rsqrt(jnp.mean(x * x, axis=-1, keepdims=True) + EPS) * g


def _pick_lane(x, lane):
    ids = lax.broadcasted_iota(jnp.int32, x.shape, 1)
    col = jnp.sum(jnp.where(ids == lane, x, 0.0), axis=-1, keepdims=True)
    return jnp.broadcast_to(col, x.shape)


def _segment_cumsum(x, seg):
    pos = lax.broadcasted_iota(jnp.int32, x.shape, 0) % seg
    s = 1
    while s < seg:
        x = x + jnp.where(pos >= s, pltpu.roll(x, s, axis=0), 0.0)
        s *= 2
    return x


def _head_normalise(a, gain, *, mean, scale=1.0):
    heads = []
    for c0 in range(0, a.shape[1], LANES):
        a_h = a[:, c0:c0 + LANES]
        ss = jnp.sum(a_h * a_h, axis=-1, keepdims=True)
        if mean:
            ss = ss * (1.0 / LANES)
        inv = lax.rsqrt(ss + EPS)
        if scale != 1.0:
            inv = inv * scale
        heads.append(a_h * inv if gain is None else a_h * inv * gain[:, c0:c0 + LANES])
    return jnp.concatenate(heads, axis=1)


def _causal_conv_silu(y, tail, w):
    tm, cw = y.shape
    groups = y.reshape(tm // 8, 8, cw)
    row_in_group = lax.broadcasted_iota(jnp.int32, (1, 8, cw), 1)
    acc = None
    for i in range(GDN_CONV):
        s = GDN_CONV - 1 - i
        if s == 0:
            shifted = groups
        else:
            rotated = pltpu.roll(groups, s, axis=1)
            previous = jnp.concatenate([pltpu.roll(tail, s, axis=0)[None], rotated[:-1]], axis=0)
            shifted = jnp.where(row_in_group < s, previous, rotated)
        term = shifted * w[i:i + 1, :][None]
        acc = term if acc is None else acc + term
    acc = acc.reshape(tm, cw)
    return acc * _sigmoid(acc)


_CONV_MODES = ("conv", "conv_l2", "conv_l2_scaled")
_AUX_MODES = _CONV_MODES + ("rms",)


def _proj_kernel(x_ref, g_ref, w_ref, *refs, modes, chunk, tiles_per_seq):
    refs = list(refs)
    aux_ref = refs.pop(0) if any(mode in _AUX_MODES for mode in modes) else None
    outs = refs[:len(modes)]
    tail_s = refs[len(modes)] if len(refs) > len(modes) else None
    tm = x_ref.shape[0]
    width = outs[0].shape[1]
    u = _rmsnorm(x_ref[...], g_ref[...]).astype(BF16)
    sequence_start = (pl.program_id(0) % tiles_per_seq) == 0
    for gi, (mode, out) in enumerate(zip(modes, outs)):
        for c0 in range(0, width, chunk):
            cols = slice(gi * width + c0, gi * width + c0 + chunk)
            y = _dot(u, w_ref[:, cols])
            if mode in _CONV_MODES:
                tail = jnp.where(sequence_start, 0.0, tail_s[:, cols])
                tail_s[:, cols] = y[tm - 8:, :]
                y = _causal_conv_silu(y, tail, aux_ref[0:GDN_CONV, cols])
                if mode != "conv":
                    scale = GDN_D ** -0.5 if mode == "conv_l2_scaled" else 1.0
                    y = _head_normalise(y, None, mean=False, scale=scale)
            elif mode == "rms":
                y = _head_normalise(y, aux_ref[0:1, cols], mean=True)
            elif mode == "silu":
                y = y * _sigmoid(y)
            elif mode == "sigmoid":
                y = _sigmoid(y)
            else:
                assert mode == "copy", mode
            out[:, c0:c0 + chunk] = y.astype(out.dtype)


def _in_projection(x2d, gain, w, aux, *, modes, tm, seq_len, name):
    m, d = x2d.shape
    n = w.shape[1]
    width = n // len(modes)
    assert (aux is not None) == any(mode in _AUX_MODES for mode in modes)
    needs_tail = any(mode in _CONV_MODES for mode in modes)
    const = lambda shape: pl.BlockSpec(shape, lambda i: (0, 0), pipeline_mode=pl.Buffered(1))
    operands = [x2d, gain, w] + ([aux] if aux is not None else [])
    chunk = 512
    return pl.pallas_call(
        functools.partial(_proj_kernel, modes=modes, chunk=chunk, tiles_per_seq=seq_len // tm),
        grid=(m // tm,),
        in_specs=[pl.BlockSpec((tm, d), lambda i: (i, 0))] + [const(a.shape) for a in operands[1:]],
        out_specs=[pl.BlockSpec((tm, width), lambda i: (i, 0)) for _ in modes],
        out_shape=[jax.ShapeDtypeStruct((m, width), BF16) for _ in modes],
        scratch_shapes=[pltpu.VMEM((8, n), F32)] if needs_tail else [],
        compiler_params=pltpu.CompilerParams(
            dimension_semantics=("arbitrary",),
            vmem_limit_bytes=VMEM_LIMIT_BYTES),
        name=name,
    )(*operands)


def _gate_proj_kernel(x_ref, g_ref, ws_hi_ref, ws_lo_ref, alog_ref, bias_ref, out_ref, out_t_ref, loc_s):
    t = x_ref.shape[0]
    n_chunks = t // GDN_CHUNK
    u = _rmsnorm(x_ref[...], g_ref[...])
    u_hi = u.astype(BF16)
    u_lo = (u - u_hi.astype(F32)).astype(BF16)
    ws_hi = ws_hi_ref[...]
    logits = _dot(u_hi, ws_hi) + _dot(u_lo, ws_hi) + _dot(u_hi, ws_lo_ref[...])
    shifted = logits + bias_ref[...]
    log_decay = -jnp.exp(alog_ref[...]) * _softplus(shifted)
    beta = _sigmoid(logits)
    log_forget = -_softplus(-shifted)
    lane = lax.broadcasted_iota(jnp.int32, logits.shape, 1)
    vals = jnp.where(lane < SMALL_GB, log_decay, jnp.where(lane < SMALL_FF, beta, log_forget))
    local = _segment_cumsum(vals, GDN_CHUNK)
    loc_s[...] = local
    totals = loc_s[pl.ds(GDN_CHUNK - 1, n_chunks, stride=GDN_CHUNK), :]
    carried = _segment_cumsum(totals, n_chunks) - totals
    full = local + jnp.broadcast_to(carried[:, None, :], (n_chunks, GDN_CHUNK, LANES)).reshape(t, LANES)
    out = jnp.where(lane < SMALL_GB, local, jnp.where(lane < SMALL_FF, vals, full))
    out_ref[0] = out
    out_t_ref[0] = jnp.transpose(out)


def _gate_projection(x3, gain, ws_hi, ws_lo, alog_row, bias_row):
    b, t, d = x3.shape
    const = lambda shape: pl.BlockSpec(shape, lambda i: (0, 0))
    return pl.pallas_call(
        _gate_proj_kernel,
        grid=(b,),
        in_specs=[
            pl.BlockSpec((None, t, d), lambda i: (i, 0, 0)),
            const((1, d)), const((d, LANES)), const((d, LANES)), const((1, LANES)), const((1, LANES)),
        ],
        out_specs=[
            pl.BlockSpec((1, t, LANES), lambda i: (i, 0, 0)),
            pl.BlockSpec((1, LANES, t), lambda i: (i, 0, 0)),
        ],
        out_shape=[
            jax.ShapeDtypeStruct((b, t, LANES), F32),
            jax.ShapeDtypeStruct((b, LANES, t), F32),
        ],
        scratch_shapes=[pltpu.VMEM((t, LANES), F32)],
        compiler_params=pltpu.CompilerParams(
            dimension_semantics=("arbitrary",),
            vmem_limit_bytes=VMEM_LIMIT_BYTES),
        name="gate_projection",
    )(x3, gain, ws_hi, ws_lo, alog_row, bias_row)


def _unit_lower_inverse(a, eye, same_block):
    d = jnp.where(same_block, a, 0.0)
    off = a - d
    x = eye - d
    p = _dot(d.astype(BF16), d.astype(BF16))
    yield
    for step in range(3):
        p_bf = p.astype(BF16)
        x = x + _dot(x.astype(BF16), p_bf)
        if step < 2:
            p = _dot(p_bf, p_bf)
        yield
    x_bf = x.astype(BF16)
    m = _dot(x_bf, off.astype(BF16))
    yield
    m_bf = m.astype(BF16)
    m2 = _dot(m_bf, m_bf)
    yield
    im = eye - m
    y = im + _dot(im.astype(BF16), m2.astype(BF16))
    yield
    return _dot(y.astype(BF16), x_bf)


def _run_in_lockstep(generators):
    results = [None] * len(generators)
    live = list(enumerate(generators))
    while live:
        still_live = []
        for idx, gen in live:
            try:
                next(gen)
                still_live.append((idx, gen))
            except StopIteration as stop:
                results[idx] = stop.value
        live = still_live
    return results


def _gdn_kernel(q_ref, k_ref, v_ref, z_ref, sm_ref, gt_ref, ng_ref, o_ref,
                gcum_s, beta_s, pm_s, qm_s, r_s, u_s,
                *, heads, unroll_b):
    h0 = pl.program_id(1) * heads
    t = q_ref.shape[1]
    c = GDN_CHUNK
    d = GDN_D
    n_chunks = t // c

    ri = lax.broadcasted_iota(jnp.int32, (c, c), 0)
    ci = lax.broadcasted_iota(jnp.int32, (c, c), 1)
    tri_incl = ri >= ci
    tri_strict = ri > ci
    same_block = (ri // GDN_INV_BLOCK) == (ci // GDN_INV_BLOCK)
    eye = jnp.where(ri == ci, 1.0, 0.0).astype(F32)

    def chunk_prepare(q_bf, k_bf, v_bf, gb, bb, g_row):
        diff = gb[:, :c] - g_row
        decay = jnp.where(tri_incl, jnp.exp(jnp.where(tri_incl, diff, 0.0)), 0.0)
        q = q_bf.astype(F32)
        k = k_bf.astype(F32)
        v = v_bf.astype(F32)
        qk_kk = _dot_nt(jnp.concatenate([q_bf, k_bf], axis=0), k_bf)
        yield
        qk = qk_kk[:c, :]
        kk = qk_kk[c:, :]
        a_mat = jnp.where(tri_strict, bb[:, :c] * kk * decay, 0.0)
        t_inv = yield from _unit_lower_inverse(a_mat, eye, same_block)
        e_g = jnp.exp(gb)
        rhs = jnp.concatenate([v * bb, k * (bb * e_g)], axis=1).astype(BF16)
        sol = _dot(t_inv.astype(BF16), rhs).astype(BF16)
        yield
        g_last = gb[c - 1:c, :]
        k_dec = k * jnp.exp(g_last - gb)
        attn = qk * decay
        lhs = jnp.concatenate([jnp.transpose(k_dec), attn], axis=0).astype(BF16)
        fused = _dot(lhs, sol)
        q_mat = fused[:d, :d]
        p_mat = fused[:d, d:]
        u_mat = fused[d:, :d]
        r_mat = q * e_g - fused[d:, d:]
        return p_mat.astype(BF16), q_mat, r_mat.astype(BF16), u_mat

    sm = sm_ref[0]
    for hh in range(heads):
        lanes = slice(hh * d, (hh + 1) * d)
        gcum_s[hh] = _pick_lane(sm, SMALL_GA + h0 + hh)
        beta_s[hh] = _pick_lane(sm, SMALL_GB + h0 + hh)
        g_rows = gt_ref[0, pl.ds(h0 + hh, 1), :]
        spans = [slice(ch * c, (ch + 1) * c) for ch in range(n_chunks)]
        loaded = [(q_ref[0, r, lanes], k_ref[0, r, lanes], v_ref[0, r, lanes],
                   gcum_s[hh, r, :], beta_s[hh, r, :], g_rows[:, r]) for r in spans]
        results = _run_in_lockstep([chunk_prepare(*args) for args in loaded])
        for ch, (r, (p_mat, q_mat, r_mat, u_mat)) in enumerate(zip(spans, results)):
            pm_s[hh, ch] = p_mat
            qm_s[hh, ch] = q_mat
            r_s[hh, r, :] = r_mat
            u_s[hh, r, :] = u_mat

    ng = ng_ref[...]

    def scan_body(i, states):
        states = list(states)
        for u in range(unroll_b):
            ch = i * unroll_b + u
            r0 = pl.multiple_of(ch * c, c)
            rows = pl.ds(r0, c)
            products = []
            for hh in range(heads):
                s_bf = states[hh].astype(BF16)
                o = _dot(r_s[hh, rows, :], s_bf) + u_s[hh, rows, :]
                products.append((o, _dot(pm_s[hh, ch], s_bf)))
            for hh in range(heads):
                lanes = slice(hh * d, (hh + 1) * d)
                o, ps = products[hh]
                dec = jnp.exp(gcum_s[hh, pl.ds(r0 + c - 1, 1), :])
                states[hh] = states[hh] * dec - ps + qm_s[hh, ch]
                z_act = z_ref[0, rows, lanes].astype(F32)
                o_ref[0, rows, lanes] = (_rmsnorm(o, ng) * z_act).astype(o_ref.dtype)
        return tuple(states)

    lax.fori_loop(0, n_chunks // unroll_b, scan_body,
                  tuple(jnp.zeros((d, d), F32) for _ in range(heads)))


def _gdn_branch(q3, k3, v3, z3, small3, small_t, norm_g, *, heads):
    b, t, _ = q3.shape
    d = GDN_D
    w = heads * d
    n_chunks = t // GDN_CHUNK
    seq_spec = pl.BlockSpec((1, t, w), lambda bi, hi: (bi, 0, hi))
    row_spec = pl.BlockSpec((1, LANES), lambda bi, hi: (0, 0))
    return pl.pallas_call(
        functools.partial(_gdn_kernel, heads=heads, unroll_b=2),
        grid=(b, GDN_HEADS // heads),
        in_specs=[
            seq_spec, seq_spec, seq_spec, seq_spec,
            pl.BlockSpec((1, t, LANES), lambda bi, hi: (bi, 0, 0)),
            pl.BlockSpec((1, GDN_HEADS, t), lambda bi, hi: (bi, SMALL_GA // GDN_HEADS, 0)),
            row_spec,
        ],
        out_specs=seq_spec,
        out_shape=jax.ShapeDtypeStruct((b, t, GDN_HEADS * d), BF16),
        scratch_shapes=[
            pltpu.VMEM((heads, t, LANES), F32),
            pltpu.VMEM((heads, t, LANES), F32),
            pltpu.VMEM((heads, n_chunks, d, d), BF16),
            pltpu.VMEM((heads, n_chunks, d, d), F32),
            pltpu.VMEM((heads, t, d), BF16),
            pltpu.VMEM((heads, t, d), F32),
        ],
        compiler_params=pltpu.CompilerParams(
            dimension_semantics=("arbitrary", "arbitrary"),
            vmem_limit_bytes=VMEM_LIMIT_BYTES),
        name="gdn_branch",
    )(q3, k3, v3, z3, small3, small_t, norm_g)


def _fox_kernel(q_ref, k_ref, v_ref, ct_ref, o_ref):
    h = pl.program_id(1)
    t = q_ref.shape[1]
    tq = FOX_TQ
    nq = t // tq
    c_row = ct_ref[0, pl.ds(h, 1), :]

    ri = lax.broadcasted_iota(jnp.int32, (tq, tq), 0)
    ci = lax.broadcasted_iota(jnp.int32, (tq, tq), 1)
    causal = ri >= ci

    def scores(qi):
        return _dot_nt(q_ref[0, qi * tq:(qi + 1) * tq, :], k_ref[0, :(qi + 1) * tq, :])

    def finish(qi, s):
        n_keys = (qi + 1) * tq
        s = s - c_row[:, :n_keys]
        diag = jnp.where(causal, s[:, n_keys - tq:], NEG_BIG)
        s = diag if qi == 0 else jnp.concatenate([s[:, :n_keys - tq], diag], axis=1)
        p = jnp.exp(s - jnp.max(s, axis=-1, keepdims=True))
        denom = jnp.sum(p, axis=-1, keepdims=True)
        o = _dot(p.astype(BF16), v_ref[0, :n_keys, :]) / denom
        o_ref[0, qi * tq:(qi + 1) * tq, :] = o.astype(o_ref.dtype)

    s_next = scores(0)
    for qi in range(nq):
        s_cur = s_next
        if qi + 1 < nq:
            s_next = scores(qi + 1)
        finish(qi, s_cur)


def _fox_branch(q3, k3, v3, small_t):
    b, t, _ = q3.shape
    d = FOX_D
    seq_spec = pl.BlockSpec((1, t, d), lambda bi, hi: (bi, 0, hi))
    return pl.pallas_call(
        _fox_kernel,
        grid=(b, FOX_HEADS),
        in_specs=[
            seq_spec, seq_spec, seq_spec,
            pl.BlockSpec((1, FOX_HEADS, t), lambda bi, hi: (bi, SMALL_FF // FOX_HEADS, 0)),
        ],
        out_specs=seq_spec,
        out_shape=jax.ShapeDtypeStruct((b, t, FOX_HEADS * d), BF16),
        compiler_params=pltpu.CompilerParams(
            dimension_semantics=("arbitrary", "arbitrary"),
            vmem_limit_bytes=VMEM_LIMIT_BYTES),
        name="fox_branch",
    )(q3, k3, v3, small_t)


def _merge_mlp_kernel(oa_ref, ob_ref, ga_ref, gb_ref, x_ref, pa_ref, pb_ref, wo_ref, ng_ref,
                      wu_ref, wd_ref, out_ref, *, ff_chunk):
    ya = _dot(oa_ref[...], pa_ref[...])
    yb = _dot(ob_ref[...], pb_ref[...])
    merged = ga_ref[...].astype(F32) * ya + gb_ref[...].astype(F32) * yb
    hid = x_ref[...] + _dot(merged.astype(BF16), wo_ref[...])
    hn = _rmsnorm(hid, ng_ref[...]).astype(BF16)
    acc = hid
    d_ff = wu_ref.shape[1]
    for c0 in range(0, d_ff, ff_chunk):
        up = _dot(hn, wu_ref[:, c0:c0 + ff_chunk])
        act = jnp.square(jnp.maximum(up, 0.0)).astype(BF16)
        acc = acc + _dot(act, wd_ref[c0:c0 + ff_chunk, :])
    out_ref[...] = acc


def _merge_mlp(oa, ob, gate_a, gate_b, x2d, pa, pb, wo, ng, wu, wd, *, tm):
    m, d = x2d.shape
    d_ff = wu.shape[1]
    tile = pl.BlockSpec((tm, d), lambda i: (i, 0))
    const = lambda shape: pl.BlockSpec(shape, lambda i: (0, 0), pipeline_mode=pl.Buffered(1))
    return pl.pallas_call(
        functools.partial(_merge_mlp_kernel, ff_chunk=1024),
        grid=(m // tm,),
        in_specs=[
            tile, tile, tile, tile, tile,
            const((d, d)), const((d, d)), const((d, d)), const((1, d)),
            const((d, d_ff)), const((d_ff, d)),
        ],
        out_specs=tile,
        out_shape=jax.ShapeDtypeStruct((m, d), F32),
        compiler_params=pltpu.CompilerParams(
            dimension_semantics=("arbitrary",),
            vmem_limit_bytes=VMEM_LIMIT_BYTES),
        name="merge_mlp",
    )(oa, ob, gate_a, gate_b, x2d, pa, pb, wo, ng, wu, wd)


def _lane_row(values, offset):
    row = jnp.zeros((1, LANES), F32)
    return row.at[0, offset:offset + values.shape[0]].set(values.astype(F32))


def kernel(x, norm_mix_g, w_in, gdn_conv_w, gdn_a_log, gdn_dt_bias, gdn_norm_g, fox_q_norm_g,
           fox_k_norm_g, fox_f_bias, w_proj_gdn, w_proj_fox, w_out, norm_mlp_g, w_up, w_down):
    b, t, d = x.shape
    depth = w_in.shape[0]
    qk_w = GDN_HEADS * GDN_D
    fox_w = FOX_HEADS * FOX_D
    o_gz = 3 * qk_w
    o_ga = 4 * qk_w
    o_gb = o_ga + GDN_HEADS
    o_fq = o_gb + GDN_HEADS
    o_fv = o_fq + 2 * fox_w
    o_ff = o_fq + 3 * fox_w
    o_gate = o_ff + FOX_HEADS
    tm = 512

    for l in range(depth):
        w = w_in[l]
        w_small = jnp.concatenate(
            [w[:, o_ga:o_fq], w[:, o_ff:o_gate],
             jnp.zeros((d, LANES - 2 * GDN_HEADS - FOX_HEADS), F32)], axis=1)
        ws_hi = w_small.astype(BF16)
        ws_lo = (w_small - ws_hi.astype(F32)).astype(BF16)

        x2d = x.reshape(b * t, d)
        gain = norm_mix_g[l][None, :]
        project = functools.partial(_in_projection, x2d, gain, tm=tm, seq_len=t)
        gq, gk, gv = project(
            w[:, :o_gz].astype(BF16), gdn_conv_w[l],
            modes=("conv_l2_scaled", "conv_l2", "conv"), name="proj_gdn_qkv")
        fox_gains = jnp.concatenate(
            [jnp.zeros((qk_w,), F32), jnp.tile(fox_q_norm_g[l] * (FOX_D ** -0.5), FOX_HEADS),
             jnp.tile(fox_k_norm_g[l], FOX_HEADS)])[None, :]
        gz, fq, fk = project(
            jnp.concatenate([w[:, o_gz:o_ga], w[:, o_fq:o_fv]], axis=1).astype(BF16), fox_gains,
            modes=("silu", "rms", "rms"), name="proj_z_fox_qk")
        fv, gate_a, gate_b = project(
            jnp.concatenate([w[:, o_fv:o_ff], w[:, o_gate:]], axis=1).astype(BF16), None,
            modes=("copy", "sigmoid", "sigmoid"), name="proj_fox_v_gates")

        bias_row = _lane_row(gdn_dt_bias[l], SMALL_GA) + _lane_row(fox_f_bias[l], SMALL_FF)
        small3, small_t = _gate_projection(x, gain, ws_hi, ws_lo, _lane_row(gdn_a_log[l], SMALL_GA),
                                           bias_row)
        seq = lambda a: a.reshape(b, t, -1)
        o_a = _gdn_branch(seq(gq), seq(gk), seq(gv), seq(gz), small3, small_t, gdn_norm_g[l][None, :],
                          heads=2)
        o_b = _fox_branch(seq(fq), seq(fk), seq(fv), small_t)

        out = _merge_mlp(
            o_a.reshape(b * t, qk_w), o_b.reshape(b * t, fox_w), gate_a, gate_b, x2d,
            w_proj_gdn[l].astype(BF16), w_proj_fox[l].astype(BF16), w_out[l].astype(BF16),
            norm_mlp_g[l][None, :], w_up[l].astype(BF16), w_down[l].astype(BF16), tm=tm)
        x = out.reshape(b, t, d)
    return x
```

```python
import functools

import jax
import jax.numpy as jnp
from jax import lax
from jax.experimental import pallas as pl
from jax.experimental.pallas import tpu as pltpu

F32 = jnp.float32
BF16 = jnp.bfloat16

LANES = 128
VMEM_LIMIT_BYTES = 56 * 1024 * 1024

EPS = 1e-6
GDN_HEADS = 8
GDN_D = 128
GDN_CONV = 4
GDN_CHUNK = 64
GDN_INV_BLOCK = 16
FOX_HEADS = 8
FOX_D = 128
FOX_TQ = 256
SMALL_GA, SMALL_GB, SMALL_FF = 0, 8, 16
NEG_BIG = -1e30


def _dot(a, b):
    return jnp.dot(a, b, preferred_element_type=F32)


def _dot_nt(a, b):
    return lax.dot_general(a, b, (((1,), (1,)), ((), ())), preferred_element_type=F32)


def _sigmoid(x):
    return 1.0 / (1.0 + jnp.exp(-x))


def _softplus(x):
    return jnp.maximum(x, 0.0) + jnp.log1p(jnp.exp(-jnp.abs(x)))


def _rmsnorm(x, g):
    return x * lax.rsqrt(jnp.mean(x * x, axis=-1, keepdims=True) + EPS) * g


def _pick_lane(x, lane):
    ids = lax.broadcasted_iota(jnp.int32, x.shape, 1)
    col = jnp.sum(jnp.where(ids == lane, x, 0.0), axis=-1, keepdims=True)
    return jnp.broadcast_to(col, x.shape)


def _segment_cumsum(x, seg):
    pos = lax.broadcasted_iota(jnp.int32, x.shape, 0) % seg
    s = 1
    while s < seg:
        x = x + jnp.where(pos >= s, pltpu.roll(x, s, axis=0), 0.0)
        s *= 2
    return x


def _head_normalise(a, gain, *, mean, scale=1.0):
    heads = []
    for c0 in range(0, a.shape[1], LANES):
        a_h = a[:, c0:c0 + LANES]
        ss = jnp.sum(a_h * a_h, axis=-1, keepdims=True)
        if mean:
            ss = ss * (1.0 / LANES)
        inv = lax.rsqrt(ss + EPS)
        if scale != 1.0:
            inv = inv * scale
        heads.append(a_h * inv if gain is None else a_h * inv * gain[:, c0:c0 + LANES])
    return jnp.concatenate(heads, axis=1)


def _causal_conv_silu(y, tail, w):
    tm, cw = y.shape
    groups = y.reshape(tm // 8, 8, cw)
    row_in_group = lax.broadcasted_iota(jnp.int32, (1, 8, cw), 1)
    acc = None
    for i in range(GDN_CONV):
        s = GDN_CONV - 1 - i
        if s == 0:
            shifted = groups
        else:
            rotated = pltpu.roll(groups, s, axis=1)
            previous = jnp.concatenate([pltpu.roll(tail, s, axis=0)[None], rotated[:-1]], axis=0)
            shifted = jnp.where(row_in_group < s, previous, rotated)
        term = shifted * w[i:i + 1, :][None]
        acc = term if acc is None else acc + term
    acc = acc.reshape(tm, cw)
    return acc * _sigmoid(acc)


_CONV_MODES = ("conv", "conv_l2", "conv_l2_scaled")
_AUX_MODES = _CONV_MODES + ("rms",)


def _proj_kernel(x_ref, g_ref, w_ref, *refs, modes, head_major, width, chunk, tiles_per_seq):
    refs = list(refs)
    aux_ref = refs.pop(0) if any(mode in _AUX_MODES for mode in modes) else None
    outs = refs[:len(modes)]
    tail_s = refs[len(modes)] if len(refs) > len(modes) else None
    tm = x_ref.shape[0]
    u = _rmsnorm(x_ref[...], g_ref[...]).astype(BF16)
    sequence_start = (pl.program_id(0) % tiles_per_seq) == 0
    for gi, (mode, out, by_head) in enumerate(zip(modes, outs, head_major)):
        for c0 in range(0, width, chunk):
            cols = slice(gi * width + c0, gi * width + c0 + chunk)
            y = _dot(u, w_ref[:, cols])
            if mode in _CONV_MODES:
                tail = jnp.where(sequence_start, 0.0, tail_s[:, cols])
                tail_s[:, cols] = y[tm - 8:, :]
                y = _causal_conv_silu(y, tail, aux_ref[0:GDN_CONV, cols])
                if mode != "conv":
                    scale = GDN_D ** -0.5 if mode == "conv_l2_scaled" else 1.0
                    y = _head_normalise(y, None, mean=False, scale=scale)
            elif mode == "rms":
                y = _head_normalise(y, aux_ref[0:1, cols], mean=True)
            elif mode == "silu":
                y = y * _sigmoid(y)
            elif mode == "sigmoid":
                y = _sigmoid(y)
            else:
                assert mode == "copy", mode
            y = y.astype(out.dtype)
            if by_head:
                for j in range(chunk // LANES):
                    out[0, c0 // LANES + j] = y[:, j * LANES:(j + 1) * LANES]
            else:
                out[:, c0:c0 + chunk] = y


def _in_projection(x2d, gain, w, aux, *, modes, head_major, tm, seq_len, name):
    m, d = x2d.shape
    n = w.shape[1]
    width = n // len(modes)
    heads = width // LANES
    tiles_per_seq = seq_len // tm
    assert (aux is not None) == any(mode in _AUX_MODES for mode in modes)
    needs_tail = any(mode in _CONV_MODES for mode in modes)
    const = lambda shape: pl.BlockSpec(shape, lambda i: (0, 0), pipeline_mode=pl.Buffered(1))
    operands = [x2d, gain, w] + ([aux] if aux is not None else [])
    flat_spec = pl.BlockSpec((tm, width), lambda i: (i, 0))
    flat_shape = jax.ShapeDtypeStruct((m, width), BF16)
    head_spec = pl.BlockSpec((1, heads, tm, LANES), lambda i: (i // tiles_per_seq, 0, i % tiles_per_seq, 0))
    head_shape = jax.ShapeDtypeStruct((m // seq_len, heads, seq_len, LANES), BF16)
    return pl.pallas_call(
        functools.partial(_proj_kernel, modes=modes, head_major=head_major, width=width, chunk=512,
                          tiles_per_seq=tiles_per_seq),
        grid=(m // tm,),
        in_specs=[pl.BlockSpec((tm, d), lambda i: (i, 0))] + [const(a.shape) for a in operands[1:]],
        out_specs=[head_spec if by_head else flat_spec for by_head in head_major],
        out_shape=[head_shape if by_head else flat_shape for by_head in head_major],
        scratch_shapes=[pltpu.VMEM((8, n), F32)] if needs_tail else [],
        compiler_params=pltpu.CompilerParams(
            dimension_semantics=("arbitrary",),
            vmem_limit_bytes=VMEM_LIMIT_BYTES),
        name=name,
    )(*operands)


def _gate_proj_kernel(x_ref, g_ref, ws_hi_ref, ws_lo_ref, alog_ref, bias_ref, out_ref, out_t_ref, loc_s):
    t = x_ref.shape[0]
    n_chunks = t // GDN_CHUNK
    u = _rmsnorm(x_ref[...], g_ref[...])
    u_hi = u.astype(BF16)
    u_lo = (u - u_hi.astype(F32)).astype(BF16)
    ws_hi = ws_hi_ref[...]
    logits = _dot(u_hi, ws_hi) + _dot(u_lo, ws_hi) + _dot(u_hi, ws_lo_ref[...])
    shifted = logits + bias_ref[...]
    log_decay = -jnp.exp(alog_ref[...]) * _softplus(shifted)
    beta = _sigmoid(logits)
    log_forget = -_softplus(-shifted)
    lane = lax.broadcasted_iota(jnp.int32, logits.shape, 1)
    vals = jnp.where(lane < SMALL_GB, log_decay, jnp.where(lane < SMALL_FF, beta, log_forget))
    local = _segment_cumsum(vals, GDN_CHUNK)
    loc_s[...] = local
    totals = loc_s[pl.ds(GDN_CHUNK - 1, n_chunks, stride=GDN_CHUNK), :]
    carried = _segment_cumsum(totals, n_chunks) - totals
    full = local + jnp.broadcast_to(carried[:, None, :], (n_chunks, GDN_CHUNK, LANES)).reshape(t, LANES)
    out = jnp.where(lane < SMALL_GB, local, jnp.where(lane < SMALL_FF, vals, full))
    out_ref[0] = out
    out_t_ref[0] = jnp.transpose(out)


def _gate_projection(x3, gain, ws_hi, ws_lo, alog_row, bias_row):
    b, t, d = x3.shape
    const = lambda shape: pl.BlockSpec(shape, lambda i: (0, 0))
    return pl.pallas_call(
        _gate_proj_kernel,
        grid=(b,),
        in_specs=[
            pl.BlockSpec((None, t, d), lambda i: (i, 0, 0)),
            const((1, d)), const((d, LANES)), const((d, LANES)), const((1, LANES)), const((1, LANES)),
        ],
        out_specs=[
            pl.BlockSpec((1, t, LANES), lambda i: (i, 0, 0)),
            pl.BlockSpec((1, LANES, t), lambda i: (i, 0, 0)),
        ],
        out_shape=[
            jax.ShapeDtypeStruct((b, t, LANES), F32),
            jax.ShapeDtypeStruct((b, LANES, t), F32),
        ],
        scratch_shapes=[pltpu.VMEM((t, LANES), F32)],
        compiler_params=pltpu.CompilerParams(
            dimension_semantics=("arbitrary",),
            vmem_limit_bytes=VMEM_LIMIT_BYTES),
        name="gate_projection",
    )(x3, gain, ws_hi, ws_lo, alog_row, bias_row)


def _unit_lower_inverse(a, eye, same_block):
    d = jnp.where(same_block, a, 0.0)
    off = a - d
    x = eye - d
    p = _dot(d.astype(BF16), d.astype(BF16))
    yield
    for step in range(3):
        p_bf = p.astype(BF16)
        x = x + _dot(x.astype(BF16), p_bf)
        if step < 2:
            p = _dot(p_bf, p_bf)
        yield
    x_bf = x.astype(BF16)
    m = _dot(x_bf, off.astype(BF16))
    yield
    m_bf = m.astype(BF16)
    m2 = _dot(m_bf, m_bf)
    yield
    im = eye - m
    y = im + _dot(im.astype(BF16), m2.astype(BF16))
    yield
    return _dot(y.astype(BF16), x_bf)


def _run_in_lockstep(generators):
    results = [None] * len(generators)
    live = list(enumerate(generators))
    while live:
        still_live = []
        for idx, gen in live:
            try:
                next(gen)
                still_live.append((idx, gen))
            except StopIteration as stop:
                results[idx] = stop.value
        live = still_live
    return results


def _gdn_kernel(q_ref, k_ref, v_ref, z_ref, sm_ref, gt_ref, ng_ref, o_ref,
                gcum_s, beta_s, pm_s, qm_s, r_s, u_s,
                *, heads, unroll_b):
    h0 = pl.program_id(1) * heads
    t = q_ref.shape[2]
    c = GDN_CHUNK
    d = GDN_D
    n_chunks = t // c

    ri = lax.broadcasted_iota(jnp.int32, (c, c), 0)
    ci = lax.broadcasted_iota(jnp.int32, (c, c), 1)
    tri_incl = ri >= ci
    tri_strict = ri > ci
    same_block = (ri // GDN_INV_BLOCK) == (ci // GDN_INV_BLOCK)
    eye = jnp.where(ri == ci, 1.0, 0.0).astype(F32)

    def chunk_prepare(q_bf, k_bf, v_bf, gb, bb, g_row):
        diff = gb[:, :c] - g_row
        decay = jnp.where(tri_incl, jnp.exp(jnp.where(tri_incl, diff, 0.0)), 0.0)
        q = q_bf.astype(F32)
        k = k_bf.astype(F32)
        v = v_bf.astype(F32)
        qk_kk = _dot_nt(jnp.concatenate([q_bf, k_bf], axis=0), k_bf)
        yield
        qk = qk_kk[:c, :]
        kk = qk_kk[c:, :]
        a_mat = jnp.where(tri_strict, bb[:, :c] * kk * decay, 0.0)
        t_inv = yield from _unit_lower_inverse(a_mat, eye, same_block)
        e_g = jnp.exp(gb)
        rhs = jnp.concatenate([v * bb, k * (bb * e_g)], axis=1).astype(BF16)
        sol = _dot(t_inv.astype(BF16), rhs).astype(BF16)
        yield
        g_last = gb[c - 1:c, :]
        k_dec = k * jnp.exp(g_last - gb)
        attn = qk * decay
        lhs = jnp.concatenate([jnp.transpose(k_dec), attn], axis=0).astype(BF16)
        fused = _dot(lhs, sol)
        q_mat = fused[:d, :d]
        p_mat = fused[:d, d:]
        u_mat = fused[d:, :d]
        r_mat = q * e_g - fused[d:, d:]
        return p_mat.astype(BF16), q_mat, r_mat.astype(BF16), u_mat

    def prepare_head(hh, carry):
        sm = sm_ref[0]
        gcum_s[hh] = _pick_lane(sm, SMALL_GA + h0 + hh)
        beta_s[hh] = _pick_lane(sm, SMALL_GB + h0 + hh)
        g_rows = gt_ref[0, pl.ds(h0 + hh, 1), :]
        spans = [slice(ch * c, (ch + 1) * c) for ch in range(n_chunks)]
        loaded = [(q_ref[0, hh, r, :], k_ref[0, hh, r, :], v_ref[0, hh, r, :],
                   gcum_s[hh, r, :], beta_s[hh, r, :], g_rows[:, r]) for r in spans]
        results = _run_in_lockstep([chunk_prepare(*args) for args in loaded])
        for ch, (r, (p_mat, q_mat, r_mat, u_mat)) in enumerate(zip(spans, results)):
            pm_s[hh, ch] = p_mat
            qm_s[hh, ch] = q_mat.astype(qm_s.dtype)
            r_s[hh, r, :] = r_mat
            u_s[hh, r, :] = u_mat.astype(u_s.dtype)
        return carry

    lax.fori_loop(0, heads, prepare_head, 0)

    ng = ng_ref[...]

    def scan_body(i, states):
        states = list(states)
        for u in range(unroll_b):
            ch = i * unroll_b + u
            r0 = pl.multiple_of(ch * c, c)
            rows = pl.ds(r0, c)
            products = []
            for hh in range(heads):
                s_bf = states[hh].astype(BF16)
                o = _dot(r_s[hh, rows, :], s_bf) + u_s[hh, rows, :]
                products.append((o, _dot(pm_s[hh, ch], s_bf)))
            for hh in range(heads):
                lanes = slice(hh * d, (hh + 1) * d)
                o, ps = products[hh]
                dec = jnp.exp(gcum_s[hh, pl.ds(r0 + c - 1, 1), :])
                states[hh] = states[hh] * dec - ps + qm_s[hh, ch]
                z_act = z_ref[0, hh, rows, :].astype(F32)
                o_ref[0, rows, lanes] = (_rmsnorm(o, ng) * z_act).astype(o_ref.dtype)
        return tuple(states)

    lax.fori_loop(0, n_chunks // unroll_b, scan_body,
                  tuple(jnp.zeros((d, d), F32) for _ in range(heads)))


def _gdn_branch(q4, k4, v4, z4, small3, small_t, norm_g, *, heads):
    b, _, t, d = q4.shape
    n_chunks = t // GDN_CHUNK
    head_spec = pl.BlockSpec((1, heads, t, d), lambda bi, hi: (bi, hi, 0, 0))
    row_spec = pl.BlockSpec((1, LANES), lambda bi, hi: (0, 0))
    return pl.pallas_call(
        functools.partial(_gdn_kernel, heads=heads, unroll_b=2),
        grid=(b, GDN_HEADS // heads),
        in_specs=[
            head_spec, head_spec, head_spec, head_spec,
            pl.BlockSpec((1, t, LANES), lambda bi, hi: (bi, 0, 0)),
            pl.BlockSpec((1, GDN_HEADS, t), lambda bi, hi: (bi, SMALL_GA // GDN_HEADS, 0)),
            row_spec,
        ],
        out_specs=pl.BlockSpec((1, t, heads * d), lambda bi, hi: (bi, 0, hi)),
        out_shape=jax.ShapeDtypeStruct((b, t, GDN_HEADS * d), BF16),
        scratch_shapes=[
            pltpu.VMEM((heads, t, LANES), F32),
            pltpu.VMEM((heads, t, LANES), F32),
            pltpu.VMEM((heads, n_chunks, d, d), BF16),
            pltpu.VMEM((heads, n_chunks, d, d), BF16),
            pltpu.VMEM((heads, t, d), BF16),
            pltpu.VMEM((heads, t, d), BF16),
        ],
        compiler_params=pltpu.CompilerParams(
            dimension_semantics=("arbitrary", "arbitrary"),
            vmem_limit_bytes=VMEM_LIMIT_BYTES),
        name="gdn_branch",
    )(q4, k4, v4, z4, small3, small_t, norm_g)


def _fox_kernel(q_ref, k_ref, v_ref, ct_ref, o_ref):
    h = pl.program_id(1)
    t = q_ref.shape[1]
    tq = FOX_TQ
    nq = t // tq
    c_row = ct_ref[0, pl.ds(h, 1), :]

    ri = lax.broadcasted_iota(jnp.int32, (tq, tq), 0)
    ci = lax.broadcasted_iota(jnp.int32, (tq, tq), 1)
    causal = ri >= ci

    def scores(qi):
        return _dot_nt(q_ref[0, qi * tq:(qi + 1) * tq, :], k_ref[0, :(qi + 1) * tq, :])

    def finish(qi, s):
        n_keys = (qi + 1) * tq
        s = s - c_row[:, :n_keys]
        diag = jnp.where(causal, s[:, n_keys - tq:], NEG_BIG)
        s = diag if qi == 0 else jnp.concatenate([s[:, :n_keys - tq], diag], axis=1)
        p = jnp.exp(s - jnp.max(s, axis=-1, keepdims=True))
        denom = jnp.sum(p, axis=-1, keepdims=True)
        o = _dot(p.astype(BF16), v_ref[0, :n_keys, :]) / denom
        o_ref[0, qi * tq:(qi + 1) * tq, :] = o.astype(o_ref.dtype)

    s_next = scores(0)
    for qi in range(nq):
        s_cur = s_next
        if qi + 1 < nq:
            s_next = scores(qi + 1)
        finish(qi, s_cur)


def _fox_branch(q3, k3, v3, small_t):
    b, t, _ = q3.shape
    d = FOX_D
    seq_spec = pl.BlockSpec((1, t, d), lambda bi, hi: (bi, 0, hi))
    return pl.pallas_call(
        _fox_kernel,
        grid=(b, FOX_HEADS),
        in_specs=[
            seq_spec, seq_spec, seq_spec,
            pl.BlockSpec((1, FOX_HEADS, t), lambda bi, hi: (bi, SMALL_FF // FOX_HEADS, 0)),
        ],
        out_specs=seq_spec,
        out_shape=jax.ShapeDtypeStruct((b, t, FOX_HEADS * d), BF16),
        compiler_params=pltpu.CompilerParams(
            dimension_semantics=("arbitrary", "arbitrary"),
            vmem_limit_bytes=VMEM_LIMIT_BYTES),
        name="fox_branch",
    )(q3, k3, v3, small_t)


def _merge_mlp_kernel(oa_ref, ob_ref, ga_ref, gb_ref, x_ref, pa_ref, pb_ref, wo_ref, ng_ref,
                      wu_ref, wd_ref, out_ref, *, ff_chunk):
    ya = _dot(oa_ref[...], pa_ref[...])
    yb = _dot(ob_ref[...], pb_ref[...])
    merged = ga_ref[...].astype(F32) * ya + gb_ref[...].astype(F32) * yb
    hid = x_ref[...] + _dot(merged.astype(BF16), wo_ref[...])
    hn = _rmsnorm(hid, ng_ref[...]).astype(BF16)
    acc = hid
    d_ff = wu_ref.shape[1]
    for c0 in range(0, d_ff, ff_chunk):
        up = _dot(hn, wu_ref[:, c0:c0 + ff_chunk])
        act = jnp.square(jnp.maximum(up, 0.0)).astype(BF16)
        acc = acc + _dot(act, wd_ref[c0:c0 + ff_chunk, :])
    out_ref[...] = acc


def _merge_mlp(oa, ob, gate_a, gate_b, x2d, pa, pb, wo, ng, wu, wd, *, tm):
    m, d = x2d.shape
    d_ff = wu.shape[1]
    tile = pl.BlockSpec((tm, d), lambda i: (i, 0))
    const = lambda shape: pl.BlockSpec(shape, lambda i: (0, 0), pipeline_mode=pl.Buffered(1))
    return pl.pallas_call(
        functools.partial(_merge_mlp_kernel, ff_chunk=1024),
        grid=(m // tm,),
        in_specs=[
            tile, tile, tile, tile, tile,
            const((d, d)), const((d, d)), const((d, d)), const((1, d)),
            const((d, d_ff)), const((d_ff, d)),
        ],
        out_specs=tile,
        out_shape=jax.ShapeDtypeStruct((m, d), F32),
        compiler_params=pltpu.CompilerParams(
            dimension_semantics=("arbitrary",),
            vmem_limit_bytes=VMEM_LIMIT_BYTES),
        name="merge_mlp",
    )(oa, ob, gate_a, gate_b, x2d, pa, pb, wo, ng, wu, wd)


def _lane_row(values, offset):
    row = jnp.zeros((1, LANES), F32)
    return row.at[0, offset:offset + values.shape[0]].set(values.astype(F32))


def kernel(x, norm_mix_g, w_in, gdn_conv_w, gdn_a_log, gdn_dt_bias, gdn_norm_g, fox_q_norm_g,
           fox_k_norm_g, fox_f_bias, w_proj_gdn, w_proj_fox, w_out, norm_mlp_g, w_up, w_down):
    b, t, d = x.shape
    depth = w_in.shape[0]
    qk_w = GDN_HEADS * GDN_D
    fox_w = FOX_HEADS * FOX_D
    o_gz = 3 * qk_w
    o_ga = 4 * qk_w
    o_gb = o_ga + GDN_HEADS
    o_fq = o_gb + GDN_HEADS
    o_fv = o_fq + 2 * fox_w
    o_ff = o_fq + 3 * fox_w
    o_gate = o_ff + FOX_HEADS
    tm = 512

    for l in range(depth):
        w = w_in[l]
        w_small = jnp.concatenate(
            [w[:, o_ga:o_fq], w[:, o_ff:o_gate],
             jnp.zeros((d, LANES - 2 * GDN_HEADS - FOX_HEADS), F32)], axis=1)
        ws_hi = w_small.astype(BF16)
        ws_lo = (w_small - ws_hi.astype(F32)).astype(BF16)

        x2d = x.reshape(b * t, d)
        gain = norm_mix_g[l][None, :]
        project = functools.partial(_in_projection, x2d, gain, tm=tm, seq_len=t)
        gq, gk, gv = project(
            w[:, :o_gz].astype(BF16), gdn_conv_w[l],
            modes=("conv_l2_scaled", "conv_l2", "conv"), head_major=(True, True, True),
            name="proj_gdn_qkv")
        fox_gains = jnp.concatenate(
            [jnp.zeros((qk_w,), F32), jnp.tile(fox_q_norm_g[l] * (FOX_D ** -0.5), FOX_HEADS),
             jnp.tile(fox_k_norm_g[l], FOX_HEADS)])[None, :]
        gz, fq, fk = project(
            jnp.concatenate([w[:, o_gz:o_ga], w[:, o_fq:o_fv]], axis=1).astype(BF16), fox_gains,
            modes=("silu", "rms", "rms"), head_major=(True, False, False), name="proj_z_fox_qk")
        fv, gate_a, gate_b = project(
            jnp.concatenate([w[:, o_fv:o_ff], w[:, o_gate:]], axis=1).astype(BF16), None,
            modes=("copy", "sigmoid", "sigmoid"), head_major=(False, False, False),
            name="proj_fox_v_gates")

        bias_row = _lane_row(gdn_dt_bias[l], SMALL_GA) + _lane_row(fox_f_bias[l], SMALL_FF)
        small3, small_t = _gate_projection(x, gain, ws_hi, ws_lo, _lane_row(gdn_a_log[l], SMALL_GA),
                                           bias_row)
        seq = lambda a: a.reshape(b, t, -1)
        o_a = _gdn_branch(gq, gk, gv, gz, small3, small_t, gdn_norm_g[l][None, :], heads=4)
        o_b = _fox_branch(seq(fq), seq(fk), seq(fv), small_t)

        out = _merge_mlp(
            o_a.reshape(b * t, qk_w), o_b.reshape(b * t, fox_w), gate_a, gate_b, x2d,
            w_proj_gdn[l].astype(BF16), w_proj_fox[l].astype(BF16), w_out[l].astype(BF16),
            norm_mlp_g[l][None, :], w_up[l].astype(BF16), w_down[l].astype(BF16), tm=tm)
        x = out.reshape(b, t, d)
    return x
```

```python
import functools

import jax
import jax.numpy as jnp
from jax import lax
from jax.experimental import pallas as pl
from jax.experimental.pallas import tpu as pltpu

F32 = jnp.float32
BF16 = jnp.bfloat16

LANES = 128
VMEM_LIMIT_BYTES = 56 * 1024 * 1024

EPS = 1e-6
GDN_HEADS = 8
GDN_D = 128
GDN_CONV = 4
GDN_CHUNK = 64
GDN_INV_BLOCK = 16
FOX_HEADS = 8
FOX_D = 128
FOX_TQ = 256
SMALL_GA, SMALL_GB, SMALL_FF = 0, 8, 16
NEG_BIG = -1e30


def _dot(a, b):
    return jnp.dot(a, b, preferred_element_type=F32)


def _dot_nt(a, b):
    return lax.dot_general(a, b, (((1,), (1,)), ((), ())), preferred_element_type=F32)


def _sigmoid(x):
    return 1.0 / (1.0 + jnp.exp(-x))


def _softplus(x):
    return jnp.maximum(x, 0.0) + jnp.log1p(jnp.exp(-jnp.abs(x)))


def _rmsnorm(x, g):
    return x * lax.rsqrt(jnp.mean(x * x, axis=-1, keepdims=True) + EPS) * g


def _pick_lane(x, lane):
    ids = lax.broadcasted_iota(jnp.int32, x.shape, 1)
    col = jnp.sum(jnp.where(ids == lane, x, 0.0), axis=-1, keepdims=True)
    return jnp.broadcast_to(col, x.shape)


def _segment_cumsum(x, seg):
    pos = lax.broadcasted_iota(jnp.int32, x.shape, 0) % seg
    s = 1
    while s < seg:
        x = x + jnp.where(pos >= s, pltpu.roll(x, s, axis=0), 0.0)
        s *= 2
    return x


def _head_normalise(a, gain, *, mean, scale=1.0):
    heads = []
    for c0 in range(0, a.shape[1], LANES):
        a_h = a[:, c0:c0 + LANES]
        ss = jnp.sum(a_h * a_h, axis=-1, keepdims=True)
        if mean:
            ss = ss * (1.0 / LANES)
        inv = lax.rsqrt(ss + EPS)
        if scale != 1.0:
            inv = inv * scale
        heads.append(a_h * inv if gain is None else a_h * inv * gain[:, c0:c0 + LANES])
    return jnp.concatenate(heads, axis=1)


def _causal_conv_silu(y, tail, w):
    tm, cw = y.shape
    groups = y.reshape(tm // 8, 8, cw)
    row_in_group = lax.broadcasted_iota(jnp.int32, (1, 8, cw), 1)
    acc = None
    for i in range(GDN_CONV):
        s = GDN_CONV - 1 - i
        if s == 0:
            shifted = groups
        else:
            rotated = pltpu.roll(groups, s, axis=1)
            previous = jnp.concatenate([pltpu.roll(tail, s, axis=0)[None], rotated[:-1]], axis=0)
            shifted = jnp.where(row_in_group < s, previous, rotated)
        term = shifted * w[i:i + 1, :][None]
        acc = term if acc is None else acc + term
    acc = acc.reshape(tm, cw)
    return acc * _sigmoid(acc)


_CONV_MODES = ("conv", "conv_l2", "conv_l2_scaled")
_AUX_MODES = _CONV_MODES + ("rms",)


def _proj_kernel(x_ref, g_ref, w_ref, *refs, modes, head_major, width, chunk, tiles_per_seq):
    refs = list(refs)
    aux_ref = refs.pop(0) if any(mode in _AUX_MODES for mode in modes) else None
    outs = refs[:len(modes)]
    tail_s = refs[len(modes)] if len(refs) > len(modes) else None
    tm = x_ref.shape[0]
    u = _rmsnorm(x_ref[...], g_ref[...]).astype(BF16)
    sequence_start = (pl.program_id(0) % tiles_per_seq) == 0
    items = [(gi, c0) for gi in range(len(modes)) for c0 in range(0, width, chunk)]
    heavy = [item for item in items if modes[item[0]] in _CONV_MODES]
    light = [item for item in items if modes[item[0]] not in _CONV_MODES]
    per_heavy = len(light) // len(heavy) if heavy else 0
    order = []
    for item in heavy:
        order.append(item)
        order.extend(light[:per_heavy])
        light = light[per_heavy:]
    order.extend(light)
    for gi, c0 in order:
        mode, out, by_head = modes[gi], outs[gi], head_major[gi]
        cols = slice(gi * width + c0, gi * width + c0 + chunk)
        y = _dot(u, w_ref[:, cols])
        if mode in _CONV_MODES:
            tail = jnp.where(sequence_start, 0.0, tail_s[:, cols])
            tail_s[:, cols] = y[tm - 8:, :]
            y = _causal_conv_silu(y, tail, aux_ref[0:GDN_CONV, cols])
            if mode != "conv":
                scale = GDN_D ** -0.5 if mode == "conv_l2_scaled" else 1.0
                y = _head_normalise(y, None, mean=False, scale=scale)
        elif mode == "rms":
            y = _head_normalise(y, aux_ref[0:1, cols], mean=True)
        elif mode == "silu":
            y = y * _sigmoid(y)
        elif mode == "sigmoid":
            y = _sigmoid(y)
        else:
            assert mode == "copy", mode
        y = y.astype(out.dtype)
        if by_head:
            for j in range(chunk // LANES):
                out[0, c0 // LANES + j] = y[:, j * LANES:(j + 1) * LANES]
        else:
            out[:, c0:c0 + chunk] = y


def _in_projection(x2d, gain, w, aux, *, modes, head_major, tm, seq_len, name):
    m, d = x2d.shape
    n = w.shape[1]
    width = n // len(modes)
    heads = width // LANES
    tiles_per_seq = seq_len // tm
    assert (aux is not None) == any(mode in _AUX_MODES for mode in modes)
    needs_tail = any(mode in _CONV_MODES for mode in modes)
    const = lambda shape: pl.BlockSpec(shape, lambda i: (0, 0), pipeline_mode=pl.Buffered(1))
    operands = [x2d, gain, w] + ([aux] if aux is not None else [])
    flat_spec = pl.BlockSpec((tm, width), lambda i: (i, 0))
    flat_shape = jax.ShapeDtypeStruct((m, width), BF16)
    head_spec = pl.BlockSpec((1, heads, tm, LANES), lambda i: (i // tiles_per_seq, 0, i % tiles_per_seq, 0))
    head_shape = jax.ShapeDtypeStruct((m // seq_len, heads, seq_len, LANES), BF16)
    return pl.pallas_call(
        functools.partial(_proj_kernel, modes=modes, head_major=head_major, width=width, chunk=512,
                          tiles_per_seq=tiles_per_seq),
        grid=(m // tm,),
        in_specs=[pl.BlockSpec((tm, d), lambda i: (i, 0))] + [const(a.shape) for a in operands[1:]],
        out_specs=[head_spec if by_head else flat_spec for by_head in head_major],
        out_shape=[head_shape if by_head else flat_shape for by_head in head_major],
        scratch_shapes=[pltpu.VMEM((8, n), F32)] if needs_tail else [],
        compiler_params=pltpu.CompilerParams(
            dimension_semantics=("arbitrary",),
            vmem_limit_bytes=VMEM_LIMIT_BYTES),
        name=name,
    )(*operands)


def _gate_proj_kernel(x_ref, g_ref, ws_hi_ref, ws_lo_ref, alog_ref, bias_ref, out_ref, out_t_ref, loc_s):
    t = x_ref.shape[0]
    n_chunks = t // GDN_CHUNK
    u = _rmsnorm(x_ref[...], g_ref[...])
    u_hi = u.astype(BF16)
    u_lo = (u - u_hi.astype(F32)).astype(BF16)
    ws_hi = ws_hi_ref[...]
    logits = _dot(u_hi, ws_hi) + _dot(u_lo, ws_hi) + _dot(u_hi, ws_lo_ref[...])
    shifted = logits + bias_ref[...]
    log_decay = -jnp.exp(alog_ref[...]) * _softplus(shifted)
    beta = _sigmoid(logits)
    log_forget = -_softplus(-shifted)
    lane = lax.broadcasted_iota(jnp.int32, logits.shape, 1)
    vals = jnp.where(lane < SMALL_GB, log_decay, jnp.where(lane < SMALL_FF, beta, log_forget))
    local = _segment_cumsum(vals, GDN_CHUNK)
    loc_s[...] = local
    totals = loc_s[pl.ds(GDN_CHUNK - 1, n_chunks, stride=GDN_CHUNK), :]
    carried = _segment_cumsum(totals, n_chunks) - totals
    full = local + jnp.broadcast_to(carried[:, None, :], (n_chunks, GDN_CHUNK, LANES)).reshape(t, LANES)
    out = jnp.where(lane < SMALL_GB, local, jnp.where(lane < SMALL_FF, vals, full))
    out_ref[0] = out
    out_t_ref[0] = jnp.transpose(out)


def _gate_projection(x3, gain, ws_hi, ws_lo, alog_row, bias_row):
    b, t, d = x3.shape
    const = lambda shape: pl.BlockSpec(shape, lambda i: (0, 0))
    return pl.pallas_call(
        _gate_proj_kernel,
        grid=(b,),
        in_specs=[
            pl.BlockSpec((None, t, d), lambda i: (i, 0, 0)),
            const((1, d)), const((d, LANES)), const((d, LANES)), const((1, LANES)), const((1, LANES)),
        ],
        out_specs=[
            pl.BlockSpec((1, t, LANES), lambda i: (i, 0, 0)),
            pl.BlockSpec((1, LANES, t), lambda i: (i, 0, 0)),
        ],
        out_shape=[
            jax.ShapeDtypeStruct((b, t, LANES), F32),
            jax.ShapeDtypeStruct((b, LANES, t), F32),
        ],
        scratch_shapes=[pltpu.VMEM((t, LANES), F32)],
        compiler_params=pltpu.CompilerParams(
            dimension_semantics=("arbitrary",),
            vmem_limit_bytes=VMEM_LIMIT_BYTES),
        name="gate_projection",
    )(x3, gain, ws_hi, ws_lo, alog_row, bias_row)


def _unit_lower_inverse(a, eye, same_block):
    d = jnp.where(same_block, a, 0.0)
    off = a - d
    x = eye - d
    p = _dot(d.astype(BF16), d.astype(BF16))
    yield
    for step in range(3):
        p_bf = p.astype(BF16)
        x = x + _dot(x.astype(BF16), p_bf)
        if step < 2:
            p = _dot(p_bf, p_bf)
        yield
    x_bf = x.astype(BF16)
    m = _dot(x_bf, off.astype(BF16))
    yield
    m_bf = m.astype(BF16)
    m2 = _dot(m_bf, m_bf)
    yield
    im = eye - m
    y = im + _dot(im.astype(BF16), m2.astype(BF16))
    yield
    return _dot(y.astype(BF16), x_bf)


def _run_in_lockstep(generators):
    results = [None] * len(generators)
    live = list(enumerate(generators))
    while live:
        still_live = []
        for idx, gen in live:
            try:
                next(gen)
                still_live.append((idx, gen))
            except StopIteration as stop:
                results[idx] = stop.value
        live = still_live
    return results


def _gdn_kernel(q_ref, k_ref, v_ref, z_ref, sm_ref, gt_ref, ng_ref, o_ref,
                gcum_s, beta_s, pm_s, qm_s, r_s, u_s,
                *, heads, unroll_b):
    h0 = pl.program_id(1) * heads
    t = q_ref.shape[2]
    c = GDN_CHUNK
    d = GDN_D
    n_chunks = t // c

    ri = lax.broadcasted_iota(jnp.int32, (c, c), 0)
    ci = lax.broadcasted_iota(jnp.int32, (c, c), 1)
    tri_incl = ri >= ci
    tri_strict = ri > ci
    same_block = (ri // GDN_INV_BLOCK) == (ci // GDN_INV_BLOCK)
    eye = jnp.where(ri == ci, 1.0, 0.0).astype(F32)

    def chunk_prepare(q_bf, k_bf, v_bf, gb, bb, g_row):
        diff = gb[:, :c] - g_row
        decay = jnp.where(tri_incl, jnp.exp(jnp.where(tri_incl, diff, 0.0)), 0.0)
        q = q_bf.astype(F32)
        k = k_bf.astype(F32)
        v = v_bf.astype(F32)
        qk_kk = _dot_nt(jnp.concatenate([q_bf, k_bf], axis=0), k_bf)
        yield
        qk = qk_kk[:c, :]
        kk = qk_kk[c:, :]
        a_mat = jnp.where(tri_strict, bb[:, :c] * kk * decay, 0.0)
        t_inv = yield from _unit_lower_inverse(a_mat, eye, same_block)
        e_g = jnp.exp(gb)
        rhs = jnp.concatenate([v * bb, k * (bb * e_g)], axis=1).astype(BF16)
        sol = _dot(t_inv.astype(BF16), rhs).astype(BF16)
        yield
        g_last = gb[c - 1:c, :]
        k_dec = k * jnp.exp(g_last - gb)
        attn = qk * decay
        lhs = jnp.concatenate([jnp.transpose(k_dec), attn], axis=0).astype(BF16)
        fused = _dot(lhs, sol)
        q_mat = fused[:d, :d]
        p_mat = fused[:d, d:]
        u_mat = fused[d:, :d]
        r_mat = q * e_g - fused[d:, d:]
        return p_mat.astype(BF16), q_mat, r_mat.astype(BF16), u_mat

    def prepare_head(hh, carry):
        sm = sm_ref[0]
        gcum_s[hh] = _pick_lane(sm, SMALL_GA + h0 + hh)
        beta_s[hh] = _pick_lane(sm, SMALL_GB + h0 + hh)
        g_rows = gt_ref[0, pl.ds(h0 + hh, 1), :]
        spans = [slice(ch * c, (ch + 1) * c) for ch in range(n_chunks)]
        loaded = [(q_ref[0, hh, r, :], k_ref[0, hh, r, :], v_ref[0, hh, r, :],
                   gcum_s[hh, r, :], beta_s[hh, r, :], g_rows[:, r]) for r in spans]
        results = _run_in_lockstep([chunk_prepare(*args) for args in loaded])
        for ch, (r, (p_mat, q_mat, r_mat, u_mat)) in enumerate(zip(spans, results)):
            pm_s[hh, ch] = p_mat
            qm_s[hh, ch] = q_mat.astype(qm_s.dtype)
            r_s[hh, r, :] = r_mat
            u_s[hh, r, :] = u_mat.astype(u_s.dtype)
        return carry

    lax.fori_loop(0, heads, prepare_head, 0)

    ng = ng_ref[...]

    def scan_body(i, states):
        states = list(states)
        for u in range(unroll_b):
            ch = i * unroll_b + u
            r0 = pl.multiple_of(ch * c, c)
            rows = pl.ds(r0, c)
            products = []
            for hh in range(heads):
                s_bf = states[hh].astype(BF16)
                o = _dot(r_s[hh, rows, :], s_bf) + u_s[hh, rows, :]
                products.append((o, _dot(pm_s[hh, ch], s_bf)))
            for hh in range(heads):
                lanes = slice(hh * d, (hh + 1) * d)
                o, ps = products[hh]
                dec = jnp.exp(gcum_s[hh, pl.ds(r0 + c - 1, 1), :])
                states[hh] = states[hh] * dec - ps + qm_s[hh, ch]
                z_act = z_ref[0, hh, rows, :].astype(F32)
                o_ref[0, rows, lanes] = (_rmsnorm(o, ng) * z_act).astype(o_ref.dtype)
        return tuple(states)

    lax.fori_loop(0, n_chunks // unroll_b, scan_body,
                  tuple(jnp.zeros((d, d), F32) for _ in range(heads)))


def _gdn_branch(q4, k4, v4, z4, small3, small_t, norm_g, *, heads):
    b, _, t, d = q4.shape
    n_chunks = t // GDN_CHUNK
    head_spec = pl.BlockSpec((1, heads, t, d), lambda bi, hi: (bi, hi, 0, 0))
    row_spec = pl.BlockSpec((1, LANES), lambda bi, hi: (0, 0))
    return pl.pallas_call(
        functools.partial(_gdn_kernel, heads=heads, unroll_b=2),
        grid=(b, GDN_HEADS // heads),
        in_specs=[
            head_spec, head_spec, head_spec, head_spec,
            pl.BlockSpec((1, t, LANES), lambda bi, hi: (bi, 0, 0)),
            pl.BlockSpec((1, GDN_HEADS, t), lambda bi, hi: (bi, SMALL_GA // GDN_HEADS, 0)),
            row_spec,
        ],
        out_specs=pl.BlockSpec((1, t, heads * d), lambda bi, hi: (bi, 0, hi)),
        out_shape=jax.ShapeDtypeStruct((b, t, GDN_HEADS * d), BF16),
        scratch_shapes=[
            pltpu.VMEM((heads, t, LANES), F32),
            pltpu.VMEM((heads, t, LANES), F32),
            pltpu.VMEM((heads, n_chunks, d, d), BF16),
            pltpu.VMEM((heads, n_chunks, d, d), BF16),
            pltpu.VMEM((heads, t, d), BF16),
            pltpu.VMEM((heads, t, d), BF16),
        ],
        compiler_params=pltpu.CompilerParams(
            dimension_semantics=("arbitrary", "arbitrary"),
            vmem_limit_bytes=VMEM_LIMIT_BYTES),
        name="gdn_branch",
    )(q4, k4, v4, z4, small3, small_t, norm_g)


def _fox_kernel(q_ref, k_ref, v_ref, ct_ref, o_ref):
    h = pl.program_id(1)
    t = q_ref.shape[1]
    tq = FOX_TQ
    nq = t // tq
    c_row = ct_ref[0, pl.ds(h, 1), :]

    ri = lax.broadcasted_iota(jnp.int32, (tq, tq), 0)
    ci = lax.broadcasted_iota(jnp.int32, (tq, tq), 1)
    causal = ri >= ci

    def scores(qi):
        return _dot_nt(q_ref[0, qi * tq:(qi + 1) * tq, :], k_ref[0, :(qi + 1) * tq, :])

    def finish(qi, s):
        n_keys = (qi + 1) * tq
        s = s - c_row[:, :n_keys]
        diag = jnp.where(causal, s[:, n_keys - tq:], NEG_BIG)
        s = diag if qi == 0 else jnp.concatenate([s[:, :n_keys - tq], diag], axis=1)
        p = jnp.exp(s - jnp.max(s, axis=-1, keepdims=True))
        denom = jnp.sum(p, axis=-1, keepdims=True)
        o = _dot(p.astype(BF16), v_ref[0, :n_keys, :]) / denom
        o_ref[0, qi * tq:(qi + 1) * tq, :] = o.astype(o_ref.dtype)

    s_next = scores(0)
    for qi in range(nq):
        s_cur = s_next
        if qi + 1 < nq:
            s_next = scores(qi + 1)
        finish(qi, s_cur)


def _fox_branch(q3, k3, v3, small_t):
    b, t, _ = q3.shape
    d = FOX_D
    seq_spec = pl.BlockSpec((1, t, d), lambda bi, hi: (bi, 0, hi))
    return pl.pallas_call(
        _fox_kernel,
        grid=(b, FOX_HEADS),
        in_specs=[
            seq_spec, seq_spec, seq_spec,
            pl.BlockSpec((1, FOX_HEADS, t), lambda bi, hi: (bi, SMALL_FF // FOX_HEADS, 0)),
        ],
        out_specs=seq_spec,
        out_shape=jax.ShapeDtypeStruct((b, t, FOX_HEADS * d), BF16),
        compiler_params=pltpu.CompilerParams(
            dimension_semantics=("arbitrary", "arbitrary"),
            vmem_limit_bytes=VMEM_LIMIT_BYTES),
        name="fox_branch",
    )(q3, k3, v3, small_t)


def _merge_mlp_kernel(oa_ref, ob_ref, ga_ref, gb_ref, x_ref, pa_ref, pb_ref, wo_ref, ng_ref,
                      wu_ref, wd_ref, out_ref, *, ff_chunk):
    ya = _dot(oa_ref[...], pa_ref[...])
    yb = _dot(ob_ref[...], pb_ref[...])
    merged = ga_ref[...].astype(F32) * ya + gb_ref[...].astype(F32) * yb
    hid = x_ref[...] + _dot(merged.astype(BF16), wo_ref[...])
    hn = _rmsnorm(hid, ng_ref[...]).astype(BF16)
    acc = hid
    d_ff = wu_ref.shape[1]
    for c0 in range(0, d_ff, ff_chunk):
        up = _dot(hn, wu_ref[:, c0:c0 + ff_chunk])
        act = jnp.square(jnp.maximum(up, 0.0)).astype(BF16)
        acc = acc + _dot(act, wd_ref[c0:c0 + ff_chunk, :])
    out_ref[...] = acc


def _merge_mlp(oa, ob, gate_a, gate_b, x2d, pa, pb, wo, ng, wu, wd, *, tm):
    m, d = x2d.shape
    d_ff = wu.shape[1]
    tile = pl.BlockSpec((tm, d), lambda i: (i, 0))
    const = lambda shape: pl.BlockSpec(shape, lambda i: (0, 0), pipeline_mode=pl.Buffered(1))
    return pl.pallas_call(
        functools.partial(_merge_mlp_kernel, ff_chunk=1024),
        grid=(m // tm,),
        in_specs=[
            tile, tile, tile, tile, tile,
            const((d, d)), const((d, d)), const((d, d)), const((1, d)),
            const((d, d_ff)), const((d_ff, d)),
        ],
        out_specs=tile,
        out_shape=jax.ShapeDtypeStruct((m, d), F32),
        compiler_params=pltpu.CompilerParams(
            dimension_semantics=("arbitrary",),
            vmem_limit_bytes=VMEM_LIMIT_BYTES),
        name="merge_mlp",
    )(oa, ob, gate_a, gate_b, x2d, pa, pb, wo, ng, wu, wd)


def _lane_row(values, offset):
    row = jnp.zeros((1, LANES), F32)
    return row.at[0, offset:offset + values.shape[0]].set(values.astype(F32))


def kernel(x, norm_mix_g, w_in, gdn_conv_w, gdn_a_log, gdn_dt_bias, gdn_norm_g, fox_q_norm_g,
           fox_k_norm_g, fox_f_bias, w_proj_gdn, w_proj_fox, w_out, norm_mlp_g, w_up, w_down):
    b, t, d = x.shape
    depth = w_in.shape[0]
    qk_w = GDN_HEADS * GDN_D
    fox_w = FOX_HEADS * FOX_D
    o_gz = 3 * qk_w
    o_ga = 4 * qk_w
    o_gb = o_ga + GDN_HEADS
    o_fq = o_gb + GDN_HEADS
    o_fv = o_fq + 2 * fox_w
    o_ff = o_fq + 3 * fox_w
    o_gate = o_ff + FOX_HEADS
    tm = 512

    for l in range(depth):
        w = w_in[l]
        w_small = jnp.concatenate(
            [w[:, o_ga:o_fq], w[:, o_ff:o_gate],
             jnp.zeros((d, LANES - 2 * GDN_HEADS - FOX_HEADS), F32)], axis=1)
        ws_hi = w_small.astype(BF16)
        ws_lo = (w_small - ws_hi.astype(F32)).astype(BF16)

        x2d = x.reshape(b * t, d)
        gain = norm_mix_g[l][None, :]
        w_main = jnp.concatenate([w[:, :o_ga], w[:, o_fq:o_ff], w[:, o_gate:]], axis=1).astype(BF16)
        aux = jnp.zeros((GDN_CONV, w_main.shape[1]), F32)
        aux = aux.at[:, :o_gz].set(gdn_conv_w[l])
        aux = aux.at[0, o_ga:o_ga + fox_w].set(jnp.tile(fox_q_norm_g[l] * (FOX_D ** -0.5), FOX_HEADS))
        aux = aux.at[0, o_ga + fox_w:o_ga + 2 * fox_w].set(jnp.tile(fox_k_norm_g[l], FOX_HEADS))
        gq, gk, gv, gz, fq, fk, fv, gate_a, gate_b = _in_projection(
            x2d, gain, w_main, aux,
            modes=("conv_l2_scaled", "conv_l2", "conv", "silu", "rms", "rms", "copy", "sigmoid", "sigmoid"),
            head_major=(True, True, True, True, False, False, False, False, False),
            tm=tm, seq_len=t, name="in_projection")

        bias_row = _lane_row(gdn_dt_bias[l], SMALL_GA) + _lane_row(fox_f_bias[l], SMALL_FF)
        small3, small_t = _gate_projection(x, gain, ws_hi, ws_lo, _lane_row(gdn_a_log[l], SMALL_GA),
                                           bias_row)
        seq = lambda a: a.reshape(b, t, -1)
        o_a = _gdn_branch(gq, gk, gv, gz, small3, small_t, gdn_norm_g[l][None, :], heads=4)
        o_b = _fox_branch(seq(fq), seq(fk), seq(fv), small_t)

        out = _merge_mlp(
            o_a.reshape(b * t, qk_w), o_b.reshape(b * t, fox_w), gate_a, gate_b, x2d,
            w_proj_gdn[l].astype(BF16), w_proj_fox[l].astype(BF16), w_out[l].astype(BF16),
            norm_mlp_g[l][None, :], w_up[l].astype(BF16), w_down[l].astype(BF16), tm=tm)
        x = out.reshape(b, t, d)
    return x
```

```python
import functools

import jax
import jax.numpy as jnp
from jax import lax
from jax.experimental import pallas as pl
from jax.experimental.pallas import tpu as pltpu

F32 = jnp.float32
BF16 = jnp.bfloat16

LANES = 128
VMEM_LIMIT_BYTES = 56 * 1024 * 1024

EPS = 1e-6
GDN_HEADS = 8
GDN_D = 128
GDN_CONV = 4
GDN_CHUNK = 64
GDN_INV_BLOCK = 16
FOX_HEADS = 8
FOX_D = 128
FOX_TQ = 256
SMALL_GA, SMALL_GB, SMALL_FF = 0, 8, 16
NEG_BIG = -1e30
LOG2_E = 1.4426950408889634


def _dot(a, b):
    return jnp.dot(a, b, preferred_element_type=F32)


def _dot_nt(a, b):
    return lax.dot_general(a, b, (((1,), (1,)), ((), ())), preferred_element_type=F32)


def _sigmoid(x):
    return 1.0 / (1.0 + jnp.exp(-x))


def _softplus(x):
    return jnp.maximum(x, 0.0) + jnp.log1p(jnp.exp(-jnp.abs(x)))


def _rmsnorm(x, g):
    return x * lax.rsqrt(jnp.mean(x * x, axis=-1, keepdims=True) + EPS) * g


def _pick_lane(x, lane):
    ids = lax.broadcasted_iota(jnp.int32, x.shape, 1)
    col = jnp.sum(jnp.where(ids == lane, x, 0.0), axis=-1, keepdims=True)
    return jnp.broadcast_to(col, x.shape)


def _segment_cumsum(x, seg):
    pos = lax.broadcasted_iota(jnp.int32, x.shape, 0) % seg
    s = 1
    while s < seg:
        x = x + jnp.where(pos >= s, pltpu.roll(x, s, axis=0), 0.0)
        s *= 2
    return x


def _head_normalise(a, gain, *, mean, scale=1.0):
    heads = []
    for c0 in range(0, a.shape[1], LANES):
        a_h = a[:, c0:c0 + LANES]
        ss = jnp.sum(a_h * a_h, axis=-1, keepdims=True)
        if mean:
            ss = ss * (1.0 / LANES)
        inv = lax.rsqrt(ss + EPS)
        if scale != 1.0:
            inv = inv * scale
        heads.append(a_h * inv if gain is None else a_h * inv * gain[:, c0:c0 + LANES])
    return jnp.concatenate(heads, axis=1)


def _causal_conv_silu(y, tail, w):
    tm, cw = y.shape
    groups = y.reshape(tm // 8, 8, cw)
    row_in_group = lax.broadcasted_iota(jnp.int32, (1, 8, cw), 1)
    acc = None
    for i in range(GDN_CONV):
        s = GDN_CONV - 1 - i
        if s == 0:
            shifted = groups
        else:
            rotated = pltpu.roll(groups, s, axis=1)
            previous = jnp.concatenate([pltpu.roll(tail, s, axis=0)[None], rotated[:-1]], axis=0)
            shifted = jnp.where(row_in_group < s, previous, rotated)
        term = shifted * w[i:i + 1, :][None]
        acc = term if acc is None else acc + term
    acc = acc.reshape(tm, cw)
    return acc * _sigmoid(acc)


_CONV_MODES = ("conv", "conv_l2", "conv_l2_scaled")
_AUX_MODES = _CONV_MODES + ("rms",)


def _proj_kernel(x_ref, g_ref, w_ref, *refs, modes, head_major, width, chunk, tiles_per_seq):
    refs = list(refs)
    aux_ref = refs.pop(0) if any(mode in _AUX_MODES for mode in modes) else None
    outs = refs[:len(modes)]
    tail_s = refs[len(modes)] if len(refs) > len(modes) else None
    tm = x_ref.shape[0]
    u = _rmsnorm(x_ref[...], g_ref[...]).astype(BF16)
    sequence_start = (pl.program_id(0) % tiles_per_seq) == 0
    items = [(gi, c0) for gi in range(len(modes)) for c0 in range(0, width, chunk)]
    heavy = [item for item in items if modes[item[0]] in _CONV_MODES]
    light = [item for item in items if modes[item[0]] not in _CONV_MODES]
    per_heavy = len(light) // len(heavy) if heavy else 0
    order = []
    for item in heavy:
        order.append(item)
        order.extend(light[:per_heavy])
        light = light[per_heavy:]
    order.extend(light)
    for gi, c0 in order:
        mode, out, by_head = modes[gi], outs[gi], head_major[gi]
        cols = slice(gi * width + c0, gi * width + c0 + chunk)
        y = _dot(u, w_ref[:, cols])
        if mode in _CONV_MODES:
            tail = jnp.where(sequence_start, 0.0, tail_s[:, cols])
            tail_s[:, cols] = y[tm - 8:, :]
            y = _causal_conv_silu(y, tail, aux_ref[0:GDN_CONV, cols])
            if mode != "conv":
                scale = GDN_D ** -0.5 if mode == "conv_l2_scaled" else 1.0
                y = _head_normalise(y, None, mean=False, scale=scale)
        elif mode == "rms":
            y = _head_normalise(y, aux_ref[0:1, cols], mean=True)
        elif mode == "silu":
            y = y * _sigmoid(y)
        elif mode == "sigmoid":
            y = _sigmoid(y)
        else:
            assert mode == "copy", mode
        y = y.astype(out.dtype)
        if by_head:
            for j in range(chunk // LANES):
                out[0, c0 // LANES + j] = y[:, j * LANES:(j + 1) * LANES]
        else:
            out[:, c0:c0 + chunk] = y


def _in_projection(x2d, gain, w, aux, *, modes, head_major, tm, seq_len, name):
    m, d = x2d.shape
    n = w.shape[1]
    width = n // len(modes)
    heads = width // LANES
    tiles_per_seq = seq_len // tm
    assert (aux is not None) == any(mode in _AUX_MODES for mode in modes)
    needs_tail = any(mode in _CONV_MODES for mode in modes)
    const = lambda shape: pl.BlockSpec(shape, lambda i: (0, 0), pipeline_mode=pl.Buffered(1))
    operands = [x2d, gain, w] + ([aux] if aux is not None else [])
    flat_spec = pl.BlockSpec((tm, width), lambda i: (i, 0))
    flat_shape = jax.ShapeDtypeStruct((m, width), BF16)
    head_spec = pl.BlockSpec((1, heads, tm, LANES), lambda i: (i // tiles_per_seq, 0, i % tiles_per_seq, 0))
    head_shape = jax.ShapeDtypeStruct((m // seq_len, heads, seq_len, LANES), BF16)
    return pl.pallas_call(
        functools.partial(_proj_kernel, modes=modes, head_major=head_major, width=width, chunk=512,
                          tiles_per_seq=tiles_per_seq),
        grid=(m // tm,),
        in_specs=[pl.BlockSpec((tm, d), lambda i: (i, 0))] + [const(a.shape) for a in operands[1:]],
        out_specs=[head_spec if by_head else flat_spec for by_head in head_major],
        out_shape=[head_shape if by_head else flat_shape for by_head in head_major],
        scratch_shapes=[pltpu.VMEM((8, n), F32)] if needs_tail else [],
        compiler_params=pltpu.CompilerParams(
            dimension_semantics=("arbitrary",),
            vmem_limit_bytes=VMEM_LIMIT_BYTES),
        name=name,
    )(*operands)


def _gate_kernel(x_ref, g_ref, ws_hi_ref, ws_lo_ref, alog_ref, bias_ref, out_ref, out_t_ref, loc_s):
    t = x_ref.shape[0]
    n_chunks = t // GDN_CHUNK
    u = _rmsnorm(x_ref[...], g_ref[...])
    u_hi = u.astype(BF16)
    u_lo = (u - u_hi.astype(F32)).astype(BF16)
    ws_hi = ws_hi_ref[...]
    logits = _dot(u_hi, ws_hi) + _dot(u_lo, ws_hi) + _dot(u_hi, ws_lo_ref[...])
    shifted = logits + bias_ref[...]
    log_decay = -jnp.exp(alog_ref[...]) * _softplus(shifted)
    beta = _sigmoid(logits)
    log2_forget = -_softplus(-shifted) * LOG2_E
    lane = lax.broadcasted_iota(jnp.int32, logits.shape, 1)
    vals = jnp.where(lane < SMALL_GB, log_decay, jnp.where(lane < SMALL_FF, beta, log2_forget))
    local = _segment_cumsum(vals, GDN_CHUNK)
    loc_s[...] = local
    totals = loc_s[pl.ds(GDN_CHUNK - 1, n_chunks, stride=GDN_CHUNK), :]
    carried = _segment_cumsum(totals, n_chunks) - totals
    full = local + jnp.broadcast_to(carried[:, None, :], (n_chunks, GDN_CHUNK, LANES)).reshape(t, LANES)
    out = jnp.where(lane < SMALL_GB, local, jnp.where(lane < SMALL_FF, vals, full))
    out_ref[0] = out
    out_t_ref[0] = jnp.transpose(out)


def _gates(x3, gain, ws_hi, ws_lo, alog_row, bias_row):
    b, t, d = x3.shape
    const = lambda shape: pl.BlockSpec(shape, lambda i: (0, 0))
    return pl.pallas_call(
        _gate_kernel,
        grid=(b,),
        in_specs=[
            pl.BlockSpec((None, t, d), lambda i: (i, 0, 0)),
            const((1, d)), const((d, LANES)), const((d, LANES)), const((1, LANES)), const((1, LANES)),
        ],
        out_specs=[
            pl.BlockSpec((1, t, LANES), lambda i: (i, 0, 0)),
            pl.BlockSpec((1, LANES, t), lambda i: (i, 0, 0)),
        ],
        out_shape=[
            jax.ShapeDtypeStruct((b, t, LANES), F32),
            jax.ShapeDtypeStruct((b, LANES, t), F32),
        ],
        scratch_shapes=[pltpu.VMEM((t, LANES), F32)],
        compiler_params=pltpu.CompilerParams(
            dimension_semantics=("arbitrary",),
            vmem_limit_bytes=VMEM_LIMIT_BYTES),
        name="gates",
    )(x3, gain, ws_hi, ws_lo, alog_row, bias_row)


def _unit_lower_inverse(a2, eye_hi, same_block2, low):
    c = a2.shape[0]
    lo = lambda slab: slab[:, :c].astype(BF16)
    pick = lambda low_part, high_part: jnp.where(low, low_part, high_part).astype(BF16)
    e2 = jnp.where(same_block2, -a2, 0.0)
    off2 = a2 + e2
    y = eye_hi + e2
    r = _dot(lo(e2), e2.astype(BF16))
    yield
    r = _dot(lo(r), pick(r, y))
    yield
    y = y + r
    r = _dot(lo(r), pick(r, y))
    yield
    y = y + r
    t = y + _dot(lo(r), y.astype(BF16))
    yield
    t_low = pltpu.roll(t, c, axis=1)
    m2 = _dot(lo(t_low), off2.astype(BF16))
    yield
    r = _dot(lo(m2), pick(m2, t))
    yield
    v = t - r
    inv = v + _dot(lo(r), v.astype(BF16))
    yield
    return pltpu.roll(inv, c, axis=1)[:, :c]


def _run_in_lockstep(generators):
    results = [None] * len(generators)
    live = list(enumerate(generators))
    while live:
        still_live = []
        for idx, gen in live:
            try:
                next(gen)
                still_live.append((idx, gen))
            except StopIteration as stop:
                results[idx] = stop.value
        live = still_live
    return results


def _gdn_kernel(q_ref, k_ref, v_ref, z_ref, sm_ref, gt_ref, ng_ref, o_ref,
                gcum_s, beta_s, rp_s, qm_s, u_s,
                *, heads, unroll_b):
    h0 = pl.program_id(1) * heads
    t = q_ref.shape[2]
    c = GDN_CHUNK
    d = GDN_D
    n_chunks = t // c

    ri = lax.broadcasted_iota(jnp.int32, (c, 2 * c), 0)
    lane = lax.broadcasted_iota(jnp.int32, (c, 2 * c), 1)
    low = lane < c
    ci = jnp.where(low, lane, lane - c)
    tri_incl = ri >= ci
    tri_strict = ri > ci
    same_block = (ri // GDN_INV_BLOCK) == (ci // GDN_INV_BLOCK)
    eye_hi = jnp.where(lane == ri + c, 1.0, 0.0).astype(F32)

    def chunk_prepare(q_bf, k_bf, v_bf, gb, bb, g_row2):
        diff = gb - g_row2
        decay = jnp.where(tri_incl, jnp.exp(jnp.where(tri_incl, diff, 0.0)), 0.0)
        q = q_bf.astype(F32)
        k = k_bf.astype(F32)
        v = v_bf.astype(F32)
        qk_kk = _dot_nt(jnp.concatenate([q_bf, k_bf], axis=0),
                        jnp.concatenate([k_bf, k_bf], axis=0))
        yield
        qk = qk_kk[:c, :]
        kk = qk_kk[c:, :]
        a_mat = jnp.where(tri_strict, bb * kk * decay, 0.0)
        t_inv = yield from _unit_lower_inverse(a_mat, eye_hi, same_block, low)
        e_g = jnp.exp(gb)
        rhs = jnp.concatenate([v * bb, k * (bb * e_g)], axis=1).astype(BF16)
        sol = _dot(t_inv.astype(BF16), rhs).astype(BF16)
        yield
        g_last = gb[c - 1:c, :]
        k_dec = k * jnp.exp(g_last - gb)
        attn = (qk * decay)[:, :c]
        lhs = jnp.concatenate([jnp.transpose(k_dec), attn], axis=0).astype(BF16)
        fused = _dot(lhs, sol)
        q_mat = fused[:d, :d]
        p_mat = fused[:d, d:]
        u_mat = fused[d:, :d]
        r_mat = q * e_g - fused[d:, d:]
        return p_mat.astype(BF16), q_mat, r_mat.astype(BF16), u_mat

    def prepare_head(hh, carry):
        sm = sm_ref[0]
        gcum_s[hh] = _pick_lane(sm, SMALL_GA + h0 + hh)
        beta_s[hh] = _pick_lane(sm, SMALL_GB + h0 + hh)
        g_rows = gt_ref[0, pl.ds(h0 + hh, 1), :]
        spans = [slice(ch * c, (ch + 1) * c) for ch in range(n_chunks)]
        g_row_slabs = []
        for pair in range(n_chunks // 2):
            window = g_rows[:, pair * 2 * c:(pair + 1) * 2 * c]
            swapped = pltpu.roll(window, c, axis=1)
            g_row_slabs += [jnp.where(low[:1], window, swapped), jnp.where(low[:1], swapped, window)]
        loaded = [(q_ref[0, hh, r, :], k_ref[0, hh, r, :], v_ref[0, hh, r, :],
                   gcum_s[hh, r, :], beta_s[hh, r, :], g2) for r, g2 in zip(spans, g_row_slabs)]
        results = _run_in_lockstep([chunk_prepare(*args) for args in loaded])
        for ch, (r, (p_mat, q_mat, r_mat, u_mat)) in enumerate(zip(spans, results)):
            rp_s[hh, ch] = jnp.concatenate([r_mat, p_mat], axis=0)
            qm_s[hh, ch] = q_mat.astype(qm_s.dtype)
            u_s[hh, r, :] = u_mat.astype(u_s.dtype)
        return carry

    lax.fori_loop(0, heads, prepare_head, 0)

    ng = ng_ref[...]

    def scan_body(i, states):
        states = list(states)
        for u in range(unroll_b):
            ch = i * unroll_b + u
            r0 = pl.multiple_of(ch * c, c)
            rows = pl.ds(r0, c)
            products = []
            for hh in range(heads):
                both = _dot(rp_s[hh, ch], states[hh].astype(BF16))
                products.append((both[:c, :] + u_s[hh, rows, :], both[c:, :]))
            for hh in range(heads):
                lanes = slice(hh * d, (hh + 1) * d)
                o, ps = products[hh]
                dec = jnp.exp(gcum_s[hh, pl.ds(r0 + c - 1, 1), :])
                states[hh] = states[hh] * dec - ps + qm_s[hh, ch]
                z_act = z_ref[0, hh, rows, :].astype(F32)
                o_ref[0, rows, lanes] = (_rmsnorm(o, ng) * z_act).astype(o_ref.dtype)
        return tuple(states)

    lax.fori_loop(0, n_chunks // unroll_b, scan_body,
                  tuple(jnp.zeros((d, d), F32) for _ in range(heads)))


def _gdn_branch(q4, k4, v4, z4, small3, small_t, norm_g, *, heads):
    b, _, t, d = q4.shape
    n_chunks = t // GDN_CHUNK
    head_spec = pl.BlockSpec((1, heads, t, d), lambda bi, hi: (bi, hi, 0, 0))
    row_spec = pl.BlockSpec((1, LANES), lambda bi, hi: (0, 0))
    return pl.pallas_call(
        functools.partial(_gdn_kernel, heads=heads, unroll_b=2),
        grid=(b, GDN_HEADS // heads),
        in_specs=[
            head_spec, head_spec, head_spec, head_spec,
            pl.BlockSpec((1, t, LANES), lambda bi, hi: (bi, 0, 0)),
            pl.BlockSpec((1, GDN_HEADS, t), lambda bi, hi: (bi, SMALL_GA // GDN_HEADS, 0)),
            row_spec,
        ],
        out_specs=pl.BlockSpec((1, t, heads * d), lambda bi, hi: (bi, 0, hi)),
        out_shape=jax.ShapeDtypeStruct((b, t, GDN_HEADS * d), BF16),
        scratch_shapes=[
            pltpu.VMEM((heads, t, LANES), F32),
            pltpu.VMEM((heads, t, LANES), F32),
            pltpu.VMEM((heads, n_chunks, GDN_CHUNK + d, d), BF16),
            pltpu.VMEM((heads, n_chunks, d, d), BF16),
            pltpu.VMEM((heads, t, d), BF16),
        ],
        compiler_params=pltpu.CompilerParams(
            dimension_semantics=("arbitrary", "arbitrary"),
            vmem_limit_bytes=VMEM_LIMIT_BYTES),
        name="gdn_branch",
    )(q4, k4, v4, z4, small3, small_t, norm_g)


def _fox_kernel(q_ref, k_ref, v_ref, ct_ref, o_ref):
    h = pl.program_id(1)
    t = q_ref.shape[1]
    tq = FOX_TQ
    nq = t // tq
    c_row = ct_ref[0, pl.ds(h, 1), :]

    ri = lax.broadcasted_iota(jnp.int32, (tq, tq), 0)
    ci = lax.broadcasted_iota(jnp.int32, (tq, tq), 1)
    causal = ri >= ci

    def scores(qi):
        return _dot_nt(q_ref[0, qi * tq:(qi + 1) * tq, :], k_ref[0, :(qi + 1) * tq, :])

    def finish(qi, s):
        n_keys = (qi + 1) * tq
        s = s - c_row[:, :n_keys]
        diag = jnp.where(causal, s[:, n_keys - tq:], NEG_BIG)
        s = diag if qi == 0 else jnp.concatenate([s[:, :n_keys - tq], diag], axis=1)
        p = jnp.exp2(s - jnp.max(s, axis=-1, keepdims=True))
        denom = jnp.sum(p, axis=-1, keepdims=True)
        o = _dot(p.astype(BF16), v_ref[0, :n_keys, :]) / denom
        o_ref[0, qi * tq:(qi + 1) * tq, :] = o.astype(o_ref.dtype)

    s_next = scores(0)
    for qi in range(nq):
        s_cur = s_next
        if qi + 1 < nq:
            s_next = scores(qi + 1)
        finish(qi, s_cur)


def _fox_branch(q3, k3, v3, small_t):
    b, t, _ = q3.shape
    d = FOX_D
    seq_spec = pl.BlockSpec((1, t, d), lambda bi, hi: (bi, 0, hi))
    return pl.pallas_call(
        _fox_kernel,
        grid=(b, FOX_HEADS),
        in_specs=[
            seq_spec, seq_spec, seq_spec,
            pl.BlockSpec((1, FOX_HEADS, t), lambda bi, hi: (bi, SMALL_FF // FOX_HEADS, 0)),
        ],
        out_specs=seq_spec,
        out_shape=jax.ShapeDtypeStruct((b, t, FOX_HEADS * d), BF16),
        compiler_params=pltpu.CompilerParams(
            dimension_semantics=("arbitrary", "arbitrary"),
            vmem_limit_bytes=VMEM_LIMIT_BYTES),
        name="fox_branch",
    )(q3, k3, v3, small_t)


def _merge_mlp_kernel(oa_ref, ob_ref, ga_ref, gb_ref, x_ref, pa_ref, pb_ref, wo_ref, ng_ref,
                      wu_ref, wd_ref, out_ref, *, ff_chunk):
    ya = _dot(oa_ref[...], pa_ref[...])
    yb = _dot(ob_ref[...], pb_ref[...])
    merged = ga_ref[...].astype(F32) * ya + gb_ref[...].astype(F32) * yb
    hid = x_ref[...] + _dot(merged.astype(BF16), wo_ref[...])
    hn = _rmsnorm(hid, ng_ref[...]).astype(BF16)
    acc = hid
    d_ff = wu_ref.shape[1]
    for c0 in range(0, d_ff, ff_chunk):
        up = _dot(hn, wu_ref[:, c0:c0 + ff_chunk])
        act = jnp.square(jnp.maximum(up, 0.0)).astype(BF16)
        acc = acc + _dot(act, wd_ref[c0:c0 + ff_chunk, :])
    out_ref[...] = acc


def _merge_mlp(oa, ob, gate_a, gate_b, x2d, pa, pb, wo, ng, wu, wd, *, tm):
    m, d = x2d.shape
    d_ff = wu.shape[1]
    tile = pl.BlockSpec((tm, d), lambda i: (i, 0))
    const = lambda shape: pl.BlockSpec(shape, lambda i: (0, 0), pipeline_mode=pl.Buffered(1))
    return pl.pallas_call(
        functools.partial(_merge_mlp_kernel, ff_chunk=1024),
        grid=(m // tm,),
        in_specs=[
            tile, tile, tile, tile, tile,
            const((d, d)), const((d, d)), const((d, d)), const((1, d)),
            const((d, d_ff)), const((d_ff, d)),
        ],
        out_specs=tile,
        out_shape=jax.ShapeDtypeStruct((m, d), F32),
        compiler_params=pltpu.CompilerParams(
            dimension_semantics=("arbitrary",),
            vmem_limit_bytes=VMEM_LIMIT_BYTES),
        name="merge_mlp",
    )(oa, ob, gate_a, gate_b, x2d, pa, pb, wo, ng, wu, wd)


def _lane_row(values, offset):
    row = jnp.zeros((1, LANES), F32)
    return row.at[0, offset:offset + values.shape[0]].set(values.astype(F32))


def kernel(x, norm_mix_g, w_in, gdn_conv_w, gdn_a_log, gdn_dt_bias, gdn_norm_g, fox_q_norm_g,
           fox_k_norm_g, fox_f_bias, w_proj_gdn, w_proj_fox, w_out, norm_mlp_g, w_up, w_down):
    b, t, d = x.shape
    depth = w_in.shape[0]
    qk_w = GDN_HEADS * GDN_D
    fox_w = FOX_HEADS * FOX_D
    o_gz = 3 * qk_w
    o_ga = 4 * qk_w
    o_gb = o_ga + GDN_HEADS
    o_fq = o_gb + GDN_HEADS
    o_fv = o_fq + 2 * fox_w
    o_ff = o_fq + 3 * fox_w
    o_gate = o_ff + FOX_HEADS
    tm = 512

    for l in range(depth):
        w = w_in[l]
        w_small = jnp.concatenate(
            [w[:, o_ga:o_fq], w[:, o_ff:o_gate],
             jnp.zeros((d, LANES - 2 * GDN_HEADS - FOX_HEADS), F32)], axis=1)
        ws_hi = w_small.astype(BF16)
        ws_lo = (w_small - ws_hi.astype(F32)).astype(BF16)

        x2d = x.reshape(b * t, d)
        gain = norm_mix_g[l][None, :]
        w_main = jnp.concatenate([w[:, :o_ga], w[:, o_fq:o_ff], w[:, o_gate:]], axis=1).astype(BF16)
        aux = jnp.zeros((GDN_CONV, w_main.shape[1]), F32)
        aux = aux.at[:, :o_gz].set(gdn_conv_w[l])
        aux = aux.at[0, o_ga:o_ga + fox_w].set(
            jnp.tile(fox_q_norm_g[l] * (LOG2_E * FOX_D ** -0.5), FOX_HEADS))
        aux = aux.at[0, o_ga + fox_w:o_ga + 2 * fox_w].set(jnp.tile(fox_k_norm_g[l], FOX_HEADS))
        gq, gk, gv, gz, fq, fk, fv, gate_a, gate_b = _in_projection(
            x2d, gain, w_main, aux,
            modes=("conv_l2_scaled", "conv_l2", "conv", "silu", "rms", "rms", "copy", "sigmoid", "sigmoid"),
            head_major=(True, True, True, True, False, False, False, False, False),
            tm=tm, seq_len=t, name="in_projection")

        bias_row = _lane_row(gdn_dt_bias[l], SMALL_GA) + _lane_row(fox_f_bias[l], SMALL_FF)
        small3, small_t = _gates(x, gain, ws_hi, ws_lo, _lane_row(gdn_a_log[l], SMALL_GA), bias_row)
        seq = lambda a: a.reshape(b, t, -1)
        o_a = _gdn_branch(gq, gk, gv, gz, small3, small_t, gdn_norm_g[l][None, :], heads=4)
        o_b = _fox_branch(seq(fq), seq(fk), seq(fv), small_t)

        out = _merge_mlp(
            o_a.reshape(b * t, qk_w), o_b.reshape(b * t, fox_w), gate_a, gate_b, x2d,
            w_proj_gdn[l].astype(BF16), w_proj_fox[l].astype(BF16), w_out[l].astype(BF16),
            norm_mlp_g[l][None, :], w_up[l].astype(BF16), w_down[l].astype(BF16), tm=tm)
        x = out.reshape(b, t, d)
    return x
```

```python
import functools

import jax
import jax.numpy as jnp
from jax import lax
from jax.experimental import pallas as pl
from jax.experimental.pallas import tpu as pltpu

F32 = jnp.float32
BF16 = jnp.bfloat16

LANES = 128
VMEM_LIMIT_BYTES = 56 * 1024 * 1024

EPS = 1e-6
GDN_HEADS = 8
GDN_D = 128
GDN_CONV = 4
GDN_CHUNK = 64
GDN_INV_BLOCK = 16
FOX_HEADS = 8
FOX_D = 128
FOX_TQ = 256
SMALL_GA, SMALL_GB, SMALL_FF = 0, 8, 16
NEG_BIG = -1e30
LOG2_E = 1.4426950408889634


def _dot(a, b):
    return jnp.dot(a, b, preferred_element_type=F32)


def _dot_nt(a, b):
    return lax.dot_general(a, b, (((1,), (1,)), ((), ())), preferred_element_type=F32)


def _sigmoid(x):
    return 1.0 / (1.0 + jnp.exp(-x))


def _softplus(x):
    return jnp.maximum(x, 0.0) + jnp.log1p(jnp.exp(-jnp.abs(x)))


def _rmsnorm(x, g):
    return x * lax.rsqrt(jnp.mean(x * x, axis=-1, keepdims=True) + EPS) * g


def _pick_lane(x, lane):
    ids = lax.broadcasted_iota(jnp.int32, x.shape, 1)
    col = jnp.sum(jnp.where(ids == lane, x, 0.0), axis=-1, keepdims=True)
    return jnp.broadcast_to(col, x.shape)


def _segment_cumsum(x, seg):
    pos = lax.broadcasted_iota(jnp.int32, x.shape, 0) % seg
    s = 1
    while s < seg:
        x = x + jnp.where(pos >= s, pltpu.roll(x, s, axis=0), 0.0)
        s *= 2
    return x


def _head_normalise(a, gain, *, mean, scale=1.0):
    heads = []
    for c0 in range(0, a.shape[1], LANES):
        a_h = a[:, c0:c0 + LANES]
        ss = jnp.sum(a_h * a_h, axis=-1, keepdims=True)
        if mean:
            ss = ss * (1.0 / LANES)
        inv = lax.rsqrt(ss + EPS)
        if scale != 1.0:
            inv = inv * scale
        heads.append(a_h * inv if gain is None else a_h * inv * gain[:, c0:c0 + LANES])
    return jnp.concatenate(heads, axis=1)


def _causal_conv_silu(y, tail, w):
    tm, cw = y.shape
    groups = y.reshape(tm // 8, 8, cw)
    row_in_group = lax.broadcasted_iota(jnp.int32, (1, 8, cw), 1)
    acc = None
    for i in range(GDN_CONV):
        s = GDN_CONV - 1 - i
        if s == 0:
            shifted = groups
        else:
            rotated = pltpu.roll(groups, s, axis=1)
            previous = jnp.concatenate([pltpu.roll(tail, s, axis=0)[None], rotated[:-1]], axis=0)
            shifted = jnp.where(row_in_group < s, previous, rotated)
        term = shifted * w[i:i + 1, :][None]
        acc = term if acc is None else acc + term
    acc = acc.reshape(tm, cw)
    return acc * _sigmoid(acc)


_CONV_MODES = ("conv", "conv_l2", "conv_l2_scaled")
_AUX_MODES = _CONV_MODES + ("rms",)


def _proj_kernel(x_ref, g_ref, w_ref, *refs, modes, head_major, width, chunk, tiles_per_seq):
    refs = list(refs)
    aux_ref = refs.pop(0) if any(mode in _AUX_MODES for mode in modes) else None
    outs = refs[:len(modes)]
    tail_s = refs[len(modes)] if len(refs) > len(modes) else None
    tm = x_ref.shape[0]
    u = _rmsnorm(x_ref[...], g_ref[...]).astype(BF16)
    sequence_start = (pl.program_id(0) % tiles_per_seq) == 0
    items = [(gi, c0) for gi in range(len(modes)) for c0 in range(0, width, chunk)]
    heavy = [item for item in items if modes[item[0]] in _CONV_MODES]
    light = [item for item in items if modes[item[0]] not in _CONV_MODES]
    per_heavy = len(light) // len(heavy) if heavy else 0
    order = []
    for item in heavy:
        order.append(item)
        order.extend(light[:per_heavy])
        light = light[per_heavy:]
    order.extend(light)
    for gi, c0 in order:
        mode, out, by_head = modes[gi], outs[gi], head_major[gi]
        cols = slice(gi * width + c0, gi * width + c0 + chunk)
        y = _dot(u, w_ref[:, cols])
        if mode in _CONV_MODES:
            tail = jnp.where(sequence_start, 0.0, tail_s[:, cols])
            tail_s[:, cols] = y[tm - 8:, :]
            y = _causal_conv_silu(y, tail, aux_ref[0:GDN_CONV, cols])
            if mode != "conv":
                scale = GDN_D ** -0.5 if mode == "conv_l2_scaled" else 1.0
                y = _head_normalise(y, None, mean=False, scale=scale)
        elif mode == "rms":
            y = _head_normalise(y, aux_ref[0:1, cols], mean=True)
        elif mode == "silu":
            y = y * _sigmoid(y)
        elif mode == "sigmoid":
            y = _sigmoid(y)
        else:
            assert mode == "copy", mode
        y = y.astype(out.dtype)
        if by_head:
            for j in range(chunk // LANES):
                out[0, c0 // LANES + j] = y[:, j * LANES:(j + 1) * LANES]
        else:
            out[:, c0:c0 + chunk] = y


def _in_projection(x2d, gain, w, aux, *, modes, head_major, tm, seq_len, name):
    m, d = x2d.shape
    n = w.shape[1]
    width = n // len(modes)
    heads = width // LANES
    tiles_per_seq = seq_len // tm
    assert (aux is not None) == any(mode in _AUX_MODES for mode in modes)
    needs_tail = any(mode in _CONV_MODES for mode in modes)
    const = lambda shape: pl.BlockSpec(shape, lambda i: (0, 0), pipeline_mode=pl.Buffered(1))
    operands = [x2d, gain, w] + ([aux] if aux is not None else [])
    flat_spec = pl.BlockSpec((tm, width), lambda i: (i, 0))
    flat_shape = jax.ShapeDtypeStruct((m, width), BF16)
    head_spec = pl.BlockSpec((1, heads, tm, LANES), lambda i: (i // tiles_per_seq, 0, i % tiles_per_seq, 0))
    head_shape = jax.ShapeDtypeStruct((m // seq_len, heads, seq_len, LANES), BF16)
    return pl.pallas_call(
        functools.partial(_proj_kernel, modes=modes, head_major=head_major, width=width, chunk=512,
                          tiles_per_seq=tiles_per_seq),
        grid=(m // tm,),
        in_specs=[pl.BlockSpec((tm, d), lambda i: (i, 0))] + [const(a.shape) for a in operands[1:]],
        out_specs=[head_spec if by_head else flat_spec for by_head in head_major],
        out_shape=[head_shape if by_head else flat_shape for by_head in head_major],
        scratch_shapes=[pltpu.VMEM((8, n), F32)] if needs_tail else [],
        compiler_params=pltpu.CompilerParams(
            dimension_semantics=("arbitrary",),
            vmem_limit_bytes=VMEM_LIMIT_BYTES),
        name=name,
    )(*operands)


def _gate_kernel(x_ref, g_ref, ws_hi_ref, ws_lo_ref, alog_ref, bias_ref, out_ref, out_t_ref, loc_s):
    t = x_ref.shape[0]
    n_chunks = t // GDN_CHUNK
    u = _rmsnorm(x_ref[...], g_ref[...])
    u_hi = u.astype(BF16)
    u_lo = (u - u_hi.astype(F32)).astype(BF16)
    ws_hi = ws_hi_ref[...]
    logits = _dot(u_hi, ws_hi) + _dot(u_lo, ws_hi) + _dot(u_hi, ws_lo_ref[...])
    shifted = logits + bias_ref[...]
    log_decay = -jnp.exp(alog_ref[...]) * _softplus(shifted)
    beta = _sigmoid(logits)
    log2_forget = -_softplus(-shifted) * LOG2_E
    lane = lax.broadcasted_iota(jnp.int32, logits.shape, 1)
    vals = jnp.where(lane < SMALL_GB, log_decay, jnp.where(lane < SMALL_FF, beta, log2_forget))
    local = _segment_cumsum(vals, GDN_CHUNK)
    loc_s[...] = local
    totals = loc_s[pl.ds(GDN_CHUNK - 1, n_chunks, stride=GDN_CHUNK), :]
    carried = _segment_cumsum(totals, n_chunks) - totals
    full = local + jnp.broadcast_to(carried[:, None, :], (n_chunks, GDN_CHUNK, LANES)).reshape(t, LANES)
    out = jnp.where(lane < SMALL_GB, local, jnp.where(lane < SMALL_FF, vals, full))
    out_ref[0] = out
    out_t_ref[0] = jnp.transpose(out)


def _gates(x3, gain, ws_hi, ws_lo, alog_row, bias_row):
    b, t, d = x3.shape
    const = lambda shape: pl.BlockSpec(shape, lambda i: (0, 0))
    return pl.pallas_call(
        _gate_kernel,
        grid=(b,),
        in_specs=[
            pl.BlockSpec((None, t, d), lambda i: (i, 0, 0)),
            const((1, d)), const((d, LANES)), const((d, LANES)), const((1, LANES)), const((1, LANES)),
        ],
        out_specs=[
            pl.BlockSpec((1, t, LANES), lambda i: (i, 0, 0)),
            pl.BlockSpec((1, LANES, t), lambda i: (i, 0, 0)),
        ],
        out_shape=[
            jax.ShapeDtypeStruct((b, t, LANES), F32),
            jax.ShapeDtypeStruct((b, LANES, t), F32),
        ],
        scratch_shapes=[pltpu.VMEM((t, LANES), F32)],
        compiler_params=pltpu.CompilerParams(
            dimension_semantics=("arbitrary",),
            vmem_limit_bytes=VMEM_LIMIT_BYTES),
        name="gates",
    )(x3, gain, ws_hi, ws_lo, alog_row, bias_row)


def _unit_lower_inverse(a2, eye_hi, same_block2, low):
    c = a2.shape[0]
    lo = lambda slab: slab[:, :c].astype(BF16)
    pick = lambda low_part, high_part: jnp.where(low, low_part, high_part).astype(BF16)
    e2 = jnp.where(same_block2, -a2, 0.0)
    off2 = a2 + e2
    y = eye_hi + e2
    r = _dot(lo(e2), e2.astype(BF16))
    yield
    r = _dot(lo(r), pick(r, y))
    yield
    y = y + r
    r = _dot(lo(r), pick(r, y))
    yield
    y = y + r
    t = y + _dot(lo(r), y.astype(BF16))
    yield
    t_low = pltpu.roll(t, c, axis=1)
    m2 = _dot(lo(t_low), off2.astype(BF16))
    yield
    r = _dot(lo(m2), pick(m2, t))
    yield
    v = t - r
    inv = v + _dot(lo(r), v.astype(BF16))
    yield
    return pltpu.roll(inv, c, axis=1)[:, :c]


def _run_in_lockstep(generators):
    results = [None] * len(generators)
    live = list(enumerate(generators))
    while live:
        still_live = []
        for idx, gen in live:
            try:
                next(gen)
                still_live.append((idx, gen))
            except StopIteration as stop:
                results[idx] = stop.value
        live = still_live
    return results


def _gdn_kernel(q_ref, k_ref, v_ref, z_ref, sm_ref, gt_ref, ng_ref, o_ref,
                gcum_s, beta_s, rp_s, qm_s, u_s,
                *, heads, unroll_b):
    h0 = pl.program_id(1) * heads
    t = q_ref.shape[2]
    c = GDN_CHUNK
    d = GDN_D
    n_chunks = t // c

    ri = lax.broadcasted_iota(jnp.int32, (c, 2 * c), 0)
    lane = lax.broadcasted_iota(jnp.int32, (c, 2 * c), 1)
    low = lane < c
    ci = jnp.where(low, lane, lane - c)
    tri_incl = ri >= ci
    tri_strict = ri > ci
    same_block = (ri // GDN_INV_BLOCK) == (ci // GDN_INV_BLOCK)
    eye_hi = jnp.where(lane == ri + c, 1.0, 0.0).astype(F32)

    def chunk_prepare(q_bf, k_bf, v_bf, gb, bb, g_row2):
        diff = gb - g_row2
        decay = jnp.where(tri_incl, jnp.exp(jnp.where(tri_incl, diff, 0.0)), 0.0)
        q = q_bf.astype(F32)
        k = k_bf.astype(F32)
        v = v_bf.astype(F32)
        qk_kk = _dot_nt(jnp.concatenate([q_bf, k_bf], axis=0),
                        jnp.concatenate([k_bf, k_bf], axis=0))
        yield
        qk = qk_kk[:c, :]
        kk = qk_kk[c:, :]
        a_mat = jnp.where(tri_strict, bb * kk * decay, 0.0)
        t_inv = yield from _unit_lower_inverse(a_mat, eye_hi, same_block, low)
        e_g = jnp.exp(gb)
        rhs = jnp.concatenate([v * bb, k * (bb * e_g)], axis=1).astype(BF16)
        sol = _dot(t_inv.astype(BF16), rhs).astype(BF16)
        yield
        g_last = gb[c - 1:c, :]
        k_dec = k * jnp.exp(g_last - gb)
        attn = (qk * decay)[:, :c]
        lhs = jnp.concatenate([jnp.transpose(k_dec), attn], axis=0).astype(BF16)
        fused = _dot(lhs, sol)
        q_mat = fused[:d, :d]
        p_mat = fused[:d, d:]
        u_mat = fused[d:, :d]
        r_mat = q * e_g - fused[d:, d:]
        return p_mat.astype(BF16), q_mat, r_mat.astype(BF16), u_mat

    def prepare_head(hh, carry):
        sm = sm_ref[0]
        gcum_s[hh] = _pick_lane(sm, SMALL_GA + h0 + hh)
        beta_s[hh] = _pick_lane(sm, SMALL_GB + h0 + hh)
        g_rows = gt_ref[0, pl.ds(h0 + hh, 1), :]
        spans = [slice(ch * c, (ch + 1) * c) for ch in range(n_chunks)]
        g_row_slabs = []
        for pair in range(n_chunks // 2):
            window = g_rows[:, pair * 2 * c:(pair + 1) * 2 * c]
            swapped = pltpu.roll(window, c, axis=1)
            g_row_slabs += [jnp.where(low[:1], window, swapped), jnp.where(low[:1], swapped, window)]
        loaded = [(q_ref[0, hh, r, :], k_ref[0, hh, r, :], v_ref[0, hh, r, :],
                   gcum_s[hh, r, :], beta_s[hh, r, :], g2) for r, g2 in zip(spans, g_row_slabs)]
        results = _run_in_lockstep([chunk_prepare(*args) for args in loaded])
        for ch, (r, (p_mat, q_mat, r_mat, u_mat)) in enumerate(zip(spans, results)):
            rp_s[hh, ch] = jnp.concatenate([r_mat, p_mat], axis=0)
            qm_s[hh, ch] = q_mat.astype(qm_s.dtype)
            u_s[hh, r, :] = u_mat.astype(u_s.dtype)
        return carry

    lax.fori_loop(0, heads, prepare_head, 0)

    ng = ng_ref[...]

    def scan_body(i, states):
        states = list(states)
        for u in range(unroll_b):
            ch = i * unroll_b + u
            r0 = pl.multiple_of(ch * c, c)
            rows = pl.ds(r0, c)
            products = []
            for hh in range(heads):
                both = _dot(rp_s[hh, ch], states[hh].astype(BF16))
                products.append((both[:c, :] + u_s[hh, rows, :], both[c:, :]))
            for hh in range(heads):
                lanes = slice(hh * d, (hh + 1) * d)
                o, ps = products[hh]
                dec = jnp.exp(gcum_s[hh, pl.ds(r0 + c - 1, 1), :])
                states[hh] = states[hh] * dec - ps + qm_s[hh, ch]
                z_act = z_ref[0, hh, rows, :].astype(F32)
                o_ref[0, rows, lanes] = (_rmsnorm(o, ng) * z_act).astype(o_ref.dtype)
        return tuple(states)

    lax.fori_loop(0, n_chunks // unroll_b, scan_body,
                  tuple(jnp.zeros((d, d), F32) for _ in range(heads)))


def _gdn_branch(q4, k4, v4, z4, small3, small_t, norm_g, *, heads):
    b, _, t, d = q4.shape
    n_chunks = t // GDN_CHUNK
    head_spec = pl.BlockSpec((1, heads, t, d), lambda bi, hi: (bi, hi, 0, 0))
    row_spec = pl.BlockSpec((1, LANES), lambda bi, hi: (0, 0))
    return pl.pallas_call(
        functools.partial(_gdn_kernel, heads=heads, unroll_b=16),
        grid=(b, GDN_HEADS // heads),
        in_specs=[
            head_spec, head_spec, head_spec, head_spec,
            pl.BlockSpec((1, t, LANES), lambda bi, hi: (bi, 0, 0)),
            pl.BlockSpec((1, GDN_HEADS, t), lambda bi, hi: (bi, SMALL_GA // GDN_HEADS, 0)),
            row_spec,
        ],
        out_specs=pl.BlockSpec((1, t, heads * d), lambda bi, hi: (bi, 0, hi)),
        out_shape=jax.ShapeDtypeStruct((b, t, GDN_HEADS * d), BF16),
        scratch_shapes=[
            pltpu.VMEM((heads, t, LANES), F32),
            pltpu.VMEM((heads, t, LANES), F32),
            pltpu.VMEM((heads, n_chunks, GDN_CHUNK + d, d), BF16),
            pltpu.VMEM((heads, n_chunks, d, d), BF16),
            pltpu.VMEM((heads, t, d), BF16),
        ],
        compiler_params=pltpu.CompilerParams(
            dimension_semantics=("arbitrary", "arbitrary"),
            vmem_limit_bytes=VMEM_LIMIT_BYTES),
        name="gdn_branch",
    )(q4, k4, v4, z4, small3, small_t, norm_g)


def _fox_kernel(q_ref, k_ref, v_ref, ct_ref, o_ref):
    h = pl.program_id(1)
    t = q_ref.shape[1]
    tq = FOX_TQ
    nq = t // tq
    c_row = ct_ref[0, pl.ds(h, 1), :]

    ri = lax.broadcasted_iota(jnp.int32, (tq, tq), 0)
    ci = lax.broadcasted_iota(jnp.int32, (tq, tq), 1)
    causal = ri >= ci

    def scores(qi):
        return _dot_nt(q_ref[0, qi * tq:(qi + 1) * tq, :], k_ref[0, :(qi + 1) * tq, :])

    def finish(qi, s):
        n_keys = (qi + 1) * tq
        s = s - c_row[:, :n_keys]
        diag = jnp.where(causal, s[:, n_keys - tq:], NEG_BIG)
        s = diag if qi == 0 else jnp.concatenate([s[:, :n_keys - tq], diag], axis=1)
        p = jnp.exp2(s - jnp.max(s, axis=-1, keepdims=True))
        denom = jnp.sum(p, axis=-1, keepdims=True)
        o = _dot(p.astype(BF16), v_ref[0, :n_keys, :]) / denom
        o_ref[0, qi * tq:(qi + 1) * tq, :] = o.astype(o_ref.dtype)

    s_next = scores(0)
    for qi in range(nq):
        s_cur = s_next
        if qi + 1 < nq:
            s_next = scores(qi + 1)
        finish(qi, s_cur)


def _fox_branch(q3, k3, v3, small_t):
    b, t, _ = q3.shape
    d = FOX_D
    seq_spec = pl.BlockSpec((1, t, d), lambda bi, hi: (bi, 0, hi))
    return pl.pallas_call(
        _fox_kernel,
        grid=(b, FOX_HEADS),
        in_specs=[
            seq_spec, seq_spec, seq_spec,
            pl.BlockSpec((1, FOX_HEADS, t), lambda bi, hi: (bi, SMALL_FF // FOX_HEADS, 0)),
        ],
        out_specs=seq_spec,
        out_shape=jax.ShapeDtypeStruct((b, t, FOX_HEADS * d), BF16),
        compiler_params=pltpu.CompilerParams(
            dimension_semantics=("arbitrary", "arbitrary"),
            vmem_limit_bytes=VMEM_LIMIT_BYTES),
        name="fox_branch",
    )(q3, k3, v3, small_t)


def _merge_mlp_kernel(oa_ref, ob_ref, ga_ref, gb_ref, x_ref, pa_ref, pb_ref, wo_ref, ng_ref,
                      wu_ref, wd_ref, out_ref, *, ff_chunk):
    ya = _dot(oa_ref[...], pa_ref[...])
    yb = _dot(ob_ref[...], pb_ref[...])
    merged = ga_ref[...].astype(F32) * ya + gb_ref[...].astype(F32) * yb
    hid = x_ref[...] + _dot(merged.astype(BF16), wo_ref[...])
    hn = _rmsnorm(hid, ng_ref[...]).astype(BF16)
    acc = hid
    d_ff = wu_ref.shape[1]
    for c0 in range(0, d_ff, ff_chunk):
        up = _dot(hn, wu_ref[:, c0:c0 + ff_chunk])
        act = jnp.square(jnp.maximum(up, 0.0)).astype(BF16)
        acc = acc + _dot(act, wd_ref[c0:c0 + ff_chunk, :])
    out_ref[...] = acc


def _merge_mlp(oa, ob, gate_a, gate_b, x2d, pa, pb, wo, ng, wu, wd, *, tm):
    m, d = x2d.shape
    d_ff = wu.shape[1]
    tile = pl.BlockSpec((tm, d), lambda i: (i, 0))
    const = lambda shape: pl.BlockSpec(shape, lambda i: (0, 0), pipeline_mode=pl.Buffered(1))
    return pl.pallas_call(
        functools.partial(_merge_mlp_kernel, ff_chunk=1024),
        grid=(m // tm,),
        in_specs=[
            tile, tile, tile, tile, tile,
            const((d, d)), const((d, d)), const((d, d)), const((1, d)),
            const((d, d_ff)), const((d_ff, d)),
        ],
        out_specs=tile,
        out_shape=jax.ShapeDtypeStruct((m, d), F32),
        compiler_params=pltpu.CompilerParams(
            dimension_semantics=("arbitrary",),
            vmem_limit_bytes=VMEM_LIMIT_BYTES),
        name="merge_mlp",
    )(oa, ob, gate_a, gate_b, x2d, pa, pb, wo, ng, wu, wd)


def _lane_row(values, offset):
    row = jnp.zeros((1, LANES), F32)
    return row.at[0, offset:offset + values.shape[0]].set(values.astype(F32))


def kernel(x, norm_mix_g, w_in, gdn_conv_w, gdn_a_log, gdn_dt_bias, gdn_norm_g, fox_q_norm_g,
           fox_k_norm_g, fox_f_bias, w_proj_gdn, w_proj_fox, w_out, norm_mlp_g, w_up, w_down):
    b, t, d = x.shape
    depth = w_in.shape[0]
    qk_w = GDN_HEADS * GDN_D
    fox_w = FOX_HEADS * FOX_D
    o_gz = 3 * qk_w
    o_ga = 4 * qk_w
    o_gb = o_ga + GDN_HEADS
    o_fq = o_gb + GDN_HEADS
    o_fv = o_fq + 2 * fox_w
    o_ff = o_fq + 3 * fox_w
    o_gate = o_ff + FOX_HEADS
    tm = 512

    for l in range(depth):
        w = w_in[l]
        w_small = jnp.concatenate(
            [w[:, o_ga:o_fq], w[:, o_ff:o_gate],
             jnp.zeros((d, LANES - 2 * GDN_HEADS - FOX_HEADS), F32)], axis=1)
        ws_hi = w_small.astype(BF16)
        ws_lo = (w_small - ws_hi.astype(F32)).astype(BF16)

        x2d = x.reshape(b * t, d)
        gain = norm_mix_g[l][None, :]
        w_main = jnp.concatenate([w[:, :o_ga], w[:, o_fq:o_ff], w[:, o_gate:]], axis=1).astype(BF16)
        aux = jnp.zeros((GDN_CONV, w_main.shape[1]), F32)
        aux = aux.at[:, :o_gz].set(gdn_conv_w[l])
        aux = aux.at[0, o_ga:o_ga + fox_w].set(
            jnp.tile(fox_q_norm_g[l] * (LOG2_E * FOX_D ** -0.5), FOX_HEADS))
        aux = aux.at[0, o_ga + fox_w:o_ga + 2 * fox_w].set(jnp.tile(fox_k_norm_g[l], FOX_HEADS))
        gq, gk, gv, gz, fq, fk, fv, gate_a, gate_b = _in_projection(
            x2d, gain, w_main, aux,
            modes=("conv_l2_scaled", "conv_l2", "conv", "silu", "rms", "rms", "copy", "sigmoid", "sigmoid"),
            head_major=(True, True, True, True, False, False, False, False, False),
            tm=tm, seq_len=t, name="in_projection")

        bias_row = _lane_row(gdn_dt_bias[l], SMALL_GA) + _lane_row(fox_f_bias[l], SMALL_FF)
        small3, small_t = _gates(x, gain, ws_hi, ws_lo, _lane_row(gdn_a_log[l], SMALL_GA), bias_row)
        seq = lambda a: a.reshape(b, t, -1)
        o_a = _gdn_branch(gq, gk, gv, gz, small3, small_t, gdn_norm_g[l][None, :], heads=4)
        o_b = _fox_branch(seq(fq), seq(fk), seq(fv), small_t)

        out = _merge_mlp(
            o_a.reshape(b * t, qk_w), o_b.reshape(b * t, fox_w), gate_a, gate_b, x2d,
            w_proj_gdn[l].astype(BF16), w_proj_fox[l].astype(BF16), w_out[l].astype(BF16),
            norm_mlp_g[l][None, :], w_up[l].astype(BF16), w_down[l].astype(BF16), tm=tm)
        x = out.reshape(b, t, d)
    return x
```

```python
import functools

import jax
import jax.numpy as jnp
from jax import lax
from jax.experimental import pallas as pl
from jax.experimental.pallas import tpu as pltpu

F32 = jnp.float32
BF16 = jnp.bfloat16

LANES = 128
SUBLANES = 8
VMEM_LIMIT_BYTES = 56 * 1024 * 1024

EPS = 1e-6
GDN_HEADS = 8
GDN_D = 128
GDN_CONV = 4
GDN_CHUNK = 64
GDN_INV_BLOCK = 16
FOX_HEADS = 8
FOX_D = 128
FOX_TQ = 256
SMALL_GA, SMALL_GB, SMALL_FF = 0, 8, 16
NEG_BIG = -1e30
LOG2_E = 1.4426950408889634


def _dot(a, b):
    return jnp.dot(a, b, preferred_element_type=F32)


def _dot_nt(a, b):
    return lax.dot_general(a, b, (((1,), (1,)), ((), ())), preferred_element_type=F32)


def _sigmoid(x):
    return 1.0 / (1.0 + jnp.exp(-x))


def _softplus(x):
    return jnp.maximum(x, 0.0) + jnp.log1p(jnp.exp(-jnp.abs(x)))


def _rmsnorm(x, g):
    return x * lax.rsqrt(jnp.mean(x * x, axis=-1, keepdims=True) + EPS) * g


def _pick_lane(x, lane):
    ids = lax.broadcasted_iota(jnp.int32, x.shape, 1)
    col = jnp.sum(jnp.where(ids == lane, x, 0.0), axis=-1, keepdims=True)
    return jnp.broadcast_to(col, x.shape)


def _segment_cumsum(x, seg):
    pos = lax.broadcasted_iota(jnp.int32, x.shape, 0) % seg
    s = 1
    while s < seg:
        x = x + jnp.where(pos >= s, pltpu.roll(x, s, axis=0), 0.0)
        s *= 2
    return x


def _head_normalise(a, gain, *, mean, scale=1.0):
    heads = []
    for c0 in range(0, a.shape[1], LANES):
        a_h = a[:, c0:c0 + LANES]
        ss = jnp.sum(a_h * a_h, axis=-1, keepdims=True)
        if mean:
            ss = ss * (1.0 / LANES)
        inv = lax.rsqrt(ss + EPS)
        if scale != 1.0:
            inv = inv * scale
        heads.append(a_h * inv if gain is None else a_h * inv * gain[:, c0:c0 + LANES])
    return jnp.concatenate(heads, axis=1)


def _causal_conv_silu(y, tail, w):
    tm, cw = y.shape
    groups = y.reshape(tm // SUBLANES, SUBLANES, cw)
    row_in_group = lax.broadcasted_iota(jnp.int32, (1, SUBLANES, cw), 1)
    acc = None
    for i in range(GDN_CONV):
        s = GDN_CONV - 1 - i
        if s == 0:
            shifted = groups
        else:
            rotated = pltpu.roll(groups, s, axis=1)
            previous = jnp.concatenate([pltpu.roll(tail, s, axis=0)[None], rotated[:-1]], axis=0)
            shifted = jnp.where(row_in_group < s, previous, rotated)
        term = shifted * w[i:i + 1, :][None]
        acc = term if acc is None else acc + term
    acc = acc.reshape(tm, cw)
    return acc * _sigmoid(acc)


_CONV_MODES = ("conv", "conv_l2", "conv_l2_scaled")
_AUX_MODES = _CONV_MODES + ("rms",)


def _proj_kernel(x_ref, g_ref, w_ref, *refs, modes, head_major, width, chunk, tiles_per_seq):
    refs = list(refs)
    aux_ref = refs.pop(0) if any(mode in _AUX_MODES for mode in modes) else None
    outs = refs[:len(modes)]
    tail_s = refs[len(modes)] if len(refs) > len(modes) else None
    tm = x_ref.shape[0]
    u = _rmsnorm(x_ref[...], g_ref[...]).astype(BF16)
    sequence_start = (pl.program_id(0) % tiles_per_seq) == 0
    items = [(gi, c0) for gi in range(len(modes)) for c0 in range(0, width, chunk)]
    heavy = [item for item in items if modes[item[0]] in _CONV_MODES]
    light = [item for item in items if modes[item[0]] not in _CONV_MODES]
    per_heavy = len(light) // len(heavy) if heavy else 0
    order = []
    for item in heavy:
        order.append(item)
        order.extend(light[:per_heavy])
        light = light[per_heavy:]
    order.extend(light)
    for gi, c0 in order:
        mode, out, by_head = modes[gi], outs[gi], head_major[gi]
        cols = slice(gi * width + c0, gi * width + c0 + chunk)
        y = _dot(u, w_ref[:, cols])
        if mode in _CONV_MODES:
            tail = jnp.where(sequence_start, 0.0, tail_s[:, cols])
            tail_s[:, cols] = y[tm - SUBLANES:, :]
            y = _causal_conv_silu(y, tail, aux_ref[0:GDN_CONV, cols])
            if mode != "conv":
                scale = GDN_D ** -0.5 if mode == "conv_l2_scaled" else 1.0
                y = _head_normalise(y, None, mean=False, scale=scale)
        elif mode == "rms":
            y = _head_normalise(y, aux_ref[0:1, cols], mean=True)
        elif mode == "silu":
            y = y * _sigmoid(y)
        elif mode == "sigmoid":
            y = _sigmoid(y)
        else:
            assert mode == "copy", mode
        y = y.astype(out.dtype)
        if by_head:
            for j in range(chunk // LANES):
                out[0, c0 // LANES + j] = y[:, j * LANES:(j + 1) * LANES]
        else:
            out[:, c0:c0 + chunk] = y


def _in_projection(x2d, gain, w, aux, *, modes, head_major, tm, seq_len, name):
    m, d = x2d.shape
    n = w.shape[1]
    width = n // len(modes)
    heads = width // LANES
    tiles_per_seq = seq_len // tm
    assert (aux is not None) == any(mode in _AUX_MODES for mode in modes)
    needs_tail = any(mode in _CONV_MODES for mode in modes)
    const = lambda shape: pl.BlockSpec(shape, lambda i: (0, 0), pipeline_mode=pl.Buffered(1))
    operands = [x2d, gain, w] + ([aux] if aux is not None else [])
    flat_spec = pl.BlockSpec((tm, width), lambda i: (i, 0))
    flat_shape = jax.ShapeDtypeStruct((m, width), BF16)
    head_spec = pl.BlockSpec((1, heads, tm, LANES), lambda i: (i // tiles_per_seq, 0, i % tiles_per_seq, 0))
    head_shape = jax.ShapeDtypeStruct((m // seq_len, heads, seq_len, LANES), BF16)
    return pl.pallas_call(
        functools.partial(_proj_kernel, modes=modes, head_major=head_major, width=width, chunk=512,
                          tiles_per_seq=tiles_per_seq),
        grid=(m // tm,),
        in_specs=[pl.BlockSpec((tm, d), lambda i: (i, 0))] + [const(a.shape) for a in operands[1:]],
        out_specs=[head_spec if by_head else flat_spec for by_head in head_major],
        out_shape=[head_shape if by_head else flat_shape for by_head in head_major],
        scratch_shapes=[pltpu.VMEM((SUBLANES, n), F32)] if needs_tail else [],
        compiler_params=pltpu.CompilerParams(
            dimension_semantics=("arbitrary",),
            vmem_limit_bytes=VMEM_LIMIT_BYTES),
        name=name,
    )(*operands)


def _gate_kernel(x_ref, g_ref, ws_hi_ref, ws_lo_ref, alog_ref, bias_ref, out_ref, out_t_ref, loc_s):
    t = x_ref.shape[0]
    n_chunks = t // GDN_CHUNK
    u = _rmsnorm(x_ref[...], g_ref[...])
    u_hi = u.astype(BF16)
    u_lo = (u - u_hi.astype(F32)).astype(BF16)
    ws_hi = ws_hi_ref[...]
    logits = _dot(u_hi, ws_hi) + _dot(u_lo, ws_hi) + _dot(u_hi, ws_lo_ref[...])
    shifted = logits + bias_ref[...]
    log_decay = -jnp.exp(alog_ref[...]) * _softplus(shifted)
    beta = _sigmoid(logits)
    log2_forget = -_softplus(-shifted) * LOG2_E
    lane = lax.broadcasted_iota(jnp.int32, logits.shape, 1)
    vals = jnp.where(lane < SMALL_GB, log_decay, jnp.where(lane < SMALL_FF, beta, log2_forget))
    local = _segment_cumsum(vals, GDN_CHUNK)
    loc_s[...] = local
    totals = loc_s[pl.ds(GDN_CHUNK - 1, n_chunks, stride=GDN_CHUNK), :]
    carried = _segment_cumsum(totals, n_chunks) - totals
    full = local + jnp.broadcast_to(carried[:, None, :], (n_chunks, GDN_CHUNK, LANES)).reshape(t, LANES)
    out = jnp.where(lane < SMALL_GB, local, jnp.where(lane < SMALL_FF, vals, full))
    out_ref[0] = out
    out_t_ref[0] = jnp.transpose(out)


def _gates(x3, gain, ws_hi, ws_lo, alog_row, bias_row):
    b, t, d = x3.shape
    const = lambda shape: pl.BlockSpec(shape, lambda i: (0, 0))
    return pl.pallas_call(
        _gate_kernel,
        grid=(b,),
        in_specs=[
            pl.BlockSpec((None, t, d), lambda i: (i, 0, 0)),
            const((1, d)), const((d, LANES)), const((d, LANES)), const((1, LANES)), const((1, LANES)),
        ],
        out_specs=[
            pl.BlockSpec((1, t, LANES), lambda i: (i, 0, 0)),
            pl.BlockSpec((1, LANES, t), lambda i: (i, 0, 0)),
        ],
        out_shape=[
            jax.ShapeDtypeStruct((b, t, LANES), F32),
            jax.ShapeDtypeStruct((b, LANES, t), F32),
        ],
        scratch_shapes=[pltpu.VMEM((t, LANES), F32)],
        compiler_params=pltpu.CompilerParams(
            dimension_semantics=("arbitrary",),
            vmem_limit_bytes=VMEM_LIMIT_BYTES),
        name="gates",
    )(x3, gain, ws_hi, ws_lo, alog_row, bias_row)


def _unit_lower_inverse(a2, eye_hi, same_block2, low):
    c = a2.shape[0]
    lo = lambda slab: slab[:, :c].astype(BF16)
    pick = lambda low_part, high_part: jnp.where(low, low_part, high_part).astype(BF16)
    e2 = jnp.where(same_block2, -a2, 0.0)
    off2 = a2 + e2
    y = eye_hi + e2
    r = _dot(lo(e2), e2.astype(BF16))
    yield
    r = _dot(lo(r), pick(r, y))
    yield
    y = y + r
    r = _dot(lo(r), pick(r, y))
    yield
    y = y + r
    t = y + _dot(lo(r), y.astype(BF16))
    yield
    t_low = pltpu.roll(t, c, axis=1)
    m2 = _dot(lo(t_low), off2.astype(BF16))
    yield
    r = _dot(lo(m2), pick(m2, t))
    yield
    v = t - r
    inv = v + _dot(lo(r), v.astype(BF16))
    yield
    return pltpu.roll(inv, c, axis=1)[:, :c]


def _run_in_lockstep(generators):
    results = [None] * len(generators)
    live = list(enumerate(generators))
    while live:
        still_live = []
        for idx, gen in live:
            try:
                next(gen)
                still_live.append((idx, gen))
            except StopIteration as stop:
                results[idx] = stop.value
        live = still_live
    return results


def _gdn_kernel(q_ref, k_ref, v_ref, z_ref, sm_ref, gt_ref, ng_ref, o_ref,
                gcum_s, beta_s, rp_s, qm_s, u_s,
                *, heads, unroll_b, lockstep_chunks):
    h0 = pl.program_id(1) * heads
    t = q_ref.shape[2]
    c = GDN_CHUNK
    d = GDN_D
    n_chunks = t // c

    ri = lax.broadcasted_iota(jnp.int32, (c, 2 * c), 0)
    lane = lax.broadcasted_iota(jnp.int32, (c, 2 * c), 1)
    low = lane < c
    ci = jnp.where(low, lane, lane - c)
    tri_incl = ri >= ci
    tri_strict = ri > ci
    same_block = (ri // GDN_INV_BLOCK) == (ci // GDN_INV_BLOCK)
    eye_hi = jnp.where(lane == ri + c, 1.0, 0.0).astype(F32)

    def chunk_prepare(q_bf, k_bf, v_bf, gb, bb, g_row2):
        diff = gb - g_row2
        decay = jnp.where(tri_incl, jnp.exp(jnp.where(tri_incl, diff, 0.0)), 0.0)
        q = q_bf.astype(F32)
        k = k_bf.astype(F32)
        v = v_bf.astype(F32)
        qk_kk = _dot_nt(jnp.concatenate([q_bf, k_bf], axis=0),
                        jnp.concatenate([k_bf, k_bf], axis=0))
        yield
        qk = qk_kk[:c, :]
        kk = qk_kk[c:, :]
        a_mat = jnp.where(tri_strict, bb * kk * decay, 0.0)
        t_inv = yield from _unit_lower_inverse(a_mat, eye_hi, same_block, low)
        e_g = jnp.exp(gb)
        rhs = jnp.concatenate([v * bb, k * (bb * e_g)], axis=1).astype(BF16)
        sol = _dot(t_inv.astype(BF16), rhs).astype(BF16)
        yield
        g_last = gb[c - 1:c, :]
        k_dec = k * jnp.exp(g_last - gb)
        attn = (qk * decay)[:, :c]
        lhs = jnp.concatenate([jnp.transpose(k_dec), attn], axis=0).astype(BF16)
        fused = _dot(lhs, sol)
        q_mat = fused[:d, :d]
        p_mat = fused[:d, d:]
        u_mat = fused[d:, :d]
        r_mat = q * e_g - fused[d:, d:]
        return p_mat.astype(BF16), q_mat, r_mat.astype(BF16), u_mat

    def prepare_head(hh, carry):
        sm = sm_ref[0]
        gcum_s[hh] = _pick_lane(sm, SMALL_GA + h0 + hh)
        beta_s[hh] = _pick_lane(sm, SMALL_GB + h0 + hh)
        g_rows = gt_ref[0, pl.ds(h0 + hh, 1), :]
        spans = [slice(ch * c, (ch + 1) * c) for ch in range(n_chunks)]
        g_row_slabs = []
        for pair in range(n_chunks // 2):
            window = g_rows[:, pair * 2 * c:(pair + 1) * 2 * c]
            swapped = pltpu.roll(window, c, axis=1)
            g_row_slabs += [jnp.where(low[:1], window, swapped), jnp.where(low[:1], swapped, window)]
        for first in range(0, n_chunks, lockstep_chunks):
            group = range(first, first + lockstep_chunks)
            loaded = [(q_ref[0, hh, spans[ch], :], k_ref[0, hh, spans[ch], :], v_ref[0, hh, spans[ch], :],
                       gcum_s[hh, spans[ch], :], beta_s[hh, spans[ch], :], g_row_slabs[ch]) for ch in group]
            results = _run_in_lockstep([chunk_prepare(*args) for args in loaded])
            for ch, (p_mat, q_mat, r_mat, u_mat) in zip(group, results):
                rp_s[hh, ch] = jnp.concatenate([r_mat, p_mat], axis=0)
                qm_s[hh, ch] = q_mat.astype(qm_s.dtype)
                u_s[hh, spans[ch], :] = u_mat.astype(u_s.dtype)
        return carry

    lax.fori_loop(0, heads, prepare_head, 0)

    ng = ng_ref[...]

    def scan_body(i, states):
        states = list(states)
        for u in range(unroll_b):
            ch = i * unroll_b + u
            r0 = pl.multiple_of(ch * c, c)
            rows = pl.ds(r0, c)
            products = []
            for hh in range(heads):
                both = _dot(rp_s[hh, ch], states[hh].astype(BF16))
                products.append((both[:c, :] + u_s[hh, rows, :], both[c:, :]))
            for hh in range(heads):
                lanes = slice(hh * d, (hh + 1) * d)
                o, ps = products[hh]
                dec = jnp.exp(gcum_s[hh, pl.ds(r0 + c - 1, 1), :])
                states[hh] = states[hh] * dec - ps + qm_s[hh, ch]
                z_act = z_ref[0, hh, rows, :].astype(F32)
                o_ref[0, rows, lanes] = (_rmsnorm(o, ng) * z_act).astype(o_ref.dtype)
        return tuple(states)

    lax.fori_loop(0, n_chunks // unroll_b, scan_body,
                  tuple(jnp.zeros((d, d), F32) for _ in range(heads)))


def _gdn_branch(q4, k4, v4, z4, small3, small_t, norm_g, *, heads):
    b, _, t, d = q4.shape
    assert d == LANES == 2 * GDN_CHUNK, "chunk-local matrices are kept as [m | m] slabs one vreg wide"
    n_chunks = t // GDN_CHUNK
    head_spec = pl.BlockSpec((1, heads, t, d), lambda bi, hi: (bi, hi, 0, 0))
    row_spec = pl.BlockSpec((1, LANES), lambda bi, hi: (0, 0))
    return pl.pallas_call(
        functools.partial(_gdn_kernel, heads=heads, unroll_b=16, lockstep_chunks=n_chunks),
        grid=(b, GDN_HEADS // heads),
        in_specs=[
            head_spec, head_spec, head_spec, head_spec,
            pl.BlockSpec((1, t, LANES), lambda bi, hi: (bi, 0, 0)),
            pl.BlockSpec((1, GDN_HEADS, t), lambda bi, hi: (bi, SMALL_GA // GDN_HEADS, 0)),
            row_spec,
        ],
        out_specs=pl.BlockSpec((1, t, heads * d), lambda bi, hi: (bi, 0, hi)),
        out_shape=jax.ShapeDtypeStruct((b, t, GDN_HEADS * d), BF16),
        scratch_shapes=[
            pltpu.VMEM((heads, t, LANES), F32),
            pltpu.VMEM((heads, t, LANES), F32),
            pltpu.VMEM((heads, n_chunks, GDN_CHUNK + d, d), BF16),
            pltpu.VMEM((heads, n_chunks, d, d), BF16),
            pltpu.VMEM((heads, t, d), BF16),
        ],
        compiler_params=pltpu.CompilerParams(
            dimension_semantics=("arbitrary", "arbitrary"),
            vmem_limit_bytes=VMEM_LIMIT_BYTES),
        name="gdn_branch",
    )(q4, k4, v4, z4, small3, small_t, norm_g)


def _fox_kernel(q_ref, k_ref, v_ref, ct_ref, o_ref, *, heads):
    h0 = pl.program_id(1) * heads
    t = q_ref.shape[1]
    d = FOX_D
    tq = FOX_TQ
    nq = t // tq
    c_rows = [ct_ref[0, pl.ds(h0 + hh, 1), :] for hh in range(heads)]

    ri = lax.broadcasted_iota(jnp.int32, (tq, tq), 0)
    ci = lax.broadcasted_iota(jnp.int32, (tq, tq), 1)
    causal = ri >= ci

    def scores(hh, qi):
        lanes = slice(hh * d, (hh + 1) * d)
        return _dot_nt(q_ref[0, qi * tq:(qi + 1) * tq, lanes], k_ref[0, :(qi + 1) * tq, lanes])

    def finish(hh, qi, s):
        lanes = slice(hh * d, (hh + 1) * d)
        n_keys = (qi + 1) * tq
        s = s - c_rows[hh][:, :n_keys]
        diag = jnp.where(causal, s[:, n_keys - tq:], NEG_BIG)
        s = diag if qi == 0 else jnp.concatenate([s[:, :n_keys - tq], diag], axis=1)
        p = jnp.exp2(s - jnp.max(s, axis=-1, keepdims=True))
        denom = jnp.sum(p, axis=-1, keepdims=True)
        o = _dot(p.astype(BF16), v_ref[0, :n_keys, lanes]) / denom
        o_ref[0, qi * tq:(qi + 1) * tq, lanes] = o.astype(o_ref.dtype)

    work = [(hh, qi) for qi in range(nq) for hh in range(heads)]
    s_next = scores(*work[0])
    for idx, item in enumerate(work):
        s_cur = s_next
        if idx + 1 < len(work):
            s_next = scores(*work[idx + 1])
        finish(*item, s_cur)


def _fox_branch(q3, k3, v3, small_t, *, heads):
    b, t, _ = q3.shape
    seq_spec = pl.BlockSpec((1, t, heads * FOX_D), lambda bi, hi: (bi, 0, hi))
    return pl.pallas_call(
        functools.partial(_fox_kernel, heads=heads),
        grid=(b, FOX_HEADS // heads),
        in_specs=[
            seq_spec, seq_spec, seq_spec,
            pl.BlockSpec((1, FOX_HEADS, t), lambda bi, hi: (bi, SMALL_FF // FOX_HEADS, 0)),
        ],
        out_specs=seq_spec,
        out_shape=jax.ShapeDtypeStruct((b, t, FOX_HEADS * FOX_D), BF16),
        compiler_params=pltpu.CompilerParams(
            dimension_semantics=("arbitrary", "arbitrary"),
            vmem_limit_bytes=VMEM_LIMIT_BYTES),
        name="fox_branch",
    )(q3, k3, v3, small_t)


def _merge_mlp_kernel(oa_ref, ob_ref, ga_ref, gb_ref, x_ref, pa_ref, pb_ref, wo_ref, ng_ref,
                      wu_ref, wd_ref, out_ref, *, ff_chunk):
    ya = _dot(oa_ref[...], pa_ref[...])
    yb = _dot(ob_ref[...], pb_ref[...])
    merged = ga_ref[...].astype(F32) * ya + gb_ref[...].astype(F32) * yb
    hid = x_ref[...] + _dot(merged.astype(BF16), wo_ref[...])
    hn = _rmsnorm(hid, ng_ref[...]).astype(BF16)
    acc = hid
    d_ff = wu_ref.shape[1]
    for c0 in range(0, d_ff, ff_chunk):
        up = _dot(hn, wu_ref[:, c0:c0 + ff_chunk])
        act = jnp.square(jnp.maximum(up, 0.0)).astype(BF16)
        acc = acc + _dot(act, wd_ref[c0:c0 + ff_chunk, :])
    out_ref[...] = acc


def _merge_mlp(oa, ob, gate_a, gate_b, x2d, pa, pb, wo, ng, wu, wd, *, tm):
    m, d = x2d.shape
    d_ff = wu.shape[1]
    tile = pl.BlockSpec((tm, d), lambda i: (i, 0))
    const = lambda shape: pl.BlockSpec(shape, lambda i: (0, 0), pipeline_mode=pl.Buffered(1))
    return pl.pallas_call(
        functools.partial(_merge_mlp_kernel, ff_chunk=1024),
        grid=(m // tm,),
        in_specs=[
            tile, tile, tile, tile, tile,
            const((d, d)), const((d, d)), const((d, d)), const((1, d)),
            const((d, d_ff)), const((d_ff, d)),
        ],
        out_specs=tile,
        out_shape=jax.ShapeDtypeStruct((m, d), F32),
        compiler_params=pltpu.CompilerParams(
            dimension_semantics=("arbitrary",),
            vmem_limit_bytes=VMEM_LIMIT_BYTES),
        name="merge_mlp",
    )(oa, ob, gate_a, gate_b, x2d, pa, pb, wo, ng, wu, wd)


def _lane_row(values, offset):
    row = jnp.zeros((1, LANES), F32)
    return row.at[0, offset:offset + values.shape[0]].set(values.astype(F32))


def kernel(x, norm_mix_g, w_in, gdn_conv_w, gdn_a_log, gdn_dt_bias, gdn_norm_g, fox_q_norm_g,
           fox_k_norm_g, fox_f_bias, w_proj_gdn, w_proj_fox, w_out, norm_mlp_g, w_up, w_down):
    b, t, d = x.shape
    depth = w_in.shape[0]
    qk_w = GDN_HEADS * GDN_D
    fox_w = FOX_HEADS * FOX_D
    o_gz = 3 * qk_w
    o_ga = 4 * qk_w
    o_gb = o_ga + GDN_HEADS
    o_fq = o_gb + GDN_HEADS
    o_fv = o_fq + 2 * fox_w
    o_ff = o_fq + 3 * fox_w
    o_gate = o_ff + FOX_HEADS
    tm = 512

    for l in range(depth):
        w = w_in[l]
        w_small = jnp.concatenate(
            [w[:, o_ga:o_fq], w[:, o_ff:o_gate],
             jnp.zeros((d, LANES - 2 * GDN_HEADS - FOX_HEADS), F32)], axis=1)
        ws_hi = w_small.astype(BF16)
        ws_lo = (w_small - ws_hi.astype(F32)).astype(BF16)

        x2d = x.reshape(b * t, d)
        gain = norm_mix_g[l][None, :]
        w_main = jnp.concatenate([w[:, :o_ga], w[:, o_fq:o_ff], w[:, o_gate:]], axis=1).astype(BF16)
        aux = jnp.zeros((GDN_CONV, w_main.shape[1]), F32)
        aux = aux.at[:, :o_gz].set(gdn_conv_w[l])
        aux = aux.at[0, o_ga:o_ga + fox_w].set(
            jnp.tile(fox_q_norm_g[l] * (LOG2_E * FOX_D ** -0.5), FOX_HEADS))
        aux = aux.at[0, o_ga + fox_w:o_ga + 2 * fox_w].set(jnp.tile(fox_k_norm_g[l], FOX_HEADS))
        gq, gk, gv, gz, fq, fk, fv, gate_a, gate_b = _in_projection(
            x2d, gain, w_main, aux,
            modes=("conv_l2_scaled", "conv_l2", "conv", "silu", "rms", "rms", "copy", "sigmoid", "sigmoid"),
            head_major=(True, True, True, True, False, False, False, False, False),
            tm=tm, seq_len=t, name="in_projection")

        bias_row = _lane_row(gdn_dt_bias[l], SMALL_GA) + _lane_row(fox_f_bias[l], SMALL_FF)
        small3, small_t = _gates(x, gain, ws_hi, ws_lo, _lane_row(gdn_a_log[l], SMALL_GA), bias_row)
        seq = lambda a: a.reshape(b, t, -1)
        o_a = _gdn_branch(gq, gk, gv, gz, small3, small_t, gdn_norm_g[l][None, :], heads=4)
        o_b = _fox_branch(seq(fq), seq(fk), seq(fv), small_t, heads=2)

        out = _merge_mlp(
            o_a.reshape(b * t, qk_w), o_b.reshape(b * t, fox_w), gate_a, gate_b, x2d,
            w_proj_gdn[l].astype(BF16), w_proj_fox[l].astype(BF16), w_out[l].astype(BF16),
            norm_mlp_g[l][None, :], w_up[l].astype(BF16), w_down[l].astype(BF16), tm=tm)
        x = out.reshape(b, t, d)
    return x
```

```python
import functools

import jax
import jax.numpy as jnp
from jax import lax
from jax.experimental import pallas as pl
from jax.experimental.pallas import tpu as pltpu

F32 = jnp.float32
BF16 = jnp.bfloat16

LANES = 128
SUBLANES = 8
VMEM_LIMIT_BYTES = 56 * 1024 * 1024

EPS = 1e-6
GDN_HEADS = 8
GDN_D = 128
GDN_CONV = 4
GDN_CHUNK = 64
GDN_INV_BLOCK = 16
FOX_HEADS = 8
FOX_D = 128
FOX_TQ = 256
SMALL_GA, SMALL_GB, SMALL_FF = 0, 8, 16
NEG_BIG = -1e30
LOG2_E = 1.4426950408889634


def _dot(a, b):
    return jnp.dot(a, b, preferred_element_type=F32)


def _dot_nt(a, b):
    return lax.dot_general(a, b, (((1,), (1,)), ((), ())), preferred_element_type=F32)


def _sigmoid(x):
    return 1.0 / (1.0 + jnp.exp(-x))


def _softplus(x):
    return jnp.maximum(x, 0.0) + jnp.log1p(jnp.exp(-jnp.abs(x)))


def _rmsnorm(x, g):
    return x * lax.rsqrt(jnp.mean(x * x, axis=-1, keepdims=True) + EPS) * g


def _pick_lane(x, lane):
    ids = lax.broadcasted_iota(jnp.int32, x.shape, 1)
    col = jnp.sum(jnp.where(ids == lane, x, 0.0), axis=-1, keepdims=True)
    return jnp.broadcast_to(col, x.shape)


def _segment_cumsum(x, seg):
    pos = lax.broadcasted_iota(jnp.int32, x.shape, 0) % seg
    s = 1
    while s < seg:
        x = x + jnp.where(pos >= s, pltpu.roll(x, s, axis=0), 0.0)
        s *= 2
    return x


def _head_normalise(a, gain, *, mean, scale=1.0):
    heads = []
    for c0 in range(0, a.shape[1], LANES):
        a_h = a[:, c0:c0 + LANES]
        ss = jnp.sum(a_h * a_h, axis=-1, keepdims=True)
        if mean:
            ss = ss * (1.0 / LANES)
        inv = lax.rsqrt(ss + EPS)
        if scale != 1.0:
            inv = inv * scale
        heads.append(a_h * inv if gain is None else a_h * inv * gain[:, c0:c0 + LANES])
    return jnp.concatenate(heads, axis=1)


def _causal_conv_silu(y, tail, w):
    tm, cw = y.shape
    groups = y.reshape(tm // SUBLANES, SUBLANES, cw)
    row_in_group = lax.broadcasted_iota(jnp.int32, (1, SUBLANES, cw), 1)
    acc = None
    for i in range(GDN_CONV):
        s = GDN_CONV - 1 - i
        if s == 0:
            shifted = groups
        else:
            rotated = pltpu.roll(groups, s, axis=1)
            previous = jnp.concatenate([pltpu.roll(tail, s, axis=0)[None], rotated[:-1]], axis=0)
            shifted = jnp.where(row_in_group < s, previous, rotated)
        term = shifted * w[i:i + 1, :][None]
        acc = term if acc is None else acc + term
    acc = acc.reshape(tm, cw)
    return acc * _sigmoid(acc)


_CONV_MODES = ("conv", "conv_l2", "conv_l2_scaled")
_AUX_MODES = _CONV_MODES + ("rms",)


def _proj_kernel(x_ref, g_ref, w_ref, *refs, modes, head_major, width, chunk, tiles_per_seq):
    refs = list(refs)
    aux_ref = refs.pop(0) if any(mode in _AUX_MODES for mode in modes) else None
    outs = refs[:len(modes)]
    tail_s = refs[len(modes)] if len(refs) > len(modes) else None
    tm = x_ref.shape[0]
    u = _rmsnorm(x_ref[...], g_ref[...]).astype(BF16)
    sequence_start = (pl.program_id(0) % tiles_per_seq) == 0
    items = [(gi, c0) for gi in range(len(modes)) for c0 in range(0, width, chunk)]
    heavy = [item for item in items if modes[item[0]] in _CONV_MODES]
    light = [item for item in items if modes[item[0]] not in _CONV_MODES]
    per_heavy = len(light) // len(heavy) if heavy else 0
    order = []
    for item in heavy:
        order.append(item)
        order.extend(light[:per_heavy])
        light = light[per_heavy:]
    order.extend(light)
    for gi, c0 in order:
        mode, out, by_head = modes[gi], outs[gi], head_major[gi]
        cols = slice(gi * width + c0, gi * width + c0 + chunk)
        y = _dot(u, w_ref[:, cols])
        if mode in _CONV_MODES:
            tail = jnp.where(sequence_start, 0.0, tail_s[:, cols])
            tail_s[:, cols] = y[tm - SUBLANES:, :]
            y = _causal_conv_silu(y, tail, aux_ref[0:GDN_CONV, cols])
            if mode != "conv":
                scale = GDN_D ** -0.5 if mode == "conv_l2_scaled" else 1.0
                y = _head_normalise(y, None, mean=False, scale=scale)
        elif mode == "rms":
            y = _head_normalise(y, aux_ref[0:1, cols], mean=True)
        elif mode == "silu":
            y = y * _sigmoid(y)
        elif mode == "sigmoid":
            y = _sigmoid(y)
        else:
            assert mode == "copy", mode
        y = y.astype(out.dtype)
        if by_head:
            for j in range(chunk // LANES):
                out[0, c0 // LANES + j] = y[:, j * LANES:(j + 1) * LANES]
        else:
            out[:, c0:c0 + chunk] = y


def _in_projection(x2d, gain, w, aux, *, modes, head_major, tm, seq_len, name):
    m, d = x2d.shape
    n = w.shape[1]
    width = n // len(modes)
    heads = width // LANES
    tiles_per_seq = seq_len // tm
    assert (aux is not None) == any(mode in _AUX_MODES for mode in modes)
    needs_tail = any(mode in _CONV_MODES for mode in modes)
    const = lambda shape: pl.BlockSpec(shape, lambda i: (0, 0), pipeline_mode=pl.Buffered(1))
    operands = [x2d, gain, w] + ([aux] if aux is not None else [])
    flat_spec = pl.BlockSpec((tm, width), lambda i: (i, 0))
    flat_shape = jax.ShapeDtypeStruct((m, width), BF16)
    head_spec = pl.BlockSpec((1, heads, tm, LANES), lambda i: (i // tiles_per_seq, 0, i % tiles_per_seq, 0))
    head_shape = jax.ShapeDtypeStruct((m // seq_len, heads, seq_len, LANES), BF16)
    return pl.pallas_call(
        functools.partial(_proj_kernel, modes=modes, head_major=head_major, width=width, chunk=512,
                          tiles_per_seq=tiles_per_seq),
        grid=(m // tm,),
        in_specs=[pl.BlockSpec((tm, d), lambda i: (i, 0))] + [const(a.shape) for a in operands[1:]],
        out_specs=[head_spec if by_head else flat_spec for by_head in head_major],
        out_shape=[head_shape if by_head else flat_shape for by_head in head_major],
        scratch_shapes=[pltpu.VMEM((SUBLANES, n), F32)] if needs_tail else [],
        compiler_params=pltpu.CompilerParams(
            dimension_semantics=("arbitrary",),
            vmem_limit_bytes=VMEM_LIMIT_BYTES),
        name=name,
    )(*operands)


def _gate_kernel(x_ref, g_ref, ws_ref, alog_ref, bias_ref, out_ref, out_t_ref, loc_s):
    t = x_ref.shape[0]
    n_chunks = t // GDN_CHUNK
    u = _rmsnorm(x_ref[...], g_ref[...])
    u_hi = u.astype(BF16)
    u_lo = (u - u_hi.astype(F32)).astype(BF16)
    hi_terms = _dot(u_hi, ws_ref[...])
    logits = hi_terms[:, :LANES] + hi_terms[:, LANES:] + _dot(u_lo, ws_ref[:, :LANES])
    shifted = logits + bias_ref[...]
    log_decay = -jnp.exp(alog_ref[...]) * _softplus(shifted)
    beta = _sigmoid(logits)
    log2_forget = -_softplus(-shifted) * LOG2_E
    lane = lax.broadcasted_iota(jnp.int32, logits.shape, 1)
    vals = jnp.where(lane < SMALL_GB, log_decay, jnp.where(lane < SMALL_FF, beta, log2_forget))
    local = _segment_cumsum(vals, GDN_CHUNK)
    loc_s[...] = local
    totals = loc_s[pl.ds(GDN_CHUNK - 1, n_chunks, stride=GDN_CHUNK), :]
    carried = _segment_cumsum(totals, n_chunks) - totals
    full = local + jnp.broadcast_to(carried[:, None, :], (n_chunks, GDN_CHUNK, LANES)).reshape(t, LANES)
    out = jnp.where(lane < SMALL_GB, local, jnp.where(lane < SMALL_FF, vals, full))
    out_ref[0] = out
    out_t_ref[0] = jnp.transpose(out)


def _gates(x3, gain, ws_hi_lo, alog_row, bias_row):
    b, t, d = x3.shape
    const = lambda shape: pl.BlockSpec(shape, lambda i: (0, 0))
    return pl.pallas_call(
        _gate_kernel,
        grid=(b,),
        in_specs=[
            pl.BlockSpec((None, t, d), lambda i: (i, 0, 0)),
            const((1, d)), const((d, 2 * LANES)), const((1, LANES)), const((1, LANES)),
        ],
        out_specs=[
            pl.BlockSpec((1, t, LANES), lambda i: (i, 0, 0)),
            pl.BlockSpec((1, LANES, t), lambda i: (i, 0, 0)),
        ],
        out_shape=[
            jax.ShapeDtypeStruct((b, t, LANES), F32),
            jax.ShapeDtypeStruct((b, LANES, t), F32),
        ],
        scratch_shapes=[pltpu.VMEM((t, LANES), F32)],
        compiler_params=pltpu.CompilerParams(
            dimension_semantics=("arbitrary",),
            vmem_limit_bytes=VMEM_LIMIT_BYTES),
        name="gates",
    )(x3, gain, ws_hi_lo, alog_row, bias_row)


def _unit_lower_inverse(a2, eye_hi, same_block2, low):
    c = a2.shape[0]
    lo = lambda slab: slab[:, :c].astype(BF16)
    pick = lambda low_part, high_part: jnp.where(low, low_part, high_part).astype(BF16)
    e2 = jnp.where(same_block2, -a2, 0.0)
    off2 = a2 + e2
    y = eye_hi + e2
    r = _dot(lo(e2), e2.astype(BF16))
    yield
    r = _dot(lo(r), pick(r, y))
    yield
    y = y + r
    r = _dot(lo(r), pick(r, y))
    yield
    y = y + r
    t = y + _dot(lo(r), y.astype(BF16))
    yield
    t_low = pltpu.roll(t, c, axis=1)
    m2 = _dot(lo(t_low), off2.astype(BF16))
    yield
    r = _dot(lo(m2), pick(m2, t))
    yield
    v = t - r
    inv = v + _dot(lo(r), v.astype(BF16))
    yield
    return pltpu.roll(inv, c, axis=1)[:, :c]


def _run_in_lockstep(generators):
    results = [None] * len(generators)
    live = list(enumerate(generators))
    while live:
        still_live = []
        for idx, gen in live:
            try:
                next(gen)
                still_live.append((idx, gen))
            except StopIteration as stop:
                results[idx] = stop.value
        live = still_live
    return results


def _gdn_kernel(q_ref, k_ref, v_ref, z_ref, sm_ref, gt_ref, ng_ref, o_ref,
                gcum_s, beta_s, rp_s, qm_s, u_s,
                *, heads, unroll_b, lockstep_chunks):
    h0 = pl.program_id(1) * heads
    t = q_ref.shape[2]
    c = GDN_CHUNK
    d = GDN_D
    n_chunks = t // c

    ri = lax.broadcasted_iota(jnp.int32, (c, 2 * c), 0)
    lane = lax.broadcasted_iota(jnp.int32, (c, 2 * c), 1)
    low = lane < c
    ci = jnp.where(low, lane, lane - c)
    tri_incl = ri >= ci
    tri_strict = ri > ci
    same_block = (ri // GDN_INV_BLOCK) == (ci // GDN_INV_BLOCK)
    eye_hi = jnp.where(lane == ri + c, 1.0, 0.0).astype(F32)

    def chunk_prepare(q_bf, k_bf, v_bf, gb, bb, g_row2):
        diff = gb - g_row2
        decay = jnp.where(tri_incl, jnp.exp(jnp.where(tri_incl, diff, 0.0)), 0.0)
        q = q_bf.astype(F32)
        k = k_bf.astype(F32)
        v = v_bf.astype(F32)
        qk_kk = _dot_nt(jnp.concatenate([q_bf, k_bf], axis=0),
                        jnp.concatenate([k_bf, k_bf], axis=0))
        yield
        qk = qk_kk[:c, :]
        kk = qk_kk[c:, :]
        a_mat = jnp.where(tri_strict, bb * kk * decay, 0.0)
        t_inv = yield from _unit_lower_inverse(a_mat, eye_hi, same_block, low)
        e_g = jnp.exp(gb)
        rhs = jnp.concatenate([v * bb, k * (bb * e_g)], axis=1).astype(BF16)
        sol = _dot(t_inv.astype(BF16), rhs).astype(BF16)
        yield
        g_last = gb[c - 1:c, :]
        k_dec = k * jnp.exp(g_last - gb)
        attn = (qk * decay)[:, :c]
        lhs = jnp.concatenate([jnp.transpose(k_dec), attn], axis=0).astype(BF16)
        fused = _dot(lhs, sol)
        q_mat = fused[:d, :d]
        p_mat = fused[:d, d:]
        u_mat = fused[d:, :d]
        r_mat = q * e_g - fused[d:, d:]
        return p_mat.astype(BF16), q_mat, r_mat.astype(BF16), u_mat

    def prepare_head(hh, carry):
        sm = sm_ref[0]
        gcum_s[hh] = _pick_lane(sm, SMALL_GA + h0 + hh)
        beta_s[hh] = _pick_lane(sm, SMALL_GB + h0 + hh)
        g_rows = gt_ref[0, pl.ds(h0 + hh, 1), :]
        spans = [slice(ch * c, (ch + 1) * c) for ch in range(n_chunks)]
        g_row_slabs = []
        for pair in range(n_chunks // 2):
            window = g_rows[:, pair * 2 * c:(pair + 1) * 2 * c]
            swapped = pltpu.roll(window, c, axis=1)
            g_row_slabs += [jnp.where(low[:1], window, swapped), jnp.where(low[:1], swapped, window)]
        for first in range(0, n_chunks, lockstep_chunks):
            group = range(first, first + lockstep_chunks)
            loaded = [(q_ref[0, hh, spans[ch], :], k_ref[0, hh, spans[ch], :], v_ref[0, hh, spans[ch], :],
                       gcum_s[hh, spans[ch], :], beta_s[hh, spans[ch], :], g_row_slabs[ch]) for ch in group]
            results = _run_in_lockstep([chunk_prepare(*args) for args in loaded])
            for ch, (p_mat, q_mat, r_mat, u_mat) in zip(group, results):
                rp_s[hh, ch] = jnp.concatenate([r_mat, p_mat], axis=0)
                qm_s[hh, ch] = q_mat.astype(qm_s.dtype)
                u_s[hh, spans[ch], :] = u_mat.astype(u_s.dtype)
        return carry

    lax.fori_loop(0, heads, prepare_head, 0)

    ng = ng_ref[...]

    def scan_body(i, states):
        states = list(states)
        for u in range(unroll_b):
            ch = i * unroll_b + u
            r0 = pl.multiple_of(ch * c, c)
            rows = pl.ds(r0, c)
            products = []
            for hh in range(heads):
                both = _dot(rp_s[hh, ch], states[hh].astype(BF16))
                products.append((both[:c, :] + u_s[hh, rows, :], both[c:, :]))
            for hh in range(heads):
                lanes = slice(hh * d, (hh + 1) * d)
                o, ps = products[hh]
                dec = jnp.exp(gcum_s[hh, pl.ds(r0 + c - 1, 1), :])
                states[hh] = states[hh] * dec - ps + qm_s[hh, ch]
                z_act = z_ref[0, hh, rows, :].astype(F32)
                o_ref[0, rows, lanes] = (_rmsnorm(o, ng) * z_act).astype(o_ref.dtype)
        return tuple(states)

    lax.fori_loop(0, n_chunks // unroll_b, scan_body,
                  tuple(jnp.zeros((d, d), F32) for _ in range(heads)))


def _gdn_branch(q4, k4, v4, z4, small3, small_t, norm_g, *, heads):
    b, _, t, d = q4.shape
    assert d == LANES == 2 * GDN_CHUNK, "chunk-local matrices are kept as [m | m] slabs one vreg wide"
    n_chunks = t // GDN_CHUNK
    head_spec = pl.BlockSpec((1, heads, t, d), lambda bi, hi: (bi, hi, 0, 0))
    row_spec = pl.BlockSpec((1, LANES), lambda bi, hi: (0, 0))
    return pl.pallas_call(
        functools.partial(_gdn_kernel, heads=heads, unroll_b=16, lockstep_chunks=n_chunks),
        grid=(b, GDN_HEADS // heads),
        in_specs=[
            head_spec, head_spec, head_spec, head_spec,
            pl.BlockSpec((1, t, LANES), lambda bi, hi: (bi, 0, 0)),
            pl.BlockSpec((1, GDN_HEADS, t), lambda bi, hi: (bi, SMALL_GA // GDN_HEADS, 0)),
            row_spec,
        ],
        out_specs=pl.BlockSpec((1, t, heads * d), lambda bi, hi: (bi, 0, hi)),
        out_shape=jax.ShapeDtypeStruct((b, t, GDN_HEADS * d), BF16),
        scratch_shapes=[
            pltpu.VMEM((heads, t, LANES), F32),
            pltpu.VMEM((heads, t, LANES), F32),
            pltpu.VMEM((heads, n_chunks, GDN_CHUNK + d, d), BF16),
            pltpu.VMEM((heads, n_chunks, d, d), BF16),
            pltpu.VMEM((heads, t, d), BF16),
        ],
        compiler_params=pltpu.CompilerParams(
            dimension_semantics=("arbitrary", "arbitrary"),
            vmem_limit_bytes=VMEM_LIMIT_BYTES),
        name="gdn_branch",
    )(q4, k4, v4, z4, small3, small_t, norm_g)


def _fox_kernel(q_ref, k_ref, v_ref, ct_ref, o_ref, v1_s, *, heads):
    h0 = pl.program_id(1) * heads
    t = q_ref.shape[1]
    d = FOX_D
    tq = FOX_TQ
    nq = t // tq
    c_rows = [ct_ref[0, pl.ds(h0 + hh, 1), :] for hh in range(heads)]
    for hh in range(heads):
        v1_s[hh, :, :d] = v_ref[0, :, hh * d:(hh + 1) * d]
        v1_s[hh, :, d:] = jnp.ones((t, d), BF16)

    ri = lax.broadcasted_iota(jnp.int32, (tq, tq), 0)
    ci = lax.broadcasted_iota(jnp.int32, (tq, tq), 1)
    causal = ri >= ci

    def scores(hh, qi):
        lanes = slice(hh * d, (hh + 1) * d)
        return _dot_nt(q_ref[0, qi * tq:(qi + 1) * tq, lanes], k_ref[0, :(qi + 1) * tq, lanes])

    def finish(hh, qi, s):
        lanes = slice(hh * d, (hh + 1) * d)
        n_keys = (qi + 1) * tq
        s = s - c_rows[hh][:, :n_keys]
        diag = jnp.where(causal, s[:, n_keys - tq:], NEG_BIG)
        s = diag if qi == 0 else jnp.concatenate([s[:, :n_keys - tq], diag], axis=1)
        p = jnp.exp2(s - jnp.max(s, axis=-1, keepdims=True))
        pv = _dot(p.astype(BF16), v1_s[hh, :n_keys, :])
        o_ref[0, qi * tq:(qi + 1) * tq, lanes] = (pv[:, :d] / pv[:, d:]).astype(o_ref.dtype)

    work = [(hh, qi) for qi in range(nq) for hh in range(heads)]
    s_next = scores(*work[0])
    for idx, item in enumerate(work):
        s_cur = s_next
        if idx + 1 < len(work):
            s_next = scores(*work[idx + 1])
        finish(*item, s_cur)


def _fox_branch(q3, k3, v3, small_t, *, heads):
    b, t, _ = q3.shape
    seq_spec = pl.BlockSpec((1, t, heads * FOX_D), lambda bi, hi: (bi, 0, hi))
    return pl.pallas_call(
        functools.partial(_fox_kernel, heads=heads),
        grid=(b, FOX_HEADS // heads),
        in_specs=[
            seq_spec, seq_spec, seq_spec,
            pl.BlockSpec((1, FOX_HEADS, t), lambda bi, hi: (bi, SMALL_FF // FOX_HEADS, 0)),
        ],
        out_specs=seq_spec,
        out_shape=jax.ShapeDtypeStruct((b, t, FOX_HEADS * FOX_D), BF16),
        scratch_shapes=[pltpu.VMEM((heads, t, 2 * FOX_D), BF16)],
        compiler_params=pltpu.CompilerParams(
            dimension_semantics=("arbitrary", "arbitrary"),
            vmem_limit_bytes=VMEM_LIMIT_BYTES),
        name="fox_branch",
    )(q3, k3, v3, small_t)


def _merge_mlp_kernel(oa_ref, ob_ref, ga_ref, gb_ref, x_ref, pa_ref, pb_ref, wo_ref, ng_ref,
                      wu_ref, wd_ref, out_ref, *, ff_chunk):
    ya = _dot(oa_ref[...], pa_ref[...])
    yb = _dot(ob_ref[...], pb_ref[...])
    merged = ga_ref[...].astype(F32) * ya + gb_ref[...].astype(F32) * yb
    hid = x_ref[...] + _dot(merged.astype(BF16), wo_ref[...])
    hn = _rmsnorm(hid, ng_ref[...]).astype(BF16)
    acc = hid
    d_ff = wu_ref.shape[1]
    for c0 in range(0, d_ff, ff_chunk):
        up = _dot(hn, wu_ref[:, c0:c0 + ff_chunk])
        act = jnp.square(jnp.maximum(up, 0.0)).astype(BF16)
        acc = acc + _dot(act, wd_ref[c0:c0 + ff_chunk, :])
    out_ref[...] = acc


def _merge_mlp(oa, ob, gate_a, gate_b, x2d, pa, pb, wo, ng, wu, wd, *, tm):
    m, d = x2d.shape
    d_ff = wu.shape[1]
    tile = pl.BlockSpec((tm, d), lambda i: (i, 0))
    const = lambda shape: pl.BlockSpec(shape, lambda i: (0, 0), pipeline_mode=pl.Buffered(1))
    return pl.pallas_call(
        functools.partial(_merge_mlp_kernel, ff_chunk=1024),
        grid=(m // tm,),
        in_specs=[
            tile, tile, tile, tile, tile,
            const((d, d)), const((d, d)), const((d, d)), const((1, d)),
            const((d, d_ff)), const((d_ff, d)),
        ],
        out_specs=tile,
        out_shape=jax.ShapeDtypeStruct((m, d), F32),
        compiler_params=pltpu.CompilerParams(
            dimension_semantics=("arbitrary",),
            vmem_limit_bytes=VMEM_LIMIT_BYTES),
        name="merge_mlp",
    )(oa, ob, gate_a, gate_b, x2d, pa, pb, wo, ng, wu, wd)


def _lane_row(values, offset):
    row = jnp.zeros((1, LANES), F32)
    return row.at[0, offset:offset + values.shape[0]].set(values.astype(F32))


def kernel(x, norm_mix_g, w_in, gdn_conv_w, gdn_a_log, gdn_dt_bias, gdn_norm_g, fox_q_norm_g,
           fox_k_norm_g, fox_f_bias, w_proj_gdn, w_proj_fox, w_out, norm_mlp_g, w_up, w_down):
    b, t, d = x.shape
    depth = w_in.shape[0]
    qk_w = GDN_HEADS * GDN_D
    fox_w = FOX_HEADS * FOX_D
    o_gz = 3 * qk_w
    o_ga = 4 * qk_w
    o_gb = o_ga + GDN_HEADS
    o_fq = o_gb + GDN_HEADS
    o_fv = o_fq + 2 * fox_w
    o_ff = o_fq + 3 * fox_w
    o_gate = o_ff + FOX_HEADS
    tm = 512

    for l in range(depth):
        w = w_in[l]
        w_small = jnp.concatenate(
            [w[:, o_ga:o_fq], w[:, o_ff:o_gate],
             jnp.zeros((d, LANES - 2 * GDN_HEADS - FOX_HEADS), F32)], axis=1)
        ws_hi = w_small.astype(BF16)
        ws_lo = (w_small - ws_hi.astype(F32)).astype(BF16)

        x2d = x.reshape(b * t, d)
        gain = norm_mix_g[l][None, :]
        w_main = jnp.concatenate([w[:, :o_ga], w[:, o_fq:o_ff], w[:, o_gate:]], axis=1).astype(BF16)
        aux = jnp.zeros((GDN_CONV, w_main.shape[1]), F32)
        aux = aux.at[:, :o_gz].set(gdn_conv_w[l])
        aux = aux.at[0, o_ga:o_ga + fox_w].set(
            jnp.tile(fox_q_norm_g[l] * (LOG2_E * FOX_D ** -0.5), FOX_HEADS))
        aux = aux.at[0, o_ga + fox_w:o_ga + 2 * fox_w].set(jnp.tile(fox_k_norm_g[l], FOX_HEADS))
        gq, gk, gv, gz, fq, fk, fv, gate_a, gate_b = _in_projection(
            x2d, gain, w_main, aux,
            modes=("conv_l2_scaled", "conv_l2", "conv", "silu", "rms", "rms", "copy", "sigmoid", "sigmoid"),
            head_major=(True, True, True, True, False, False, False, False, False),
            tm=tm, seq_len=t, name="in_projection")

        bias_row = _lane_row(gdn_dt_bias[l], SMALL_GA) + _lane_row(fox_f_bias[l], SMALL_FF)
        small3, small_t = _gates(x, gain, jnp.concatenate([ws_hi, ws_lo], axis=1),
                                 _lane_row(gdn_a_log[l], SMALL_GA), bias_row)
        seq = lambda a: a.reshape(b, t, -1)
        o_a = _gdn_branch(gq, gk, gv, gz, small3, small_t, gdn_norm_g[l][None, :], heads=4)
        o_b = _fox_branch(seq(fq), seq(fk), seq(fv), small_t, heads=2)

        out = _merge_mlp(
            o_a.reshape(b * t, qk_w), o_b.reshape(b * t, fox_w), gate_a, gate_b, x2d,
            w_proj_gdn[l].astype(BF16), w_proj_fox[l].astype(BF16), w_out[l].astype(BF16),
            norm_mlp_g[l][None, :], w_up[l].astype(BF16), w_down[l].astype(BF16), tm=tm)
        x = out.reshape(b, t, d)
    return x
```

```python
import functools

import jax
import jax.numpy as jnp
from jax import lax
from jax.experimental import pallas as pl
from jax.experimental.pallas import tpu as pltpu

F32 = jnp.float32
BF16 = jnp.bfloat16

LANES = 128
SUBLANES = 8
VMEM_LIMIT_BYTES = 56 * 1024 * 1024

EPS = 1e-6
GDN_HEADS = 8
GDN_D = 128
GDN_CONV = 4
GDN_CHUNK = 64
GDN_INV_BLOCK = 16
FOX_HEADS = 8
FOX_D = 128
FOX_TQ = 256
SMALL_GA, SMALL_GB, SMALL_FF = 0, 8, 16
NEG_BIG = -1e30
LOG2_E = 1.4426950408889634


def _dot(a, b):
    return jnp.dot(a, b, preferred_element_type=F32)


def _dot_nt(a, b):
    return lax.dot_general(a, b, (((1,), (1,)), ((), ())), preferred_element_type=F32)


def _sigmoid(x):
    return 1.0 / (1.0 + jnp.exp(-x))


def _softplus(x):
    return jnp.maximum(x, 0.0) + jnp.log1p(jnp.exp(-jnp.abs(x)))


def _rmsnorm(x, g):
    return x * lax.rsqrt(jnp.mean(x * x, axis=-1, keepdims=True) + EPS) * g


def _pick_lane(x, lane):
    ids = lax.broadcasted_iota(jnp.int32, x.shape, 1)
    col = jnp.sum(jnp.where(ids == lane, x, 0.0), axis=-1, keepdims=True)
    return jnp.broadcast_to(col, x.shape)


def _segment_cumsum(x, seg):
    pos = lax.broadcasted_iota(jnp.int32, x.shape, 0) % seg
    s = 1
    while s < seg:
        x = x + jnp.where(pos >= s, pltpu.roll(x, s, axis=0), 0.0)
        s *= 2
    return x


def _head_normalise(a, gain, *, mean, scale=1.0):
    heads = []
    for c0 in range(0, a.shape[1], LANES):
        a_h = a[:, c0:c0 + LANES]
        ss = jnp.sum(a_h * a_h, axis=-1, keepdims=True)
        if mean:
            ss = ss * (1.0 / LANES)
        inv = lax.rsqrt(ss + EPS)
        if scale != 1.0:
            inv = inv * scale
        heads.append(a_h * inv if gain is None else a_h * inv * gain[:, c0:c0 + LANES])
    return jnp.concatenate(heads, axis=1)


def _causal_conv_silu(y, tail, w):
    tm, cw = y.shape
    groups = y.reshape(tm // SUBLANES, SUBLANES, cw)
    row_in_group = lax.broadcasted_iota(jnp.int32, (1, SUBLANES, cw), 1)
    acc = None
    for i in range(GDN_CONV):
        s = GDN_CONV - 1 - i
        if s == 0:
            shifted = groups
        else:
            rotated = pltpu.roll(groups, s, axis=1)
            previous = jnp.concatenate([pltpu.roll(tail, s, axis=0)[None], rotated[:-1]], axis=0)
            shifted = jnp.where(row_in_group < s, previous, rotated)
        term = shifted * w[i:i + 1, :][None]
        acc = term if acc is None else acc + term
    acc = acc.reshape(tm, cw)
    return acc * _sigmoid(acc)


_CONV_MODES = ("conv", "conv_l2", "conv_l2_scaled")
_AUX_MODES = _CONV_MODES + ("rms",)


def _proj_kernel(x_ref, g_ref, w_ref, *refs, modes, head_major, width, chunk, tiles_per_seq):
    refs = list(refs)
    aux_ref = refs.pop(0) if any(mode in _AUX_MODES for mode in modes) else None
    outs = refs[:len(modes)]
    tail_s = refs[len(modes)] if len(refs) > len(modes) else None
    tm = x_ref.shape[0]
    u = _rmsnorm(x_ref[...], g_ref[...]).astype(BF16)
    sequence_start = (pl.program_id(0) % tiles_per_seq) == 0
    items = [(gi, c0) for gi in range(len(modes)) for c0 in range(0, width, chunk)]
    heavy = [item for item in items if modes[item[0]] in _CONV_MODES]
    light = [item for item in items if modes[item[0]] not in _CONV_MODES]
    per_heavy = len(light) // len(heavy) if heavy else 0
    order = []
    for item in heavy:
        order.append(item)
        order.extend(light[:per_heavy])
        light = light[per_heavy:]
    order.extend(light)
    for gi, c0 in order:
        mode, out, by_head = modes[gi], outs[gi], head_major[gi]
        cols = slice(gi * width + c0, gi * width + c0 + chunk)
        y = _dot(u, w_ref[:, cols])
        if mode in _CONV_MODES:
            tail = jnp.where(sequence_start, 0.0, tail_s[:, cols])
            tail_s[:, cols] = y[tm - SUBLANES:, :]
            y = _causal_conv_silu(y, tail, aux_ref[0:GDN_CONV, cols])
            if mode != "conv":
                scale = GDN_D ** -0.5 if mode == "conv_l2_scaled" else 1.0
                y = _head_normalise(y, None, mean=False, scale=scale)
        elif mode == "rms":
            y = _head_normalise(y, aux_ref[0:1, cols], mean=True)
        elif mode == "silu":
            y = y * _sigmoid(y)
        elif mode == "sigmoid":
            y = _sigmoid(y)
        else:
            assert mode == "copy", mode
        y = y.astype(out.dtype)
        if by_head:
            for j in range(chunk // LANES):
                out[0, c0 // LANES + j] = y[:, j * LANES:(j + 1) * LANES]
        else:
            out[:, c0:c0 + chunk] = y


def _in_projection(x2d, gain, w, aux, *, modes, head_major, tm, seq_len, name):
    m, d = x2d.shape
    n = w.shape[1]
    width = n // len(modes)
    heads = width // LANES
    tiles_per_seq = seq_len // tm
    assert (aux is not None) == any(mode in _AUX_MODES for mode in modes)
    needs_tail = any(mode in _CONV_MODES for mode in modes)
    const = lambda shape: pl.BlockSpec(shape, lambda i: (0, 0), pipeline_mode=pl.Buffered(1))
    operands = [x2d, gain, w] + ([aux] if aux is not None else [])
    flat_spec = pl.BlockSpec((tm, width), lambda i: (i, 0))
    flat_shape = jax.ShapeDtypeStruct((m, width), BF16)
    head_spec = pl.BlockSpec((1, heads, tm, LANES), lambda i: (i // tiles_per_seq, 0, i % tiles_per_seq, 0))
    head_shape = jax.ShapeDtypeStruct((m // seq_len, heads, seq_len, LANES), BF16)
    return pl.pallas_call(
        functools.partial(_proj_kernel, modes=modes, head_major=head_major, width=width, chunk=512,
                          tiles_per_seq=tiles_per_seq),
        grid=(m // tm,),
        in_specs=[pl.BlockSpec((tm, d), lambda i: (i, 0))] + [const(a.shape) for a in operands[1:]],
        out_specs=[head_spec if by_head else flat_spec for by_head in head_major],
        out_shape=[head_shape if by_head else flat_shape for by_head in head_major],
        scratch_shapes=[pltpu.VMEM((SUBLANES, n), F32)] if needs_tail else [],
        compiler_params=pltpu.CompilerParams(
            dimension_semantics=("arbitrary",),
            vmem_limit_bytes=VMEM_LIMIT_BYTES),
        name=name,
    )(*operands)


def _gate_kernel(x_ref, g_ref, ws_ref, alog_ref, bias_ref, out_ref, out_t_ref, loc_s):
    t = x_ref.shape[0]
    n_chunks = t // GDN_CHUNK
    u = _rmsnorm(x_ref[...], g_ref[...])
    u_hi = u.astype(BF16)
    u_lo = (u - u_hi.astype(F32)).astype(BF16)
    hi_terms = _dot(u_hi, ws_ref[...])
    logits = hi_terms[:, :LANES] + hi_terms[:, LANES:] + _dot(u_lo, ws_ref[:, :LANES])
    shifted = logits + bias_ref[...]
    log_decay = -jnp.exp(alog_ref[...]) * _softplus(shifted)
    beta = _sigmoid(logits)
    log2_forget = -_softplus(-shifted) * LOG2_E
    lane = lax.broadcasted_iota(jnp.int32, logits.shape, 1)
    vals = jnp.where(lane < SMALL_GB, log_decay, jnp.where(lane < SMALL_FF, beta, log2_forget))
    local = _segment_cumsum(vals, GDN_CHUNK)
    loc_s[...] = local
    totals = loc_s[pl.ds(GDN_CHUNK - 1, n_chunks, stride=GDN_CHUNK), :]
    carried = _segment_cumsum(totals, n_chunks) - totals
    full = local + jnp.broadcast_to(carried[:, None, :], (n_chunks, GDN_CHUNK, LANES)).reshape(t, LANES)
    out = jnp.where(lane < SMALL_GB, local, jnp.where(lane < SMALL_FF, vals, full))
    out_ref[0] = out
    out_t_ref[0] = jnp.transpose(out)


def _gates(x3, gain, ws_hi_lo, alog_row, bias_row):
    b, t, d = x3.shape
    const = lambda shape: pl.BlockSpec(shape, lambda i: (0, 0))
    return pl.pallas_call(
        _gate_kernel,
        grid=(b,),
        in_specs=[
            pl.BlockSpec((None, t, d), lambda i: (i, 0, 0)),
            const((1, d)), const((d, 2 * LANES)), const((1, LANES)), const((1, LANES)),
        ],
        out_specs=[
            pl.BlockSpec((1, t, LANES), lambda i: (i, 0, 0)),
            pl.BlockSpec((1, LANES, t), lambda i: (i, 0, 0)),
        ],
        out_shape=[
            jax.ShapeDtypeStruct((b, t, LANES), F32),
            jax.ShapeDtypeStruct((b, LANES, t), F32),
        ],
        scratch_shapes=[pltpu.VMEM((t, LANES), F32)],
        compiler_params=pltpu.CompilerParams(
            dimension_semantics=("arbitrary",),
            vmem_limit_bytes=VMEM_LIMIT_BYTES),
        name="gates",
    )(x3, gain, ws_hi_lo, alog_row, bias_row)


def _unit_lower_inverse(a2, eye_hi, same_block2, low):
    c = a2.shape[0]
    lo = lambda slab: slab[:, :c].astype(BF16)
    pick = lambda low_part, high_part: jnp.where(low, low_part, high_part).astype(BF16)
    e2 = jnp.where(same_block2, -a2, 0.0)
    off2 = a2 + e2
    y = eye_hi + e2
    r = _dot(lo(e2), e2.astype(BF16))
    yield
    r = _dot(lo(r), pick(r, y))
    yield
    y = y + r
    r = _dot(lo(r), pick(r, y))
    yield
    y = y + r
    t = y + _dot(lo(r), y.astype(BF16))
    yield
    t_low = pltpu.roll(t, c, axis=1)
    m2 = _dot(lo(t_low), off2.astype(BF16))
    yield
    r = _dot(lo(m2), pick(m2, t))
    yield
    v = t - r
    inv = v + _dot(lo(r), v.astype(BF16))
    yield
    return pltpu.roll(inv, c, axis=1)[:, :c]


def _run_in_lockstep(generators):
    results = [None] * len(generators)
    live = list(enumerate(generators))
    while live:
        still_live = []
        for idx, gen in live:
            try:
                next(gen)
                still_live.append((idx, gen))
            except StopIteration as stop:
                results[idx] = stop.value
        live = still_live
    return results


def _gdn_kernel(q_ref, k_ref, v_ref, z_ref, sm_ref, gt_ref, ng_ref, o_ref,
                gcum_s, beta_s, rp_s, qm_s, u_s,
                *, heads, unroll_b, lockstep_chunks):
    h0 = pl.program_id(1) * heads
    t = q_ref.shape[2]
    c = GDN_CHUNK
    d = GDN_D
    n_chunks = t // c

    ri = lax.broadcasted_iota(jnp.int32, (c, 2 * c), 0)
    lane = lax.broadcasted_iota(jnp.int32, (c, 2 * c), 1)
    low = lane < c
    ci = jnp.where(low, lane, lane - c)
    tri_incl = ri >= ci
    tri_strict = ri > ci
    same_block = (ri // GDN_INV_BLOCK) == (ci // GDN_INV_BLOCK)
    eye_hi = jnp.where(lane == ri + c, 1.0, 0.0).astype(F32)

    def chunk_prepare(q_bf, k_bf, v_bf, gb, bb, g_row2):
        diff = gb - g_row2
        decay = jnp.where(tri_incl, jnp.exp(jnp.where(tri_incl, diff, 0.0)), 0.0)
        q = q_bf.astype(F32)
        k = k_bf.astype(F32)
        v = v_bf.astype(F32)
        qk_kk = _dot_nt(jnp.concatenate([q_bf, k_bf], axis=0),
                        jnp.concatenate([k_bf, k_bf], axis=0))
        yield
        qk = qk_kk[:c, :]
        kk = qk_kk[c:, :]
        a_mat = jnp.where(tri_strict, bb * kk * decay, 0.0)
        t_inv = yield from _unit_lower_inverse(a_mat, eye_hi, same_block, low)
        e_g = jnp.exp(gb)
        rhs = jnp.concatenate([v * bb, k * (bb * e_g)], axis=1).astype(BF16)
        sol = _dot(t_inv.astype(BF16), rhs).astype(BF16)
        yield
        g_last = gb[c - 1:c, :]
        k_dec = k * jnp.exp(g_last - gb)
        attn = (qk * decay)[:, :c]
        lhs = jnp.concatenate([jnp.transpose(k_dec), attn], axis=0).astype(BF16)
        fused = _dot(lhs, sol)
        q_mat = fused[:d, :d]
        p_mat = fused[:d, d:]
        u_mat = fused[d:, :d]
        r_mat = q * e_g - fused[d:, d:]
        return p_mat.astype(BF16), q_mat, r_mat.astype(BF16), u_mat

    def prepare_head(hh, carry):
        sm = sm_ref[0]
        gcum_s[hh] = _pick_lane(sm, SMALL_GA + h0 + hh)
        beta_s[hh] = _pick_lane(sm, SMALL_GB + h0 + hh)
        g_rows = gt_ref[0, pl.ds(h0 + hh, 1), :]
        spans = [slice(ch * c, (ch + 1) * c) for ch in range(n_chunks)]
        g_row_slabs = []
        for pair in range(n_chunks // 2):
            window = g_rows[:, pair * 2 * c:(pair + 1) * 2 * c]
            swapped = pltpu.roll(window, c, axis=1)
            g_row_slabs += [jnp.where(low[:1], window, swapped), jnp.where(low[:1], swapped, window)]
        for first in range(0, n_chunks, lockstep_chunks):
            group = range(first, first + lockstep_chunks)
            loaded = [(q_ref[0, hh, spans[ch], :], k_ref[0, hh, spans[ch], :], v_ref[0, hh, spans[ch], :],
                       gcum_s[hh, spans[ch], :], beta_s[hh, spans[ch], :], g_row_slabs[ch]) for ch in group]
            results = _run_in_lockstep([chunk_prepare(*args) for args in loaded])
            for ch, (p_mat, q_mat, r_mat, u_mat) in zip(group, results):
                rp_s[hh, ch] = jnp.concatenate([p_mat, r_mat], axis=0)
                qm_s[hh, ch] = q_mat.astype(qm_s.dtype)
                u_s[hh, spans[ch], :] = u_mat.astype(u_s.dtype)
        return carry

    lax.fori_loop(0, heads, prepare_head, 0)

    ng = ng_ref[...]

    def scan_body(i, states):
        states = list(states)
        for u in range(unroll_b):
            ch = i * unroll_b + u
            r0 = pl.multiple_of(ch * c, c)
            rows = pl.ds(r0, c)
            products = []
            for hh in range(heads):
                both = _dot(rp_s[hh, ch], states[hh].astype(BF16))
                products.append((both[d:, :] + u_s[hh, rows, :], both[:d, :]))
            for hh in range(heads):
                lanes = slice(hh * d, (hh + 1) * d)
                o, ps = products[hh]
                dec = jnp.exp(gcum_s[hh, pl.ds(r0 + c - 1, 1), :])
                states[hh] = states[hh] * dec - ps + qm_s[hh, ch]
                z_act = z_ref[0, hh, rows, :].astype(F32)
                o_ref[0, rows, lanes] = (_rmsnorm(o, ng) * z_act).astype(o_ref.dtype)
        return tuple(states)

    lax.fori_loop(0, n_chunks // unroll_b, scan_body,
                  tuple(jnp.zeros((d, d), F32) for _ in range(heads)))


def _gdn_branch(q4, k4, v4, z4, small3, small_t, norm_g, *, heads):
    b, _, t, d = q4.shape
    assert d == LANES == 2 * GDN_CHUNK, "chunk-local matrices are kept as [m | m] slabs one vreg wide"
    n_chunks = t // GDN_CHUNK
    head_spec = pl.BlockSpec((1, heads, t, d), lambda bi, hi: (bi, hi, 0, 0))
    row_spec = pl.BlockSpec((1, LANES), lambda bi, hi: (0, 0))
    return pl.pallas_call(
        functools.partial(_gdn_kernel, heads=heads, unroll_b=16, lockstep_chunks=n_chunks),
        grid=(b, GDN_HEADS // heads),
        in_specs=[
            head_spec, head_spec, head_spec, head_spec,
            pl.BlockSpec((1, t, LANES), lambda bi, hi: (bi, 0, 0)),
            pl.BlockSpec((1, GDN_HEADS, t), lambda bi, hi: (bi, SMALL_GA // GDN_HEADS, 0)),
            row_spec,
        ],
        out_specs=pl.BlockSpec((1, t, heads * d), lambda bi, hi: (bi, 0, hi)),
        out_shape=jax.ShapeDtypeStruct((b, t, GDN_HEADS * d), BF16),
        scratch_shapes=[
            pltpu.VMEM((heads, t, LANES), F32),
            pltpu.VMEM((heads, t, LANES), F32),
            pltpu.VMEM((heads, n_chunks, d + GDN_CHUNK, d), BF16),
            pltpu.VMEM((heads, n_chunks, d, d), BF16),
            pltpu.VMEM((heads, t, d), BF16),
        ],
        compiler_params=pltpu.CompilerParams(
            dimension_semantics=("arbitrary", "arbitrary"),
            vmem_limit_bytes=VMEM_LIMIT_BYTES),
        name="gdn_branch",
    )(q4, k4, v4, z4, small3, small_t, norm_g)


def _fox_kernel(q_ref, k_ref, v_ref, ct_ref, o_ref, v1_s, *, heads):
    h0 = pl.program_id(1) * heads
    t = q_ref.shape[1]
    d = FOX_D
    tq = FOX_TQ
    nq = t // tq
    c_rows = [ct_ref[0, pl.ds(h0 + hh, 1), :] for hh in range(heads)]
    for hh in range(heads):
        v1_s[hh, :, :d] = v_ref[0, :, hh * d:(hh + 1) * d]
        v1_s[hh, :, d:] = jnp.ones((t, d), BF16)

    ri = lax.broadcasted_iota(jnp.int32, (tq, tq), 0)
    ci = lax.broadcasted_iota(jnp.int32, (tq, tq), 1)
    causal = ri >= ci

    def scores(hh, qi):
        lanes = slice(hh * d, (hh + 1) * d)
        return _dot_nt(q_ref[0, qi * tq:(qi + 1) * tq, lanes], k_ref[0, :(qi + 1) * tq, lanes])

    def finish(hh, qi, s):
        lanes = slice(hh * d, (hh + 1) * d)
        n_keys = (qi + 1) * tq
        s = s - c_rows[hh][:, :n_keys]
        diag = jnp.where(causal, s[:, n_keys - tq:], NEG_BIG)
        s = diag if qi == 0 else jnp.concatenate([s[:, :n_keys - tq], diag], axis=1)
        p = jnp.exp2(s - jnp.max(s, axis=-1, keepdims=True))
        pv = _dot(p.astype(BF16), v1_s[hh, :n_keys, :])
        o_ref[0, qi * tq:(qi + 1) * tq, lanes] = (pv[:, :d] / pv[:, d:]).astype(o_ref.dtype)

    work = [(hh, qi) for qi in range(nq) for hh in range(heads)]
    s_next = scores(*work[0])
    for idx, item in enumerate(work):
        s_cur = s_next
        if idx + 1 < len(work):
            s_next = scores(*work[idx + 1])
        finish(*item, s_cur)


def _fox_branch(q3, k3, v3, small_t, *, heads):
    b, t, _ = q3.shape
    seq_spec = pl.BlockSpec((1, t, heads * FOX_D), lambda bi, hi: (bi, 0, hi))
    return pl.pallas_call(
        functools.partial(_fox_kernel, heads=heads),
        grid=(b, FOX_HEADS // heads),
        in_specs=[
            seq_spec, seq_spec, seq_spec,
            pl.BlockSpec((1, FOX_HEADS, t), lambda bi, hi: (bi, SMALL_FF // FOX_HEADS, 0)),
        ],
        out_specs=seq_spec,
        out_shape=jax.ShapeDtypeStruct((b, t, FOX_HEADS * FOX_D), BF16),
        scratch_shapes=[pltpu.VMEM((heads, t, 2 * FOX_D), BF16)],
        compiler_params=pltpu.CompilerParams(
            dimension_semantics=("arbitrary", "arbitrary"),
            vmem_limit_bytes=VMEM_LIMIT_BYTES),
        name="fox_branch",
    )(q3, k3, v3, small_t)


def _merge_mlp_kernel(oa_ref, ob_ref, ga_ref, gb_ref, x_ref, pa_ref, pb_ref, wo_ref, ng_ref,
                      wu_ref, wd_ref, out_ref, *, ff_chunk):
    ya = _dot(oa_ref[...], pa_ref[...])
    yb = _dot(ob_ref[...], pb_ref[...])
    merged = ga_ref[...].astype(F32) * ya + gb_ref[...].astype(F32) * yb
    hid = x_ref[...] + _dot(merged.astype(BF16), wo_ref[...])
    hn = _rmsnorm(hid, ng_ref[...]).astype(BF16)
    acc = hid
    d_ff = wu_ref.shape[1]
    for c0 in range(0, d_ff, ff_chunk):
        up = _dot(hn, wu_ref[:, c0:c0 + ff_chunk])
        act = jnp.square(jnp.maximum(up, 0.0)).astype(BF16)
        acc = acc + _dot(act, wd_ref[c0:c0 + ff_chunk, :])
    out_ref[...] = acc


def _merge_mlp(oa, ob, gate_a, gate_b, x2d, pa, pb, wo, ng, wu, wd, *, tm):
    m, d = x2d.shape
    d_ff = wu.shape[1]
    tile = pl.BlockSpec((tm, d), lambda i: (i, 0))
    const = lambda shape: pl.BlockSpec(shape, lambda i: (0, 0), pipeline_mode=pl.Buffered(1))
    return pl.pallas_call(
        functools.partial(_merge_mlp_kernel, ff_chunk=1024),
        grid=(m // tm,),
        in_specs=[
            tile, tile, tile, tile, tile,
            const((d, d)), const((d, d)), const((d, d)), const((1, d)),
            const((d, d_ff)), const((d_ff, d)),
        ],
        out_specs=tile,
        out_shape=jax.ShapeDtypeStruct((m, d), F32),
        compiler_params=pltpu.CompilerParams(
            dimension_semantics=("arbitrary",),
            vmem_limit_bytes=VMEM_LIMIT_BYTES),
        name="merge_mlp",
    )(oa, ob, gate_a, gate_b, x2d, pa, pb, wo, ng, wu, wd)


def _lane_row(values, offset):
    row = jnp.zeros((1, LANES), F32)
    return row.at[0, offset:offset + values.shape[0]].set(values.astype(F32))


def kernel(x, norm_mix_g, w_in, gdn_conv_w, gdn_a_log, gdn_dt_bias, gdn_norm_g, fox_q_norm_g,
           fox_k_norm_g, fox_f_bias, w_proj_gdn, w_proj_fox, w_out, norm_mlp_g, w_up, w_down):
    b, t, d = x.shape
    depth = w_in.shape[0]
    qk_w = GDN_HEADS * GDN_D
    fox_w = FOX_HEADS * FOX_D
    o_gz = 3 * qk_w
    o_ga = 4 * qk_w
    o_gb = o_ga + GDN_HEADS
    o_fq = o_gb + GDN_HEADS
    o_fv = o_fq + 2 * fox_w
    o_ff = o_fq + 3 * fox_w
    o_gate = o_ff + FOX_HEADS
    tm = 512

    for l in range(depth):
        w = w_in[l]
        w_small = jnp.concatenate(
            [w[:, o_ga:o_fq], w[:, o_ff:o_gate],
             jnp.zeros((d, LANES - 2 * GDN_HEADS - FOX_HEADS), F32)], axis=1)
        ws_hi = w_small.astype(BF16)
        ws_lo = (w_small - ws_hi.astype(F32)).astype(BF16)

        x2d = x.reshape(b * t, d)
        gain = norm_mix_g[l][None, :]
        w_main = jnp.concatenate([w[:, :o_ga], w[:, o_fq:o_ff], w[:, o_gate:]], axis=1).astype(BF16)
        aux = jnp.zeros((GDN_CONV, w_main.shape[1]), F32)
        aux = aux.at[:, :o_gz].set(gdn_conv_w[l])
        aux = aux.at[0, o_ga:o_ga + fox_w].set(
            jnp.tile(fox_q_norm_g[l] * (LOG2_E * FOX_D ** -0.5), FOX_HEADS))
        aux = aux.at[0, o_ga + fox_w:o_ga + 2 * fox_w].set(jnp.tile(fox_k_norm_g[l], FOX_HEADS))
        gq, gk, gv, gz, fq, fk, fv, gate_a, gate_b = _in_projection(
            x2d, gain, w_main, aux,
            modes=("conv_l2_scaled", "conv_l2", "conv", "silu", "rms", "rms", "copy", "sigmoid", "sigmoid"),
            head_major=(True, True, True, True, False, False, False, False, False),
            tm=tm, seq_len=t, name="in_projection")

        bias_row = _lane_row(gdn_dt_bias[l], SMALL_GA) + _lane_row(fox_f_bias[l], SMALL_FF)
        small3, small_t = _gates(x, gain, jnp.concatenate([ws_hi, ws_lo], axis=1),
                                 _lane_row(gdn_a_log[l], SMALL_GA), bias_row)
        seq = lambda a: a.reshape(b, t, -1)
        o_a = _gdn_branch(gq, gk, gv, gz, small3, small_t, gdn_norm_g[l][None, :], heads=4)
        o_b = _fox_branch(seq(fq), seq(fk), seq(fv), small_t, heads=4)

        out = _merge_mlp(
            o_a.reshape(b * t, qk_w), o_b.reshape(b * t, fox_w), gate_a, gate_b, x2d,
            w_proj_gdn[l].astype(BF16), w_proj_fox[l].astype(BF16), w_out[l].astype(BF16),
            norm_mlp_g[l][None, :], w_up[l].astype(BF16), w_down[l].astype(BF16), tm=tm)
        x = out.reshape(b, t, d)
    return x
```

```python
import functools

import jax
import jax.numpy as jnp
from jax import lax
from jax.experimental import pallas as pl
from jax.experimental.pallas import tpu as pltpu

F32 = jnp.float32
BF16 = jnp.bfloat16

LANES = 128
SUBLANES = 8
VMEM_LIMIT_BYTES = 56 * 1024 * 1024

EPS = 1e-6
GDN_HEADS = 8
GDN_D = 128
GDN_CONV = 4
GDN_CHUNK = 64
GDN_INV_BLOCK = 16
FOX_HEADS = 8
FOX_D = 128
FOX_TQ = 256
SMALL_GA, SMALL_GB, SMALL_FF = 0, 8, 16
NEG_BIG = -1e30
LOG2_E = 1.4426950408889634


def _dot(a, b):
    return jnp.dot(a, b, preferred_element_type=F32)


def _dot_nt(a, b):
    return lax.dot_general(a, b, (((1,), (1,)), ((), ())), preferred_element_type=F32)


def _sigmoid(x):
    return 1.0 / (1.0 + jnp.exp(-x))


def _softplus(x):
    return jnp.maximum(x, 0.0) + jnp.log1p(jnp.exp(-jnp.abs(x)))


def _rmsnorm(x, g):
    return x * lax.rsqrt(jnp.mean(x * x, axis=-1, keepdims=True) + EPS) * g


def _pick_lane(x, lane):
    ids = lax.broadcasted_iota(jnp.int32, x.shape, 1)
    col = jnp.sum(jnp.where(ids == lane, x, 0.0), axis=-1, keepdims=True)
    return jnp.broadcast_to(col, x.shape)


def _segment_cumsum(x, seg):
    pos = lax.broadcasted_iota(jnp.int32, x.shape, 0) % seg
    s = 1
    while s < seg:
        x = x + jnp.where(pos >= s, pltpu.roll(x, s, axis=0), 0.0)
        s *= 2
    return x


def _head_normalise(a, gain, *, mean, scale=1.0):
    heads = []
    for c0 in range(0, a.shape[1], LANES):
        a_h = a[:, c0:c0 + LANES]
        ss = jnp.sum(a_h * a_h, axis=-1, keepdims=True)
        if mean:
            ss = ss * (1.0 / LANES)
        inv = lax.rsqrt(ss + EPS)
        if scale != 1.0:
            inv = inv * scale
        heads.append(a_h * inv if gain is None else a_h * inv * gain[:, c0:c0 + LANES])
    return jnp.concatenate(heads, axis=1)


def _causal_conv_silu(y, tail, w):
    tm, cw = y.shape
    groups = y.reshape(tm // SUBLANES, SUBLANES, cw)
    row_in_group = lax.broadcasted_iota(jnp.int32, (1, SUBLANES, cw), 1)
    acc = None
    for i in range(GDN_CONV):
        s = GDN_CONV - 1 - i
        if s == 0:
            shifted = groups
        else:
            rotated = pltpu.roll(groups, s, axis=1)
            previous = jnp.concatenate([pltpu.roll(tail, s, axis=0)[None], rotated[:-1]], axis=0)
            shifted = jnp.where(row_in_group < s, previous, rotated)
        term = shifted * w[i:i + 1, :][None]
        acc = term if acc is None else acc + term
    acc = acc.reshape(tm, cw)
    return acc * _sigmoid(acc)


_CONV_MODES = ("conv", "conv_l2", "conv_l2_scaled")
_AUX_MODES = _CONV_MODES + ("rms",)


def _proj_kernel(x_ref, g_ref, w_ref, *refs, modes, head_major, width, chunk, tiles_per_seq):
    refs = list(refs)
    aux_ref = refs.pop(0) if any(mode in _AUX_MODES for mode in modes) else None
    outs = refs[:len(modes)]
    tail_s = refs[len(modes)] if len(refs) > len(modes) else None
    tm = x_ref.shape[0]
    u = _rmsnorm(x_ref[...], g_ref[...]).astype(BF16)
    sequence_start = (pl.program_id(0) % tiles_per_seq) == 0
    items = [(gi, c0) for gi in range(len(modes)) for c0 in range(0, width, chunk)]
    heavy = [item for item in items if modes[item[0]] in _CONV_MODES]
    light = [item for item in items if modes[item[0]] not in _CONV_MODES]
    per_heavy = len(light) // len(heavy) if heavy else 0
    order = []
    for item in heavy:
        order.append(item)
        order.extend(light[:per_heavy])
        light = light[per_heavy:]
    order.extend(light)
    for gi, c0 in order:
        mode, out, by_head = modes[gi], outs[gi], head_major[gi]
        cols = slice(gi * width + c0, gi * width + c0 + chunk)
        y = _dot(u, w_ref[:, cols])
        if mode in _CONV_MODES:
            tail = jnp.where(sequence_start, 0.0, tail_s[:, cols])
            tail_s[:, cols] = y[tm - SUBLANES:, :]
            y = _causal_conv_silu(y, tail, aux_ref[0:GDN_CONV, cols])
            if mode != "conv":
                scale = GDN_D ** -0.5 if mode == "conv_l2_scaled" else 1.0
                y = _head_normalise(y, None, mean=False, scale=scale)
        elif mode == "rms":
            y = _head_normalise(y, aux_ref[0:1, cols], mean=True)
        elif mode == "silu":
            y = y * _sigmoid(y)
        elif mode == "sigmoid":
            y = _sigmoid(y)
        else:
            assert mode == "copy", mode
        y = y.astype(out.dtype)
        if by_head:
            for j in range(chunk // LANES):
                out[0, c0 // LANES + j] = y[:, j * LANES:(j + 1) * LANES]
        else:
            out[:, c0:c0 + chunk] = y


def _in_projection(x2d, gain, w, aux, *, modes, head_major, tm, seq_len, name):
    m, d = x2d.shape
    n = w.shape[1]
    width = n // len(modes)
    heads = width // LANES
    tiles_per_seq = seq_len // tm
    assert (aux is not None) == any(mode in _AUX_MODES for mode in modes)
    needs_tail = any(mode in _CONV_MODES for mode in modes)
    const = lambda shape: pl.BlockSpec(shape, lambda i: (0, 0), pipeline_mode=pl.Buffered(1))
    operands = [x2d, gain, w] + ([aux] if aux is not None else [])
    flat_spec = pl.BlockSpec((tm, width), lambda i: (i, 0))
    flat_shape = jax.ShapeDtypeStruct((m, width), BF16)
    head_spec = pl.BlockSpec((1, heads, tm, LANES), lambda i: (i // tiles_per_seq, 0, i % tiles_per_seq, 0))
    head_shape = jax.ShapeDtypeStruct((m // seq_len, heads, seq_len, LANES), BF16)
    return pl.pallas_call(
        functools.partial(_proj_kernel, modes=modes, head_major=head_major, width=width, chunk=512,
                          tiles_per_seq=tiles_per_seq),
        grid=(m // tm,),
        in_specs=[pl.BlockSpec((tm, d), lambda i: (i, 0))] + [const(a.shape) for a in operands[1:]],
        out_specs=[head_spec if by_head else flat_spec for by_head in head_major],
        out_shape=[head_shape if by_head else flat_shape for by_head in head_major],
        scratch_shapes=[pltpu.VMEM((SUBLANES, n), F32)] if needs_tail else [],
        compiler_params=pltpu.CompilerParams(
            dimension_semantics=("arbitrary",),
            vmem_limit_bytes=VMEM_LIMIT_BYTES),
        name=name,
    )(*operands)


def _gate_kernel(x_ref, g_ref, ws_ref, alog_ref, bias_ref, out_ref, out_t_ref, loc_s):
    t = x_ref.shape[0]
    n_chunks = t // GDN_CHUNK
    u = _rmsnorm(x_ref[...], g_ref[...])
    u_hi = u.astype(BF16)
    u_lo = (u - u_hi.astype(F32)).astype(BF16)
    hi_terms = _dot(u_hi, ws_ref[...])
    logits = hi_terms[:, :LANES] + hi_terms[:, LANES:] + _dot(u_lo, ws_ref[:, :LANES])
    shifted = logits + bias_ref[...]
    log_decay = -jnp.exp(alog_ref[...]) * _softplus(shifted)
    beta = _sigmoid(logits)
    log2_forget = -_softplus(-shifted) * LOG2_E
    lane = lax.broadcasted_iota(jnp.int32, logits.shape, 1)
    vals = jnp.where(lane < SMALL_GB, log_decay, jnp.where(lane < SMALL_FF, beta, log2_forget))
    local = _segment_cumsum(vals, GDN_CHUNK)
    loc_s[...] = local
    totals = loc_s[pl.ds(GDN_CHUNK - 1, n_chunks, stride=GDN_CHUNK), :]
    carried = _segment_cumsum(totals, n_chunks) - totals
    full = local + jnp.broadcast_to(carried[:, None, :], (n_chunks, GDN_CHUNK, LANES)).reshape(t, LANES)
    out = jnp.where(lane < SMALL_GB, local, jnp.where(lane < SMALL_FF, vals, full))
    out_ref[0] = out
    out_t_ref[0] = jnp.transpose(out)


def _gates(x3, gain, ws_hi_lo, alog_row, bias_row):
    b, t, d = x3.shape
    const = lambda shape: pl.BlockSpec(shape, lambda i: (0, 0))
    return pl.pallas_call(
        _gate_kernel,
        grid=(b,),
        in_specs=[
            pl.BlockSpec((None, t, d), lambda i: (i, 0, 0)),
            const((1, d)), const((d, 2 * LANES)), const((1, LANES)), const((1, LANES)),
        ],
        out_specs=[
            pl.BlockSpec((1, t, LANES), lambda i: (i, 0, 0)),
            pl.BlockSpec((1, LANES, t), lambda i: (i, 0, 0)),
        ],
        out_shape=[
            jax.ShapeDtypeStruct((b, t, LANES), F32),
            jax.ShapeDtypeStruct((b, LANES, t), F32),
        ],
        scratch_shapes=[pltpu.VMEM((t, LANES), F32)],
        compiler_params=pltpu.CompilerParams(
            dimension_semantics=("arbitrary",),
            vmem_limit_bytes=VMEM_LIMIT_BYTES),
        name="gates",
    )(x3, gain, ws_hi_lo, alog_row, bias_row)


def _unit_lower_inverse(a2, eye_hi, same_block2, low):
    c = a2.shape[0]
    lo = lambda slab: slab[:, :c].astype(BF16)
    pick = lambda low_part, high_part: jnp.where(low, low_part, high_part).astype(BF16)
    e2 = jnp.where(same_block2, -a2, 0.0)
    off2 = a2 + e2
    y = eye_hi + e2
    r = _dot(lo(e2), e2.astype(BF16))
    yield
    r = _dot(lo(r), pick(r, y))
    yield
    y = y + r
    r = _dot(lo(r), pick(r, y))
    yield
    y = y + r
    t = y + _dot(lo(r), y.astype(BF16))
    yield
    t_low = pltpu.roll(t, c, axis=1)
    m2 = _dot(lo(t_low), off2.astype(BF16))
    yield
    r = _dot(lo(m2), pick(m2, t))
    yield
    v = t - r
    inv = v + _dot(lo(r), v.astype(BF16))
    yield
    return pltpu.roll(inv, c, axis=1)[:, :c]


def _run_in_lockstep(generators):
    results = [None] * len(generators)
    live = list(enumerate(generators))
    while live:
        still_live = []
        for idx, gen in live:
            try:
                next(gen)
                still_live.append((idx, gen))
            except StopIteration as stop:
                results[idx] = stop.value
        live = still_live
    return results


def _gdn_kernel(q_ref, k_ref, v_ref, z_ref, sm_ref, gt_ref, ng_ref, o_ref,
                gcum_s, beta_s, rp_s, qm_s, u_s,
                *, heads, unroll_b, lockstep_chunks):
    h0 = pl.program_id(1) * heads
    t = q_ref.shape[2]
    c = GDN_CHUNK
    d = GDN_D
    n_chunks = t // c

    ri = lax.broadcasted_iota(jnp.int32, (c, 2 * c), 0)
    lane = lax.broadcasted_iota(jnp.int32, (c, 2 * c), 1)
    low = lane < c
    ci = jnp.where(low, lane, lane - c)
    tri_incl = ri >= ci
    tri_strict = ri > ci
    same_block = (ri // GDN_INV_BLOCK) == (ci // GDN_INV_BLOCK)
    eye_hi = jnp.where(lane == ri + c, 1.0, 0.0).astype(F32)

    def chunk_prepare(q_bf, k_bf, v_bf, gb, bb, g_row2):
        diff = gb - g_row2
        decay = jnp.where(tri_incl, jnp.exp(jnp.where(tri_incl, diff, 0.0)), 0.0)
        q = q_bf.astype(F32)
        k = k_bf.astype(F32)
        v = v_bf.astype(F32)
        qk_kk = _dot_nt(jnp.concatenate([q_bf, k_bf], axis=0),
                        jnp.concatenate([k_bf, k_bf], axis=0))
        yield
        qk = qk_kk[:c, :]
        kk = qk_kk[c:, :]
        a_mat = jnp.where(tri_strict, bb * kk * decay, 0.0)
        t_inv = yield from _unit_lower_inverse(a_mat, eye_hi, same_block, low)
        e_g = jnp.exp(gb)
        rhs = jnp.concatenate([v * bb, k * (bb * e_g)], axis=1).astype(BF16)
        sol = _dot(t_inv.astype(BF16), rhs).astype(BF16)
        yield
        g_last = gb[c - 1:c, :]
        k_dec = k * jnp.exp(g_last - gb)
        attn = (qk * decay)[:, :c]
        lhs = jnp.concatenate([jnp.transpose(k_dec), attn], axis=0).astype(BF16)
        fused = _dot(lhs, sol)
        q_mat = fused[:d, :d]
        p_mat = fused[:d, d:]
        u_mat = fused[d:, :d]
        r_mat = q * e_g - fused[d:, d:]
        return p_mat.astype(BF16), q_mat, r_mat.astype(BF16), u_mat

    def prepare_head(hh, carry):
        sm = sm_ref[0]
        gcum_s[hh] = _pick_lane(sm, SMALL_GA + h0 + hh)
        beta_s[hh] = _pick_lane(sm, SMALL_GB + h0 + hh)
        g_rows = gt_ref[0, pl.ds(h0 + hh, 1), :]
        spans = [slice(ch * c, (ch + 1) * c) for ch in range(n_chunks)]
        g_row_slabs = []
        for pair in range(n_chunks // 2):
            window = g_rows[:, pair * 2 * c:(pair + 1) * 2 * c]
            swapped = pltpu.roll(window, c, axis=1)
            g_row_slabs += [jnp.where(low[:1], window, swapped), jnp.where(low[:1], swapped, window)]
        for first in range(0, n_chunks, lockstep_chunks):
            group = range(first, first + lockstep_chunks)
            loaded = [(q_ref[0, hh, spans[ch], :], k_ref[0, hh, spans[ch], :], v_ref[0, hh, spans[ch], :],
                       gcum_s[hh, spans[ch], :], beta_s[hh, spans[ch], :], g_row_slabs[ch]) for ch in group]
            results = _run_in_lockstep([chunk_prepare(*args) for args in loaded])
            for ch, (p_mat, q_mat, r_mat, u_mat) in zip(group, results):
                rp_s[hh, ch] = jnp.concatenate([p_mat, r_mat], axis=0)
                qm_s[hh, ch] = q_mat.astype(qm_s.dtype)
                u_s[hh, spans[ch], :] = u_mat.astype(u_s.dtype)
        return carry

    lax.fori_loop(0, heads, prepare_head, 0)

    ng = ng_ref[...]

    def scan_body(i, states):
        states = list(states)
        for u in range(unroll_b):
            ch = i * unroll_b + u
            r0 = pl.multiple_of(ch * c, c)
            rows = pl.ds(r0, c)
            products = []
            for hh in range(heads):
                both = _dot(rp_s[hh, ch], states[hh].astype(BF16))
                products.append((both[d:, :] + u_s[hh, rows, :], both[:d, :]))
            for hh in range(heads):
                lanes = slice(hh * d, (hh + 1) * d)
                o, ps = products[hh]
                dec = jnp.exp(gcum_s[hh, pl.ds(r0 + c - 1, 1), :])
                states[hh] = states[hh] * dec - ps + qm_s[hh, ch]
                z_act = z_ref[0, hh, rows, :].astype(F32)
                o_ref[0, rows, lanes] = (_rmsnorm(o, ng) * z_act).astype(o_ref.dtype)
        return tuple(states)

    lax.fori_loop(0, n_chunks // unroll_b, scan_body,
                  tuple(jnp.zeros((d, d), F32) for _ in range(heads)))


def _gdn_branch(q4, k4, v4, z4, small3, small_t, norm_g, *, heads):
    b, _, t, d = q4.shape
    assert d == LANES == 2 * GDN_CHUNK, "chunk-local matrices are kept as [m | m] slabs one vreg wide"
    n_chunks = t // GDN_CHUNK
    head_spec = pl.BlockSpec((1, heads, t, d), lambda bi, hi: (bi, hi, 0, 0))
    row_spec = pl.BlockSpec((1, LANES), lambda bi, hi: (0, 0))
    return pl.pallas_call(
        functools.partial(_gdn_kernel, heads=heads, unroll_b=16, lockstep_chunks=n_chunks),
        grid=(b, GDN_HEADS // heads),
        in_specs=[
            head_spec, head_spec, head_spec, head_spec,
            pl.BlockSpec((1, t, LANES), lambda bi, hi: (bi, 0, 0)),
            pl.BlockSpec((1, GDN_HEADS, t), lambda bi, hi: (bi, SMALL_GA // GDN_HEADS, 0)),
            row_spec,
        ],
        out_specs=pl.BlockSpec((1, t, heads * d), lambda bi, hi: (bi, 0, hi)),
        out_shape=jax.ShapeDtypeStruct((b, t, GDN_HEADS * d), BF16),
        scratch_shapes=[
            pltpu.VMEM((heads, t, LANES), F32),
            pltpu.VMEM((heads, t, LANES), F32),
            pltpu.VMEM((heads, n_chunks, d + GDN_CHUNK, d), BF16),
            pltpu.VMEM((heads, n_chunks, d, d), BF16),
            pltpu.VMEM((heads, t, d), BF16),
        ],
        compiler_params=pltpu.CompilerParams(
            dimension_semantics=("arbitrary", "arbitrary"),
            vmem_limit_bytes=VMEM_LIMIT_BYTES),
        name="gdn_branch",
    )(q4, k4, v4, z4, small3, small_t, norm_g)


def _fox_kernel(q_ref, k_ref, v_ref, ct_ref, o_ref, v1_s, *, heads):
    h0 = pl.program_id(1) * heads
    t = q_ref.shape[1]
    d = FOX_D
    tq = FOX_TQ
    nq = t // tq
    c_rows = [ct_ref[0, pl.ds(h0 + hh, 1), :] for hh in range(heads)]
    for hh in range(heads):
        v1_s[hh, :, :d] = v_ref[0, :, hh * d:(hh + 1) * d]
        v1_s[hh, :, d:] = jnp.ones((t, d), BF16)

    ri = lax.broadcasted_iota(jnp.int32, (tq, tq), 0)
    ci = lax.broadcasted_iota(jnp.int32, (tq, tq), 1)
    causal = ri >= ci

    def scores(hh, qi):
        lanes = slice(hh * d, (hh + 1) * d)
        return _dot_nt(q_ref[0, qi * tq:(qi + 1) * tq, lanes], k_ref[0, :(qi + 1) * tq, lanes])

    def finish(hh, qi, s):
        lanes = slice(hh * d, (hh + 1) * d)
        n_keys = (qi + 1) * tq
        s = s - c_rows[hh][:, :n_keys]
        diag = jnp.where(causal, s[:, n_keys - tq:], NEG_BIG)
        s = diag if qi == 0 else jnp.concatenate([s[:, :n_keys - tq], diag], axis=1)
        p = jnp.exp2(s - jnp.max(s, axis=-1, keepdims=True))
        pv = _dot(p.astype(BF16), v1_s[hh, :n_keys, :])
        o_ref[0, qi * tq:(qi + 1) * tq, lanes] = (pv[:, :d] / pv[:, d:]).astype(o_ref.dtype)

    work = [(hh, qi) for qi in range(nq) for hh in range(heads)]
    s_next = scores(*work[0])
    for idx, item in enumerate(work):
        s_cur = s_next
        if idx + 1 < len(work):
            s_next = scores(*work[idx + 1])
        finish(*item, s_cur)


def _fox_branch(q3, k3, v3, small_t, *, heads):
    b, t, _ = q3.shape
    seq_spec = pl.BlockSpec((1, t, heads * FOX_D), lambda bi, hi: (bi, 0, hi))
    return pl.pallas_call(
        functools.partial(_fox_kernel, heads=heads),
        grid=(b, FOX_HEADS // heads),
        in_specs=[
            seq_spec, seq_spec, seq_spec,
            pl.BlockSpec((1, FOX_HEADS, t), lambda bi, hi: (bi, SMALL_FF // FOX_HEADS, 0)),
        ],
        out_specs=seq_spec,
        out_shape=jax.ShapeDtypeStruct((b, t, FOX_HEADS * FOX_D), BF16),
        scratch_shapes=[pltpu.VMEM((heads, t, 2 * FOX_D), BF16)],
        compiler_params=pltpu.CompilerParams(
            dimension_semantics=("arbitrary", "arbitrary"),
            vmem_limit_bytes=VMEM_LIMIT_BYTES),
        name="fox_branch",
    )(q3, k3, v3, small_t)


def _merge_mlp_kernel(oa_ref, ob_ref, ga_ref, gb_ref, x_ref, pa_ref, pb_ref, wo_ref, ng_ref,
                      wu_ref, wd_ref, out_ref, *, ff_chunk):
    ya = _dot(oa_ref[...], pa_ref[...])
    yb = _dot(ob_ref[...], pb_ref[...])
    merged = ga_ref[...].astype(F32) * ya + gb_ref[...].astype(F32) * yb
    hid = x_ref[...] + _dot(merged.astype(BF16), wo_ref[...])
    hn = _rmsnorm(hid, ng_ref[...]).astype(BF16)
    acc = hid
    d_ff = wu_ref.shape[1]
    for c0 in range(0, d_ff, ff_chunk):
        up = _dot(hn, wu_ref[:, c0:c0 + ff_chunk])
        act = jnp.square(jnp.maximum(up, 0.0)).astype(BF16)
        acc = acc + _dot(act, wd_ref[c0:c0 + ff_chunk, :])
    out_ref[...] = acc


def _merge_mlp(oa, ob, gate_a, gate_b, x2d, pa, pb, wo, ng, wu, wd, *, tm):
    m, d = x2d.shape
    d_ff = wu.shape[1]
    tile = pl.BlockSpec((tm, d), lambda i: (i, 0))
    const = lambda shape: pl.BlockSpec(shape, lambda i: (0, 0), pipeline_mode=pl.Buffered(1))
    return pl.pallas_call(
        functools.partial(_merge_mlp_kernel, ff_chunk=1024),
        grid=(m // tm,),
        in_specs=[
            tile, tile, tile, tile, tile,
            const((d, d)), const((d, d)), const((d, d)), const((1, d)),
            const((d, d_ff)), const((d_ff, d)),
        ],
        out_specs=tile,
        out_shape=jax.ShapeDtypeStruct((m, d), F32),
        compiler_params=pltpu.CompilerParams(
            dimension_semantics=("arbitrary",),
            vmem_limit_bytes=VMEM_LIMIT_BYTES),
        name="merge_mlp",
    )(oa, ob, gate_a, gate_b, x2d, pa, pb, wo, ng, wu, wd)


def _lane_row(values, offset):
    row = jnp.zeros((1, LANES), F32)
    return row.at[0, offset:offset + values.shape[0]].set(values.astype(F32))


def kernel(x, norm_mix_g, w_in, gdn_conv_w, gdn_a_log, gdn_dt_bias, gdn_norm_g, fox_q_norm_g,
           fox_k_norm_g, fox_f_bias, w_proj_gdn, w_proj_fox, w_out, norm_mlp_g, w_up, w_down):
    b, t, d = x.shape
    depth = w_in.shape[0]
    qk_w = GDN_HEADS * GDN_D
    fox_w = FOX_HEADS * FOX_D
    o_gz = 3 * qk_w
    o_ga = 4 * qk_w
    o_gb = o_ga + GDN_HEADS
    o_fq = o_gb + GDN_HEADS
    o_fv = o_fq + 2 * fox_w
    o_ff = o_fq + 3 * fox_w
    o_gate = o_ff + FOX_HEADS
    tm = 512

    for l in range(depth):
        w = w_in[l]
        w_small = jnp.concatenate(
            [w[:, o_ga:o_fq], w[:, o_ff:o_gate],
             jnp.zeros((d, LANES - 2 * GDN_HEADS - FOX_HEADS), F32)], axis=1)
        ws_hi = w_small.astype(BF16)
        ws_lo = (w_small - ws_hi.astype(F32)).astype(BF16)

        x2d = x.reshape(b * t, d)
        gain = norm_mix_g[l][None, :]
        w_main = jnp.concatenate([w[:, :o_ga], w[:, o_fq:o_ff], w[:, o_gate:]], axis=1).astype(BF16)
        aux = jnp.zeros((GDN_CONV, w_main.shape[1]), F32)
        aux = aux.at[:, :o_gz].set(gdn_conv_w[l])
        aux = aux.at[0, o_ga:o_ga + fox_w].set(
            jnp.tile(fox_q_norm_g[l] * (LOG2_E * FOX_D ** -0.5), FOX_HEADS))
        aux = aux.at[0, o_ga + fox_w:o_ga + 2 * fox_w].set(jnp.tile(fox_k_norm_g[l], FOX_HEADS))
        gq, gk, gv, gz, fq, fk, fv, gate_a, gate_b = _in_projection(
            x2d, gain, w_main, aux,
            modes=("conv_l2_scaled", "conv_l2", "conv", "silu", "rms", "rms", "copy", "sigmoid", "sigmoid"),
            head_major=(True, True, True, True, False, False, False, False, False),
            tm=tm // 2, seq_len=t, name="in_projection")

        bias_row = _lane_row(gdn_dt_bias[l], SMALL_GA) + _lane_row(fox_f_bias[l], SMALL_FF)
        small3, small_t = _gates(x, gain, jnp.concatenate([ws_hi, ws_lo], axis=1),
                                 _lane_row(gdn_a_log[l], SMALL_GA), bias_row)
        seq = lambda a: a.reshape(b, t, -1)
        o_a = _gdn_branch(gq, gk, gv, gz, small3, small_t, gdn_norm_g[l][None, :], heads=4)
        o_b = _fox_branch(seq(fq), seq(fk), seq(fv), small_t, heads=4)

        out = _merge_mlp(
            o_a.reshape(b * t, qk_w), o_b.reshape(b * t, fox_w), gate_a, gate_b, x2d,
            w_proj_gdn[l].astype(BF16), w_proj_fox[l].astype(BF16), w_out[l].astype(BF16),
            norm_mlp_g[l][None, :], w_up[l].astype(BF16), w_down[l].astype(BF16), tm=tm)
        x = out.reshape(b, t, d)
    return x
```

```python
import functools

import jax
import jax.numpy as jnp
from jax import lax
from jax.experimental import pallas as pl
from jax.experimental.pallas import tpu as pltpu

F32 = jnp.float32
BF16 = jnp.bfloat16

LANES = 128
SUBLANES = 8
VMEM_LIMIT_BYTES = 56 * 1024 * 1024

EPS = 1e-6
GDN_HEADS = 8
GDN_D = 128
GDN_CONV = 4
GDN_CHUNK = 64
GDN_INV_BLOCK = 16
FOX_HEADS = 8
FOX_D = 128
FOX_TQ = 256
SMALL_GA, SMALL_GB, SMALL_FF = 0, 8, 16
NEG_BIG = -1e30
LOG2_E = 1.4426950408889634


def _dot(a, b):
    return jnp.dot(a, b, preferred_element_type=F32)


def _dot_nt(a, b):
    return lax.dot_general(a, b, (((1,), (1,)), ((), ())), preferred_element_type=F32)


def _sigmoid(x):
    return 1.0 / (1.0 + jnp.exp(-x))


def _silu(x):
    half = 0.5 * x
    return half + half * jnp.tanh(half)


def _softplus(x):
    return jnp.maximum(x, 0.0) + jnp.log1p(jnp.exp(-jnp.abs(x)))


def _rmsnorm(x, g):
    return x * lax.rsqrt(jnp.mean(x * x, axis=-1, keepdims=True) + EPS) * g


def _pick_lane(x, lane):
    ids = lax.broadcasted_iota(jnp.int32, x.shape, 1)
    col = jnp.sum(jnp.where(ids == lane, x, 0.0), axis=-1, keepdims=True)
    return jnp.broadcast_to(col, x.shape)


def _segment_cumsum(x, seg):
    pos = lax.broadcasted_iota(jnp.int32, x.shape, 0) % seg
    s = 1
    while s < seg:
        x = x + jnp.where(pos >= s, pltpu.roll(x, s, axis=0), 0.0)
        s *= 2
    return x


def _head_normalise(a, gain, *, mean, scale=1.0):
    heads = []
    for c0 in range(0, a.shape[1], LANES):
        a_h = a[:, c0:c0 + LANES]
        ss = jnp.sum(a_h * a_h, axis=-1, keepdims=True)
        if mean:
            ss = ss * (1.0 / LANES)
        inv = lax.rsqrt(ss + EPS)
        if scale != 1.0:
            inv = inv * scale
        heads.append(a_h * inv if gain is None else a_h * inv * gain[:, c0:c0 + LANES])
    return jnp.concatenate(heads, axis=1)


def _causal_conv_silu(y, tail, w):
    tm, cw = y.shape
    groups = y.reshape(tm // SUBLANES, SUBLANES, cw)
    row_in_group = lax.broadcasted_iota(jnp.int32, (1, SUBLANES, cw), 1)
    acc = None
    for i in range(GDN_CONV):
        s = GDN_CONV - 1 - i
        if s == 0:
            shifted = groups
        else:
            rotated = pltpu.roll(groups, s, axis=1)
            previous = jnp.concatenate([pltpu.roll(tail, s, axis=0)[None], rotated[:-1]], axis=0)
            shifted = jnp.where(row_in_group < s, previous, rotated)
        term = shifted * w[i:i + 1, :][None]
        acc = term if acc is None else acc + term
    acc = acc.reshape(tm, cw)
    return _silu(acc)


_CONV_MODES = ("conv", "conv_l2", "conv_l2_scaled")
_AUX_MODES = _CONV_MODES + ("rms",)


def _proj_kernel(x_ref, g_ref, w_ref, *refs, modes, head_major, width, chunk, tiles_per_seq):
    refs = list(refs)
    aux_ref = refs.pop(0) if any(mode in _AUX_MODES for mode in modes) else None
    outs = refs[:len(modes)]
    tail_s = refs[len(modes)] if len(refs) > len(modes) else None
    tm = x_ref.shape[0]
    u = _rmsnorm(x_ref[...], g_ref[...]).astype(BF16)
    sequence_start = (pl.program_id(0) % tiles_per_seq) == 0
    items = [(gi, c0) for gi in range(len(modes)) for c0 in range(0, width, chunk)]
    heavy = [item for item in items if modes[item[0]] in _CONV_MODES]
    light = [item for item in items if modes[item[0]] not in _CONV_MODES]
    per_heavy = len(light) // len(heavy) if heavy else 0
    order = []
    for item in heavy:
        order.append(item)
        order.extend(light[:per_heavy])
        light = light[per_heavy:]
    order.extend(light)
    for gi, c0 in order:
        mode, out, by_head = modes[gi], outs[gi], head_major[gi]
        cols = slice(gi * width + c0, gi * width + c0 + chunk)
        y = _dot(u, w_ref[:, cols])
        if mode in _CONV_MODES:
            tail = jnp.where(sequence_start, 0.0, tail_s[:, cols])
            tail_s[:, cols] = y[tm - SUBLANES:, :]
            y = _causal_conv_silu(y, tail, aux_ref[0:GDN_CONV, cols])
            if mode != "conv":
                scale = GDN_D ** -0.5 if mode == "conv_l2_scaled" else 1.0
                y = _head_normalise(y, None, mean=False, scale=scale)
        elif mode == "rms":
            y = _head_normalise(y, aux_ref[0:1, cols], mean=True)
        elif mode == "silu":
            y = _silu(y)
        elif mode == "sigmoid":
            y = _sigmoid(y)
        else:
            assert mode == "copy", mode
        y = y.astype(out.dtype)
        if by_head:
            for j in range(chunk // LANES):
                out[0, c0 // LANES + j] = y[:, j * LANES:(j + 1) * LANES]
        else:
            out[:, c0:c0 + chunk] = y


def _in_projection(x2d, gain, w, aux, *, modes, head_major, tm, seq_len, name):
    m, d = x2d.shape
    n = w.shape[1]
    width = n // len(modes)
    heads = width // LANES
    tiles_per_seq = seq_len // tm
    assert (aux is not None) == any(mode in _AUX_MODES for mode in modes)
    needs_tail = any(mode in _CONV_MODES for mode in modes)
    const = lambda shape: pl.BlockSpec(shape, lambda i: (0, 0), pipeline_mode=pl.Buffered(1))
    operands = [x2d, gain, w] + ([aux] if aux is not None else [])
    flat_spec = pl.BlockSpec((tm, width), lambda i: (i, 0))
    flat_shape = jax.ShapeDtypeStruct((m, width), BF16)
    head_spec = pl.BlockSpec((1, heads, tm, LANES), lambda i: (i // tiles_per_seq, 0, i % tiles_per_seq, 0))
    head_shape = jax.ShapeDtypeStruct((m // seq_len, heads, seq_len, LANES), BF16)
    return pl.pallas_call(
        functools.partial(_proj_kernel, modes=modes, head_major=head_major, width=width, chunk=512,
                          tiles_per_seq=tiles_per_seq),
        grid=(m // tm,),
        in_specs=[pl.BlockSpec((tm, d), lambda i: (i, 0))] + [const(a.shape) for a in operands[1:]],
        out_specs=[head_spec if by_head else flat_spec for by_head in head_major],
        out_shape=[head_shape if by_head else flat_shape for by_head in head_major],
        scratch_shapes=[pltpu.VMEM((SUBLANES, n), F32)] if needs_tail else [],
        compiler_params=pltpu.CompilerParams(
            dimension_semantics=("arbitrary",),
            vmem_limit_bytes=VMEM_LIMIT_BYTES),
        name=name,
    )(*operands)


def _gate_kernel(x_ref, g_ref, ws_ref, alog_ref, bias_ref, out_ref, out_t_ref, loc_s):
    t = x_ref.shape[0]
    n_chunks = t // GDN_CHUNK
    u = _rmsnorm(x_ref[...], g_ref[...])
    u_hi = u.astype(BF16)
    u_lo = (u - u_hi.astype(F32)).astype(BF16)
    hi_terms = _dot(u_hi, ws_ref[...])
    logits = hi_terms[:, :LANES] + hi_terms[:, LANES:] + _dot(u_lo, ws_ref[:, :LANES])
    shifted = logits + bias_ref[...]
    log_decay = -jnp.exp(alog_ref[...]) * _softplus(shifted)
    beta = _sigmoid(logits)
    log2_forget = -_softplus(-shifted) * LOG2_E
    lane = lax.broadcasted_iota(jnp.int32, logits.shape, 1)
    vals = jnp.where(lane < SMALL_GB, log_decay, jnp.where(lane < SMALL_FF, beta, log2_forget))
    local = _segment_cumsum(vals, GDN_CHUNK)
    loc_s[...] = local
    totals = loc_s[pl.ds(GDN_CHUNK - 1, n_chunks, stride=GDN_CHUNK), :]
    carried = _segment_cumsum(totals, n_chunks) - totals
    full = local + jnp.broadcast_to(carried[:, None, :], (n_chunks, GDN_CHUNK, LANES)).reshape(t, LANES)
    out = jnp.where(lane < SMALL_GB, local, jnp.where(lane < SMALL_FF, vals, full))
    out_ref[0] = out
    out_t_ref[0] = jnp.transpose(out)


def _gates(x3, gain, ws_hi_lo, alog_row, bias_row):
    b, t, d = x3.shape
    const = lambda shape: pl.BlockSpec(shape, lambda i: (0, 0))
    return pl.pallas_call(
        _gate_kernel,
        grid=(b,),
        in_specs=[
            pl.BlockSpec((None, t, d), lambda i: (i, 0, 0)),
            const((1, d)), const((d, 2 * LANES)), const((1, LANES)), const((1, LANES)),
        ],
        out_specs=[
            pl.BlockSpec((1, t, LANES), lambda i: (i, 0, 0)),
            pl.BlockSpec((1, LANES, t), lambda i: (i, 0, 0)),
        ],
        out_shape=[
            jax.ShapeDtypeStruct((b, t, LANES), F32),
            jax.ShapeDtypeStruct((b, LANES, t), F32),
        ],
        scratch_shapes=[pltpu.VMEM((t, LANES), F32)],
        compiler_params=pltpu.CompilerParams(
            dimension_semantics=("arbitrary",),
            vmem_limit_bytes=VMEM_LIMIT_BYTES),
        name="gates",
    )(x3, gain, ws_hi_lo, alog_row, bias_row)


def _unit_lower_inverse(a2, eye_hi, same_block2, low):
    c = a2.shape[0]
    lo = lambda slab: slab[:, :c].astype(BF16)
    pick = lambda low_part, high_part: jnp.where(low, low_part, high_part).astype(BF16)
    e2 = jnp.where(same_block2, -a2, 0.0)
    off2 = a2 + e2
    y = eye_hi + e2
    r = _dot(lo(e2), e2.astype(BF16))
    yield
    r = _dot(lo(r), pick(r, y))
    yield
    y = y + r
    r = _dot(lo(r), pick(r, y))
    yield
    y = y + r
    t = y + _dot(lo(r), y.astype(BF16))
    yield
    t_low = pltpu.roll(t, c, axis=1)
    m2 = _dot(lo(t_low), off2.astype(BF16))
    yield
    r = _dot(lo(m2), pick(m2, t))
    yield
    v = t - r
    inv = v + _dot(lo(r), v.astype(BF16))
    yield
    return pltpu.roll(inv, c, axis=1)[:, :c]


def _run_in_lockstep(generators):
    results = [None] * len(generators)
    live = list(enumerate(generators))
    while live:
        still_live = []
        for idx, gen in live:
            try:
                next(gen)
                still_live.append((idx, gen))
            except StopIteration as stop:
                results[idx] = stop.value
        live = still_live
    return results


def _gdn_kernel(q_ref, k_ref, v_ref, z_ref, sm_ref, gt_ref, ng_ref, o_ref,
                gcum_s, beta_s, rp_s, qm_s, u_s,
                *, heads, unroll_b, lockstep_chunks):
    h0 = pl.program_id(1) * heads
    t = q_ref.shape[2]
    c = GDN_CHUNK
    d = GDN_D
    n_chunks = t // c

    ri = lax.broadcasted_iota(jnp.int32, (c, 2 * c), 0)
    lane = lax.broadcasted_iota(jnp.int32, (c, 2 * c), 1)
    low = lane < c
    ci = jnp.where(low, lane, lane - c)
    tri_incl = ri >= ci
    tri_strict = ri > ci
    same_block = (ri // GDN_INV_BLOCK) == (ci // GDN_INV_BLOCK)
    eye_hi = jnp.where(lane == ri + c, 1.0, 0.0).astype(F32)

    def chunk_prepare(q_bf, k_bf, v_bf, gb, bb, g_row2):
        diff = gb - g_row2
        decay = jnp.where(tri_incl, jnp.exp(jnp.where(tri_incl, diff, 0.0)), 0.0)
        q = q_bf.astype(F32)
        k = k_bf.astype(F32)
        v = v_bf.astype(F32)
        qk_kk = _dot_nt(jnp.concatenate([q_bf, k_bf], axis=0),
                        jnp.concatenate([k_bf, k_bf], axis=0))
        yield
        qk = qk_kk[:c, :]
        kk = qk_kk[c:, :]
        a_mat = jnp.where(tri_strict, bb * kk * decay, 0.0)
        t_inv = yield from _unit_lower_inverse(a_mat, eye_hi, same_block, low)
        e_g = jnp.exp(gb)
        rhs = jnp.concatenate([v * bb, k * (bb * e_g)], axis=1).astype(BF16)
        sol = _dot(t_inv.astype(BF16), rhs).astype(BF16)
        yield
        g_last = gb[c - 1:c, :]
        k_dec = k * jnp.exp(g_last - gb)
        attn = (qk * decay)[:, :c]
        lhs = jnp.concatenate([jnp.transpose(k_dec), attn], axis=0).astype(BF16)
        fused = _dot(lhs, sol)
        q_mat = fused[:d, :d]
        p_mat = fused[:d, d:]
        u_mat = fused[d:, :d]
        r_mat = q * e_g - fused[d:, d:]
        return p_mat.astype(BF16), q_mat, r_mat.astype(BF16), u_mat

    def prepare_head(hh, carry):
        sm = sm_ref[0]
        gcum_s[hh] = _pick_lane(sm, SMALL_GA + h0 + hh)
        beta_s[hh] = _pick_lane(sm, SMALL_GB + h0 + hh)
        g_rows = gt_ref[0, pl.ds(h0 + hh, 1), :]
        spans = [slice(ch * c, (ch + 1) * c) for ch in range(n_chunks)]
        g_row_slabs = []
        for pair in range(n_chunks // 2):
            window = g_rows[:, pair * 2 * c:(pair + 1) * 2 * c]
            swapped = pltpu.roll(window, c, axis=1)
            g_row_slabs += [jnp.where(low[:1], window, swapped), jnp.where(low[:1], swapped, window)]
        for first in range(0, n_chunks, lockstep_chunks):
            group = range(first, first + lockstep_chunks)
            loaded = [(q_ref[0, hh, spans[ch], :], k_ref[0, hh, spans[ch], :], v_ref[0, hh, spans[ch], :],
                       gcum_s[hh, spans[ch], :], beta_s[hh, spans[ch], :], g_row_slabs[ch]) for ch in group]
            results = _run_in_lockstep([chunk_prepare(*args) for args in loaded])
            for ch, (p_mat, q_mat, r_mat, u_mat) in zip(group, results):
                rp_s[hh, ch] = jnp.concatenate([p_mat, r_mat], axis=0)
                qm_s[hh, ch] = q_mat.astype(qm_s.dtype)
                u_s[hh, spans[ch], :] = u_mat.astype(u_s.dtype)
        return carry

    lax.fori_loop(0, heads, prepare_head, 0)

    ng = ng_ref[...]

    def scan_body(i, states):
        states = list(states)
        for u in range(unroll_b):
            ch = i * unroll_b + u
            r0 = pl.multiple_of(ch * c, c)
            rows = pl.ds(r0, c)
            products = []
            for hh in range(heads):
                both = _dot(rp_s[hh, ch], states[hh].astype(BF16))
                products.append((both[d:, :] + u_s[hh, rows, :], both[:d, :]))
            for hh in range(heads):
                lanes = slice(hh * d, (hh + 1) * d)
                o, ps = products[hh]
                dec = jnp.exp(gcum_s[hh, pl.ds(r0 + c - 1, 1), :])
                states[hh] = states[hh] * dec - ps + qm_s[hh, ch]
                z_act = z_ref[0, hh, rows, :].astype(F32)
                o_ref[0, rows, lanes] = (_rmsnorm(o, ng) * z_act).astype(o_ref.dtype)
        return tuple(states)

    lax.fori_loop(0, n_chunks // unroll_b, scan_body,
                  tuple(jnp.zeros((d, d), F32) for _ in range(heads)))


def _gdn_branch(q4, k4, v4, z4, small3, small_t, norm_g, *, heads):
    b, _, t, d = q4.shape
    assert d == LANES == 2 * GDN_CHUNK, "chunk-local matrices are kept as [m | m] slabs one vreg wide"
    n_chunks = t // GDN_CHUNK
    head_spec = pl.BlockSpec((1, heads, t, d), lambda bi, hi: (bi, hi, 0, 0))
    row_spec = pl.BlockSpec((1, LANES), lambda bi, hi: (0, 0))
    return pl.pallas_call(
        functools.partial(_gdn_kernel, heads=heads, unroll_b=16, lockstep_chunks=n_chunks),
        grid=(b, GDN_HEADS // heads),
        in_specs=[
            head_spec, head_spec, head_spec, head_spec,
            pl.BlockSpec((1, t, LANES), lambda bi, hi: (bi, 0, 0)),
            pl.BlockSpec((1, GDN_HEADS, t), lambda bi, hi: (bi, SMALL_GA // GDN_HEADS, 0)),
            row_spec,
        ],
        out_specs=pl.BlockSpec((1, t, heads * d), lambda bi, hi: (bi, 0, hi)),
        out_shape=jax.ShapeDtypeStruct((b, t, GDN_HEADS * d), BF16),
        scratch_shapes=[
            pltpu.VMEM((heads, t, LANES), F32),
            pltpu.VMEM((heads, t, LANES), F32),
            pltpu.VMEM((heads, n_chunks, d + GDN_CHUNK, d), BF16),
            pltpu.VMEM((heads, n_chunks, d, d), BF16),
            pltpu.VMEM((heads, t, d), BF16),
        ],
        compiler_params=pltpu.CompilerParams(
            dimension_semantics=("arbitrary", "arbitrary"),
            vmem_limit_bytes=VMEM_LIMIT_BYTES),
        name="gdn_branch",
    )(q4, k4, v4, z4, small3, small_t, norm_g)


def _fox_kernel(q_ref, k_ref, v_ref, ct_ref, o_ref, v1_s, *, heads):
    h0 = pl.program_id(1) * heads
    t = q_ref.shape[1]
    d = FOX_D
    tq = FOX_TQ
    nq = t // tq
    c_rows = [ct_ref[0, pl.ds(h0 + hh, 1), :] for hh in range(heads)]
    for hh in range(heads):
        v1_s[hh, :, :d] = v_ref[0, :, hh * d:(hh + 1) * d]
        v1_s[hh, :, d:] = jnp.ones((t, d), BF16)

    ri = lax.broadcasted_iota(jnp.int32, (tq, tq), 0)
    ci = lax.broadcasted_iota(jnp.int32, (tq, tq), 1)
    causal = ri >= ci

    def scores(hh, qi):
        lanes = slice(hh * d, (hh + 1) * d)
        return _dot_nt(q_ref[0, qi * tq:(qi + 1) * tq, lanes], k_ref[0, :(qi + 1) * tq, lanes])

    def finish(hh, qi, s):
        lanes = slice(hh * d, (hh + 1) * d)
        n_keys = (qi + 1) * tq
        s = s - c_rows[hh][:, :n_keys]
        diag = jnp.where(causal, s[:, n_keys - tq:], NEG_BIG)
        s = diag if qi == 0 else jnp.concatenate([s[:, :n_keys - tq], diag], axis=1)
        p = jnp.exp2(s - jnp.max(s, axis=-1, keepdims=True))
        pv = _dot(p.astype(BF16), v1_s[hh, :n_keys, :])
        o_ref[0, qi * tq:(qi + 1) * tq, lanes] = (pv[:, :d] / pv[:, d:]).astype(o_ref.dtype)

    work = [(hh, qi) for qi in range(nq) for hh in range(heads)]
    s_next = scores(*work[0])
    for idx, item in enumerate(work):
        s_cur = s_next
        if idx + 1 < len(work):
            s_next = scores(*work[idx + 1])
        finish(*item, s_cur)


def _fox_branch(q3, k3, v3, small_t, *, heads):
    b, t, _ = q3.shape
    seq_spec = pl.BlockSpec((1, t, heads * FOX_D), lambda bi, hi: (bi, 0, hi))
    return pl.pallas_call(
        functools.partial(_fox_kernel, heads=heads),
        grid=(b, FOX_HEADS // heads),
        in_specs=[
            seq_spec, seq_spec, seq_spec,
            pl.BlockSpec((1, FOX_HEADS, t), lambda bi, hi: (bi, SMALL_FF // FOX_HEADS, 0)),
        ],
        out_specs=seq_spec,
        out_shape=jax.ShapeDtypeStruct((b, t, FOX_HEADS * FOX_D), BF16),
        scratch_shapes=[pltpu.VMEM((heads, t, 2 * FOX_D), BF16)],
        compiler_params=pltpu.CompilerParams(
            dimension_semantics=("arbitrary", "arbitrary"),
            vmem_limit_bytes=VMEM_LIMIT_BYTES),
        name="fox_branch",
    )(q3, k3, v3, small_t)


def _merge_mlp_kernel(oa_ref, ob_ref, ga_ref, gb_ref, x_ref, pa_ref, pb_ref, wo_ref, ng_ref,
                      wu_ref, wd_ref, out_ref, *, ff_chunk):
    ya = _dot(oa_ref[...], pa_ref[...])
    yb = _dot(ob_ref[...], pb_ref[...])
    merged = ga_ref[...].astype(F32) * ya + gb_ref[...].astype(F32) * yb
    hid = x_ref[...] + _dot(merged.astype(BF16), wo_ref[...])
    hn = _rmsnorm(hid, ng_ref[...]).astype(BF16)
    acc = hid
    d_ff = wu_ref.shape[1]
    for c0 in range(0, d_ff, ff_chunk):
        up = _dot(hn, wu_ref[:, c0:c0 + ff_chunk])
        act = jnp.square(jnp.maximum(up, 0.0)).astype(BF16)
        acc = acc + _dot(act, wd_ref[c0:c0 + ff_chunk, :])
    out_ref[...] = acc


def _merge_mlp(oa, ob, gate_a, gate_b, x2d, pa, pb, wo, ng, wu, wd, *, tm):
    m, d = x2d.shape
    d_ff = wu.shape[1]
    tile = pl.BlockSpec((tm, d), lambda i: (i, 0))
    const = lambda shape: pl.BlockSpec(shape, lambda i: (0, 0), pipeline_mode=pl.Buffered(1))
    return pl.pallas_call(
        functools.partial(_merge_mlp_kernel, ff_chunk=1024),
        grid=(m // tm,),
        in_specs=[
            tile, tile, tile, tile, tile,
            const((d, d)), const((d, d)), const((d, d)), const((1, d)),
            const((d, d_ff)), const((d_ff, d)),
        ],
        out_specs=tile,
        out_shape=jax.ShapeDtypeStruct((m, d), F32),
        compiler_params=pltpu.CompilerParams(
            dimension_semantics=("arbitrary",),
            vmem_limit_bytes=VMEM_LIMIT_BYTES),
        name="merge_mlp",
    )(oa, ob, gate_a, gate_b, x2d, pa, pb, wo, ng, wu, wd)


def _lane_row(values, offset):
    row = jnp.zeros((1, LANES), F32)
    return row.at[0, offset:offset + values.shape[0]].set(values.astype(F32))


def kernel(x, norm_mix_g, w_in, gdn_conv_w, gdn_a_log, gdn_dt_bias, gdn_norm_g, fox_q_norm_g,
           fox_k_norm_g, fox_f_bias, w_proj_gdn, w_proj_fox, w_out, norm_mlp_g, w_up, w_down):
    b, t, d = x.shape
    depth = w_in.shape[0]
    qk_w = GDN_HEADS * GDN_D
    fox_w = FOX_HEADS * FOX_D
    o_gz = 3 * qk_w
    o_ga = 4 * qk_w
    o_gb = o_ga + GDN_HEADS
    o_fq = o_gb + GDN_HEADS
    o_fv = o_fq + 2 * fox_w
    o_ff = o_fq + 3 * fox_w
    o_gate = o_ff + FOX_HEADS
    tm = 512

    for l in range(depth):
        w = w_in[l]
        w_small = jnp.concatenate(
            [w[:, o_ga:o_fq], w[:, o_ff:o_gate],
             jnp.zeros((d, LANES - 2 * GDN_HEADS - FOX_HEADS), F32)], axis=1)
        ws_hi = w_small.astype(BF16)
        ws_lo = (w_small - ws_hi.astype(F32)).astype(BF16)

        x2d = x.reshape(b * t, d)
        gain = norm_mix_g[l][None, :]
        w_main = jnp.concatenate([w[:, :o_ga], w[:, o_fq:o_ff], w[:, o_gate:]], axis=1).astype(BF16)
        aux = jnp.zeros((GDN_CONV, w_main.shape[1]), F32)
        aux = aux.at[:, :o_gz].set(gdn_conv_w[l])
        aux = aux.at[0, o_ga:o_ga + fox_w].set(
            jnp.tile(fox_q_norm_g[l] * (LOG2_E * FOX_D ** -0.5), FOX_HEADS))
        aux = aux.at[0, o_ga + fox_w:o_ga + 2 * fox_w].set(jnp.tile(fox_k_norm_g[l], FOX_HEADS))
        gq, gk, gv, gz, fq, fk, fv, gate_a, gate_b = _in_projection(
            x2d, gain, w_main, aux,
            modes=("conv_l2_scaled", "conv_l2", "conv", "silu", "rms", "rms", "copy", "sigmoid", "sigmoid"),
            head_major=(True, True, True, True, False, False, False, False, False),
            tm=tm, seq_len=t, name="in_projection")

        bias_row = _lane_row(gdn_dt_bias[l], SMALL_GA) + _lane_row(fox_f_bias[l], SMALL_FF)
        small3, small_t = _gates(x, gain, jnp.concatenate([ws_hi, ws_lo], axis=1),
                                 _lane_row(gdn_a_log[l], SMALL_GA), bias_row)
        seq = lambda a: a.reshape(b, t, -1)
        o_a = _gdn_branch(gq, gk, gv, gz, small3, small_t, gdn_norm_g[l][None, :], heads=4)
        o_b = _fox_branch(seq(fq), seq(fk), seq(fv), small_t, heads=4)

        out = _merge_mlp(
            o_a.reshape(b * t, qk_w), o_b.reshape(b * t, fox_w), gate_a, gate_b, x2d,
            w_proj_gdn[l].astype(BF16), w_proj_fox[l].astype(BF16), w_out[l].astype(BF16),
            norm_mlp_g[l][None, :], w_up[l].astype(BF16), w_down[l].astype(BF16), tm=tm)
        x = out.reshape(b, t, d)
    return x
```

```python
import functools

import jax
import jax.numpy as jnp
from jax import lax
from jax.experimental import pallas as pl
from jax.experimental.pallas import tpu as pltpu

F32 = jnp.float32
BF16 = jnp.bfloat16

LANES = 128
SUBLANES = 8
VMEM_LIMIT_BYTES = 56 * 1024 * 1024

EPS = 1e-6
GDN_HEADS = 8
GDN_D = 128
GDN_CONV = 4
GDN_CHUNK = 64
GDN_INV_BLOCK = 16
FOX_HEADS = 8
FOX_D = 128
FOX_TQ = 256
SMALL_GA, SMALL_GB, SMALL_FF = 0, 8, 16
NEG_BIG = -1e30
LOG2_E = 1.4426950408889634


def _dot(a, b):
    return jnp.dot(a, b, preferred_element_type=F32)


def _dot_nt(a, b):
    return lax.dot_general(a, b, (((1,), (1,)), ((), ())), preferred_element_type=F32)


def _sigmoid(x):
    return 0.5 + 0.5 * jnp.tanh(0.5 * x)


def _silu(x):
    half = 0.5 * x
    return half + half * jnp.tanh(half)


def _softplus(x):
    return jnp.maximum(x, 0.0) + jnp.log1p(jnp.exp(-jnp.abs(x)))


def _rmsnorm(x, g):
    return x * lax.rsqrt(jnp.mean(x * x, axis=-1, keepdims=True) + EPS) * g


def _pick_lane(x, lane):
    ids = lax.broadcasted_iota(jnp.int32, x.shape, 1)
    col = jnp.sum(jnp.where(ids == lane, x, 0.0), axis=-1, keepdims=True)
    return jnp.broadcast_to(col, x.shape)


def _segment_cumsum(x, seg):
    pos = lax.broadcasted_iota(jnp.int32, x.shape, 0) % seg
    s = 1
    while s < seg:
        x = x + jnp.where(pos >= s, pltpu.roll(x, s, axis=0), 0.0)
        s *= 2
    return x


def _head_normalise(a, gain, *, mean, scale=1.0):
    heads = []
    for c0 in range(0, a.shape[1], LANES):
        a_h = a[:, c0:c0 + LANES]
        ss = jnp.sum(a_h * a_h, axis=-1, keepdims=True)
        if mean:
            ss = ss * (1.0 / LANES)
        inv = lax.rsqrt(ss + EPS)
        if scale != 1.0:
            inv = inv * scale
        heads.append(a_h * inv if gain is None else a_h * inv * gain[:, c0:c0 + LANES])
    return jnp.concatenate(heads, axis=1)


def _causal_conv_silu(y, tail, w):
    tm, cw = y.shape
    groups = y.reshape(tm // SUBLANES, SUBLANES, cw)
    row_in_group = lax.broadcasted_iota(jnp.int32, (1, SUBLANES, cw), 1)
    acc = None
    for i in range(GDN_CONV):
        s = GDN_CONV - 1 - i
        if s == 0:
            shifted = groups
        else:
            rotated = pltpu.roll(groups, s, axis=1)
            previous = jnp.concatenate([pltpu.roll(tail, s, axis=0)[None], rotated[:-1]], axis=0)
            shifted = jnp.where(row_in_group < s, previous, rotated)
        term = shifted * w[i:i + 1, :][None]
        acc = term if acc is None else acc + term
    acc = acc.reshape(tm, cw)
    return _silu(acc)


_CONV_MODES = ("conv", "conv_l2", "conv_l2_scaled")
_AUX_MODES = _CONV_MODES + ("rms",)


def _proj_kernel(x_ref, g_ref, w_ref, *refs, modes, head_major, width, chunk, tiles_per_seq):
    refs = list(refs)
    aux_ref = refs.pop(0) if any(mode in _AUX_MODES for mode in modes) else None
    outs = refs[:len(modes)]
    tail_s = refs[len(modes)] if len(refs) > len(modes) else None
    tm = x_ref.shape[0]
    u = _rmsnorm(x_ref[...], g_ref[...]).astype(BF16)
    sequence_start = (pl.program_id(0) % tiles_per_seq) == 0
    items = [(gi, c0) for gi in range(len(modes)) for c0 in range(0, width, chunk)]
    heavy = [item for item in items if modes[item[0]] in _CONV_MODES]
    light = [item for item in items if modes[item[0]] not in _CONV_MODES]
    per_heavy = len(light) // len(heavy) if heavy else 0
    order = []
    for item in heavy:
        order.append(item)
        order.extend(light[:per_heavy])
        light = light[per_heavy:]
    order.extend(light)
    for gi, c0 in order:
        mode, out, by_head = modes[gi], outs[gi], head_major[gi]
        cols = slice(gi * width + c0, gi * width + c0 + chunk)
        y = _dot(u, w_ref[:, cols])
        if mode in _CONV_MODES:
            tail = jnp.where(sequence_start, 0.0, tail_s[:, cols])
            tail_s[:, cols] = y[tm - SUBLANES:, :]
            y = _causal_conv_silu(y, tail, aux_ref[0:GDN_CONV, cols])
            if mode != "conv":
                scale = GDN_D ** -0.5 if mode == "conv_l2_scaled" else 1.0
                y = _head_normalise(y, None, mean=False, scale=scale)
        elif mode == "rms":
            y = _head_normalise(y, aux_ref[0:1, cols], mean=True)
        elif mode == "silu":
            y = _silu(y)
        elif mode == "sigmoid":
            y = _sigmoid(y)
        else:
            assert mode == "copy", mode
        y = y.astype(out.dtype)
        if by_head:
            for j in range(chunk // LANES):
                out[0, c0 // LANES + j] = y[:, j * LANES:(j + 1) * LANES]
        else:
            out[:, c0:c0 + chunk] = y


def _in_projection(x2d, gain, w, aux, *, modes, head_major, tm, seq_len, name):
    m, d = x2d.shape
    n = w.shape[1]
    width = n // len(modes)
    heads = width // LANES
    tiles_per_seq = seq_len // tm
    assert (aux is not None) == any(mode in _AUX_MODES for mode in modes)
    needs_tail = any(mode in _CONV_MODES for mode in modes)
    const = lambda shape: pl.BlockSpec(shape, lambda i: (0, 0), pipeline_mode=pl.Buffered(1))
    operands = [x2d, gain, w] + ([aux] if aux is not None else [])
    flat_spec = pl.BlockSpec((tm, width), lambda i: (i, 0))
    flat_shape = jax.ShapeDtypeStruct((m, width), BF16)
    head_spec = pl.BlockSpec((1, heads, tm, LANES), lambda i: (i // tiles_per_seq, 0, i % tiles_per_seq, 0))
    head_shape = jax.ShapeDtypeStruct((m // seq_len, heads, seq_len, LANES), BF16)
    return pl.pallas_call(
        functools.partial(_proj_kernel, modes=modes, head_major=head_major, width=width, chunk=512,
                          tiles_per_seq=tiles_per_seq),
        grid=(m // tm,),
        in_specs=[pl.BlockSpec((tm, d), lambda i: (i, 0))] + [const(a.shape) for a in operands[1:]],
        out_specs=[head_spec if by_head else flat_spec for by_head in head_major],
        out_shape=[head_shape if by_head else flat_shape for by_head in head_major],
        scratch_shapes=[pltpu.VMEM((SUBLANES, n), F32)] if needs_tail else [],
        compiler_params=pltpu.CompilerParams(
            dimension_semantics=("arbitrary",),
            vmem_limit_bytes=VMEM_LIMIT_BYTES),
        name=name,
    )(*operands)


def _gate_kernel(x_ref, g_ref, ws_ref, alog_ref, bias_ref, out_ref, out_t_ref, loc_s):
    t = x_ref.shape[0]
    n_chunks = t // GDN_CHUNK
    u = _rmsnorm(x_ref[...], g_ref[...])
    u_hi = u.astype(BF16)
    u_lo = (u - u_hi.astype(F32)).astype(BF16)
    hi_terms = _dot(u_hi, ws_ref[...])
    logits = hi_terms[:, :LANES] + hi_terms[:, LANES:] + _dot(u_lo, ws_ref[:, :LANES])
    shifted = logits + bias_ref[...]
    log_decay = -jnp.exp(alog_ref[...]) * _softplus(shifted)
    beta = _sigmoid(logits)
    log2_forget = -_softplus(-shifted) * LOG2_E
    lane = lax.broadcasted_iota(jnp.int32, logits.shape, 1)
    vals = jnp.where(lane < SMALL_GB, log_decay, jnp.where(lane < SMALL_FF, beta, log2_forget))
    local = _segment_cumsum(vals, GDN_CHUNK)
    loc_s[...] = local
    totals = loc_s[pl.ds(GDN_CHUNK - 1, n_chunks, stride=GDN_CHUNK), :]
    carried = _segment_cumsum(totals, n_chunks) - totals
    full = local + jnp.broadcast_to(carried[:, None, :], (n_chunks, GDN_CHUNK, LANES)).reshape(t, LANES)
    out = jnp.where(lane < SMALL_GB, local, jnp.where(lane < SMALL_FF, vals, full))
    out_ref[0] = out
    out_t_ref[0] = jnp.transpose(out)


def _gates(x3, gain, ws_hi_lo, alog_row, bias_row):
    b, t, d = x3.shape
    const = lambda shape: pl.BlockSpec(shape, lambda i: (0, 0))
    return pl.pallas_call(
        _gate_kernel,
        grid=(b,),
        in_specs=[
            pl.BlockSpec((None, t, d), lambda i: (i, 0, 0)),
            const((1, d)), const((d, 2 * LANES)), const((1, LANES)), const((1, LANES)),
        ],
        out_specs=[
            pl.BlockSpec((1, t, LANES), lambda i: (i, 0, 0)),
            pl.BlockSpec((1, LANES, t), lambda i: (i, 0, 0)),
        ],
        out_shape=[
            jax.ShapeDtypeStruct((b, t, LANES), F32),
            jax.ShapeDtypeStruct((b, LANES, t), F32),
        ],
        scratch_shapes=[pltpu.VMEM((t, LANES), F32)],
        compiler_params=pltpu.CompilerParams(
            dimension_semantics=("arbitrary",),
            vmem_limit_bytes=VMEM_LIMIT_BYTES),
        name="gates",
    )(x3, gain, ws_hi_lo, alog_row, bias_row)


def _unit_lower_inverse(a2, eye_hi, same_block2, low):
    c = a2.shape[0]
    lo = lambda slab: slab[:, :c].astype(BF16)
    pick = lambda low_part, high_part: jnp.where(low, low_part, high_part).astype(BF16)
    e2 = jnp.where(same_block2, -a2, 0.0)
    off2 = a2 + e2
    y = eye_hi + e2
    r = _dot(lo(e2), e2.astype(BF16))
    yield
    r = _dot(lo(r), pick(r, y))
    yield
    y = y + r
    r = _dot(lo(r), pick(r, y))
    yield
    y = y + r
    t = y + _dot(lo(r), y.astype(BF16))
    yield
    t_low = pltpu.roll(t, c, axis=1)
    m2 = _dot(lo(t_low), off2.astype(BF16))
    yield
    r = _dot(lo(m2), pick(m2, t))
    yield
    v = t - r
    inv = v + _dot(lo(r), v.astype(BF16))
    yield
    return pltpu.roll(inv, c, axis=1)[:, :c]


def _run_in_lockstep(generators):
    results = [None] * len(generators)
    live = list(enumerate(generators))
    while live:
        still_live = []
        for idx, gen in live:
            try:
                next(gen)
                still_live.append((idx, gen))
            except StopIteration as stop:
                results[idx] = stop.value
        live = still_live
    return results


def _gdn_kernel(q_ref, k_ref, v_ref, z_ref, sm_ref, gt_ref, ng_ref, o_ref,
                gcum_s, beta_s, rp_s, qm_s, u_s,
                *, heads, unroll_b, lockstep_chunks):
    h0 = pl.program_id(1) * heads
    t = q_ref.shape[2]
    c = GDN_CHUNK
    d = GDN_D
    n_chunks = t // c

    ri = lax.broadcasted_iota(jnp.int32, (c, 2 * c), 0)
    lane = lax.broadcasted_iota(jnp.int32, (c, 2 * c), 1)
    low = lane < c
    ci = jnp.where(low, lane, lane - c)
    tri_incl = ri >= ci
    tri_strict = ri > ci
    same_block = (ri // GDN_INV_BLOCK) == (ci // GDN_INV_BLOCK)
    eye_hi = jnp.where(lane == ri + c, 1.0, 0.0).astype(F32)

    def chunk_prepare(q_bf, k_bf, v_bf, gb, bb, g_row2):
        diff = gb - g_row2
        decay = jnp.where(tri_incl, jnp.exp(jnp.where(tri_incl, diff, 0.0)), 0.0)
        q = q_bf.astype(F32)
        k = k_bf.astype(F32)
        v = v_bf.astype(F32)
        qk_kk = _dot_nt(jnp.concatenate([q_bf, k_bf], axis=0),
                        jnp.concatenate([k_bf, k_bf], axis=0))
        yield
        qk = qk_kk[:c, :]
        kk = qk_kk[c:, :]
        a_mat = jnp.where(tri_strict, bb * kk * decay, 0.0)
        t_inv = yield from _unit_lower_inverse(a_mat, eye_hi, same_block, low)
        e_g = jnp.exp(gb)
        rhs = jnp.concatenate([v * bb, k * (bb * e_g)], axis=1).astype(BF16)
        sol = _dot(t_inv.astype(BF16), rhs).astype(BF16)
        yield
        g_last = gb[c - 1:c, :]
        k_dec = k * jnp.exp(g_last - gb)
        attn = (qk * decay)[:, :c]
        lhs = jnp.concatenate([jnp.transpose(k_dec), attn], axis=0).astype(BF16)
        fused = _dot(lhs, sol)
        q_mat = fused[:d, :d]
        p_mat = fused[:d, d:]
        u_mat = fused[d:, :d]
        r_mat = q * e_g - fused[d:, d:]
        return p_mat.astype(BF16), q_mat, r_mat.astype(BF16), u_mat

    def prepare_head(hh, carry):
        sm = sm_ref[0]
        gcum_s[hh] = _pick_lane(sm, SMALL_GA + h0 + hh)
        beta_s[hh] = _pick_lane(sm, SMALL_GB + h0 + hh)
        g_rows = gt_ref[0, pl.ds(h0 + hh, 1), :]
        spans = [slice(ch * c, (ch + 1) * c) for ch in range(n_chunks)]
        g_row_slabs = []
        for pair in range(n_chunks // 2):
            window = g_rows[:, pair * 2 * c:(pair + 1) * 2 * c]
            swapped = pltpu.roll(window, c, axis=1)
            g_row_slabs += [jnp.where(low[:1], window, swapped), jnp.where(low[:1], swapped, window)]
        for first in range(0, n_chunks, lockstep_chunks):
            group = range(first, first + lockstep_chunks)
            loaded = [(q_ref[0, hh, spans[ch], :], k_ref[0, hh, spans[ch], :], v_ref[0, hh, spans[ch], :],
                       gcum_s[hh, spans[ch], :], beta_s[hh, spans[ch], :], g_row_slabs[ch]) for ch in group]
            results = _run_in_lockstep([chunk_prepare(*args) for args in loaded])
            for ch, (p_mat, q_mat, r_mat, u_mat) in zip(group, results):
                rp_s[hh, ch] = jnp.concatenate([p_mat, r_mat], axis=0)
                qm_s[hh, ch] = q_mat.astype(qm_s.dtype)
                u_s[hh, spans[ch], :] = u_mat.astype(u_s.dtype)
        return carry

    lax.fori_loop(0, heads, prepare_head, 0)

    ng = ng_ref[...]

    def scan_body(i, states):
        states = list(states)
        for u in range(unroll_b):
            ch = i * unroll_b + u
            r0 = pl.multiple_of(ch * c, c)
            rows = pl.ds(r0, c)
            products = []
            for hh in range(heads):
                both = _dot(rp_s[hh, ch], states[hh].astype(BF16))
                products.append((both[d:, :] + u_s[hh, rows, :], both[:d, :]))
            for hh in range(heads):
                lanes = slice(hh * d, (hh + 1) * d)
                o, ps = products[hh]
                dec = jnp.exp(gcum_s[hh, pl.ds(r0 + c - 1, 1), :])
                states[hh] = states[hh] * dec - ps + qm_s[hh, ch]
                z_act = z_ref[0, hh, rows, :].astype(F32)
                o_ref[0, rows, lanes] = (_rmsnorm(o, ng) * z_act).astype(o_ref.dtype)
        return tuple(states)

    lax.fori_loop(0, n_chunks // unroll_b, scan_body,
                  tuple(jnp.zeros((d, d), F32) for _ in range(heads)))


def _gdn_branch(q4, k4, v4, z4, small3, small_t, norm_g, *, heads):
    b, _, t, d = q4.shape
    assert d == LANES == 2 * GDN_CHUNK, "chunk-local matrices are kept as [m | m] slabs one vreg wide"
    n_chunks = t // GDN_CHUNK
    head_spec = pl.BlockSpec((1, heads, t, d), lambda bi, hi: (bi, hi, 0, 0))
    row_spec = pl.BlockSpec((1, LANES), lambda bi, hi: (0, 0))
    return pl.pallas_call(
        functools.partial(_gdn_kernel, heads=heads, unroll_b=16, lockstep_chunks=n_chunks),
        grid=(b, GDN_HEADS // heads),
        in_specs=[
            head_spec, head_spec, head_spec, head_spec,
            pl.BlockSpec((1, t, LANES), lambda bi, hi: (bi, 0, 0)),
            pl.BlockSpec((1, GDN_HEADS, t), lambda bi, hi: (bi, SMALL_GA // GDN_HEADS, 0)),
            row_spec,
        ],
        out_specs=pl.BlockSpec((1, t, heads * d), lambda bi, hi: (bi, 0, hi)),
        out_shape=jax.ShapeDtypeStruct((b, t, GDN_HEADS * d), BF16),
        scratch_shapes=[
            pltpu.VMEM((heads, t, LANES), F32),
            pltpu.VMEM((heads, t, LANES), F32),
            pltpu.VMEM((heads, n_chunks, d + GDN_CHUNK, d), BF16),
            pltpu.VMEM((heads, n_chunks, d, d), BF16),
            pltpu.VMEM((heads, t, d), BF16),
        ],
        compiler_params=pltpu.CompilerParams(
            dimension_semantics=("arbitrary", "arbitrary"),
            vmem_limit_bytes=VMEM_LIMIT_BYTES),
        name="gdn_branch",
    )(q4, k4, v4, z4, small3, small_t, norm_g)


def _fox_kernel(q_ref, k_ref, v_ref, ct_ref, o_ref, v1_s, *, heads):
    h0 = pl.program_id(1) * heads
    t = q_ref.shape[1]
    d = FOX_D
    tq = FOX_TQ
    nq = t // tq
    c_rows = [ct_ref[0, pl.ds(h0 + hh, 1), :] for hh in range(heads)]
    for hh in range(heads):
        v1_s[hh, :, :d] = v_ref[0, :, hh * d:(hh + 1) * d]
        v1_s[hh, :, d:] = jnp.ones((t, d), BF16)

    ri = lax.broadcasted_iota(jnp.int32, (tq, tq), 0)
    ci = lax.broadcasted_iota(jnp.int32, (tq, tq), 1)
    causal = ri >= ci

    def scores(hh, qi):
        lanes = slice(hh * d, (hh + 1) * d)
        return _dot_nt(q_ref[0, qi * tq:(qi + 1) * tq, lanes], k_ref[0, :(qi + 1) * tq, lanes])

    def finish(hh, qi, s):
        lanes = slice(hh * d, (hh + 1) * d)
        n_keys = (qi + 1) * tq
        s = s - c_rows[hh][:, :n_keys]
        diag = jnp.where(causal, s[:, n_keys - tq:], NEG_BIG)
        s = diag if qi == 0 else jnp.concatenate([s[:, :n_keys - tq], diag], axis=1)
        p = jnp.exp2(s - jnp.max(s, axis=-1, keepdims=True))
        pv = _dot(p.astype(BF16), v1_s[hh, :n_keys, :])
        o_ref[0, qi * tq:(qi + 1) * tq, lanes] = (pv[:, :d] / pv[:, d:]).astype(o_ref.dtype)

    work = [(hh, qi) for qi in range(nq) for hh in range(heads)]
    s_next = scores(*work[0])
    for idx, item in enumerate(work):
        s_cur = s_next
        if idx + 1 < len(work):
            s_next = scores(*work[idx + 1])
        finish(*item, s_cur)


def _fox_branch(q3, k3, v3, small_t, *, heads):
    b, t, _ = q3.shape
    seq_spec = pl.BlockSpec((1, t, heads * FOX_D), lambda bi, hi: (bi, 0, hi))
    return pl.pallas_call(
        functools.partial(_fox_kernel, heads=heads),
        grid=(b, FOX_HEADS // heads),
        in_specs=[
            seq_spec, seq_spec, seq_spec,
            pl.BlockSpec((1, FOX_HEADS, t), lambda bi, hi: (bi, SMALL_FF // FOX_HEADS, 0)),
        ],
        out_specs=seq_spec,
        out_shape=jax.ShapeDtypeStruct((b, t, FOX_HEADS * FOX_D), BF16),
        scratch_shapes=[pltpu.VMEM((heads, t, 2 * FOX_D), BF16)],
        compiler_params=pltpu.CompilerParams(
            dimension_semantics=("arbitrary", "arbitrary"),
            vmem_limit_bytes=VMEM_LIMIT_BYTES),
        name="fox_branch",
    )(q3, k3, v3, small_t)


def _merge_mlp_kernel(oa_ref, ob_ref, ga_ref, gb_ref, x_ref, pa_ref, pb_ref, wo_ref, ng_ref,
                      wu_ref, wd_ref, out_ref, *, ff_chunk):
    ya = _dot(oa_ref[...], pa_ref[...])
    yb = _dot(ob_ref[...], pb_ref[...])
    merged = ga_ref[...].astype(F32) * ya + gb_ref[...].astype(F32) * yb
    hid = x_ref[...] + _dot(merged.astype(BF16), wo_ref[...])
    hn = _rmsnorm(hid, ng_ref[...]).astype(BF16)
    acc = hid
    d_ff = wu_ref.shape[1]
    for c0 in range(0, d_ff, ff_chunk):
        up = _dot(hn, wu_ref[:, c0:c0 + ff_chunk])
        act = jnp.square(jnp.maximum(up, 0.0)).astype(BF16)
        acc = acc + _dot(act, wd_ref[c0:c0 + ff_chunk, :])
    out_ref[...] = acc


def _merge_mlp(oa, ob, gate_a, gate_b, x2d, pa, pb, wo, ng, wu, wd, *, tm):
    m, d = x2d.shape
    d_ff = wu.shape[1]
    tile = pl.BlockSpec((tm, d), lambda i: (i, 0))
    const = lambda shape: pl.BlockSpec(shape, lambda i: (0, 0), pipeline_mode=pl.Buffered(1))
    return pl.pallas_call(
        functools.partial(_merge_mlp_kernel, ff_chunk=1024),
        grid=(m // tm,),
        in_specs=[
            tile, tile, tile, tile, tile,
            const((d, d)), const((d, d)), const((d, d)), const((1, d)),
            const((d, d_ff)), const((d_ff, d)),
        ],
        out_specs=tile,
        out_shape=jax.ShapeDtypeStruct((m, d), F32),
        compiler_params=pltpu.CompilerParams(
            dimension_semantics=("arbitrary",),
            vmem_limit_bytes=VMEM_LIMIT_BYTES),
        name="merge_mlp",
    )(oa, ob, gate_a, gate_b, x2d, pa, pb, wo, ng, wu, wd)


def _lane_row(values, offset):
    row = jnp.zeros((1, LANES), F32)
    return row.at[0, offset:offset + values.shape[0]].set(values.astype(F32))


def kernel(x, norm_mix_g, w_in, gdn_conv_w, gdn_a_log, gdn_dt_bias, gdn_norm_g, fox_q_norm_g,
           fox_k_norm_g, fox_f_bias, w_proj_gdn, w_proj_fox, w_out, norm_mlp_g, w_up, w_down):
    b, t, d = x.shape
    depth = w_in.shape[0]
    qk_w = GDN_HEADS * GDN_D
    fox_w = FOX_HEADS * FOX_D
    o_gz = 3 * qk_w
    o_ga = 4 * qk_w
    o_gb = o_ga + GDN_HEADS
    o_fq = o_gb + GDN_HEADS
    o_fv = o_fq + 2 * fox_w
    o_ff = o_fq + 3 * fox_w
    o_gate = o_ff + FOX_HEADS
    tm = 512

    for l in range(depth):
        w = w_in[l]
        w_small = jnp.concatenate(
            [w[:, o_ga:o_fq], w[:, o_ff:o_gate],
             jnp.zeros((d, LANES - 2 * GDN_HEADS - FOX_HEADS), F32)], axis=1)
        ws_hi = w_small.astype(BF16)
        ws_lo = (w_small - ws_hi.astype(F32)).astype(BF16)

        x2d = x.reshape(b * t, d)
        gain = norm_mix_g[l][None, :]
        w_main = jnp.concatenate([w[:, :o_ga], w[:, o_fq:o_ff], w[:, o_gate:]], axis=1).astype(BF16)
        aux = jnp.zeros((GDN_CONV, w_main.shape[1]), F32)
        aux = aux.at[:, :o_gz].set(gdn_conv_w[l])
        aux = aux.at[0, o_ga:o_ga + fox_w].set(
            jnp.tile(fox_q_norm_g[l] * (LOG2_E * FOX_D ** -0.5), FOX_HEADS))
        aux = aux.at[0, o_ga + fox_w:o_ga + 2 * fox_w].set(jnp.tile(fox_k_norm_g[l], FOX_HEADS))
        gq, gk, gv, gz, fq, fk, fv, gate_a, gate_b = _in_projection(
            x2d, gain, w_main, aux,
            modes=("conv_l2_scaled", "conv_l2", "conv", "silu", "rms", "rms", "copy", "sigmoid", "sigmoid"),
            head_major=(True, True, True, True, False, False, False, False, False),
            tm=tm, seq_len=t, name="in_projection")

        bias_row = _lane_row(gdn_dt_bias[l], SMALL_GA) + _lane_row(fox_f_bias[l], SMALL_FF)
        small3, small_t = _gates(x, gain, jnp.concatenate([ws_hi, ws_lo], axis=1),
                                 _lane_row(gdn_a_log[l], SMALL_GA), bias_row)
        seq = lambda a: a.reshape(b, t, -1)
        o_a = _gdn_branch(gq, gk, gv, gz, small3, small_t, gdn_norm_g[l][None, :], heads=4)
        o_b = _fox_branch(seq(fq), seq(fk), seq(fv), small_t, heads=4)

        out = _merge_mlp(
            o_a.reshape(b * t, qk_w), o_b.reshape(b * t, fox_w), gate_a, gate_b, x2d,
            w_proj_gdn[l].astype(BF16), w_proj_fox[l].astype(BF16), w_out[l].astype(BF16),
            norm_mlp_g[l][None, :], w_up[l].astype(BF16), w_down[l].astype(BF16), tm=tm)
        x = out.reshape(b, t, d)
    return x
```

```python
import functools

import jax
import jax.numpy as jnp
from jax import lax
from jax.experimental import pallas as pl
from jax.experimental.pallas import tpu as pltpu

F32 = jnp.float32
BF16 = jnp.bfloat16

LANES = 128
SUBLANES = 8
VMEM_LIMIT_BYTES = 56 * 1024 * 1024

EPS = 1e-6
GDN_HEADS = 8
GDN_D = 128
GDN_CONV = 4
GDN_CHUNK = 64
GDN_INV_BLOCK = 16
FOX_HEADS = 8
FOX_D = 128
FOX_TQ = 256
SMALL_GA, SMALL_GB, SMALL_FF = 0, 8, 16
NEG_BIG = -1e30
LOG2_E = 1.4426950408889634


def _dot(a, b):
    return jnp.dot(a, b, preferred_element_type=F32)


def _dot_nt(a, b):
    return lax.dot_general(a, b, (((1,), (1,)), ((), ())), preferred_element_type=F32)


def _sigmoid(x):
    return 0.5 + 0.5 * jnp.tanh(0.5 * x)


def _silu(x):
    half = 0.5 * x
    return half + half * jnp.tanh(half)


def _softplus(x):
    return jnp.maximum(x, 0.0) + jnp.log1p(jnp.exp(-jnp.abs(x)))


def _rmsnorm(x, g):
    return x * lax.rsqrt(jnp.mean(x * x, axis=-1, keepdims=True) + EPS) * g


def _pick_lane(x, lane):
    ids = lax.broadcasted_iota(jnp.int32, x.shape, 1)
    col = jnp.sum(jnp.where(ids == lane, x, 0.0), axis=-1, keepdims=True)
    return jnp.broadcast_to(col, x.shape)


def _segment_cumsum(x, seg):
    pos = lax.broadcasted_iota(jnp.int32, x.shape, 0) % seg
    s = 1
    while s < seg:
        x = x + jnp.where(pos >= s, pltpu.roll(x, s, axis=0), 0.0)
        s *= 2
    return x


def _head_normalise(a, gain, *, mean, scale=1.0):
    heads = []
    for c0 in range(0, a.shape[1], LANES):
        a_h = a[:, c0:c0 + LANES]
        ss = jnp.sum(a_h * a_h, axis=-1, keepdims=True)
        if mean:
            ss = ss * (1.0 / LANES)
        inv = lax.rsqrt(ss + EPS)
        if scale != 1.0:
            inv = inv * scale
        heads.append(a_h * inv if gain is None else a_h * inv * gain[:, c0:c0 + LANES])
    return jnp.concatenate(heads, axis=1)


def _causal_conv_silu(y, tail, w):
    tm, cw = y.shape
    groups = y.reshape(tm // SUBLANES, SUBLANES, cw)
    row_in_group = lax.broadcasted_iota(jnp.int32, (1, SUBLANES, cw), 1)
    acc = None
    for i in range(GDN_CONV):
        s = GDN_CONV - 1 - i
        if s == 0:
            shifted = groups
        else:
            rotated = pltpu.roll(groups, s, axis=1)
            previous = jnp.concatenate([pltpu.roll(tail, s, axis=0)[None], rotated[:-1]], axis=0)
            shifted = jnp.where(row_in_group < s, previous, rotated)
        term = shifted * w[i:i + 1, :][None]
        acc = term if acc is None else acc + term
    acc = acc.reshape(tm, cw)
    return _silu(acc)


_CONV_MODES = ("conv", "conv_l2", "conv_l2_scaled")
_AUX_MODES = _CONV_MODES + ("rms",)


def _proj_kernel(x_ref, g_ref, w_ref, *refs, modes, head_major, width, chunk, tiles_per_seq):
    refs = list(refs)
    aux_ref = refs.pop(0) if any(mode in _AUX_MODES for mode in modes) else None
    outs = refs[:len(modes)]
    tail_s = refs[len(modes)] if len(refs) > len(modes) else None
    tm = x_ref.shape[0]
    u = _rmsnorm(x_ref[...], g_ref[...]).astype(BF16)
    sequence_start = (pl.program_id(0) % tiles_per_seq) == 0
    items = [(gi, c0) for gi in range(len(modes)) for c0 in range(0, width, chunk)]
    heavy = [item for item in items if modes[item[0]] in _CONV_MODES]
    light = [item for item in items if modes[item[0]] not in _CONV_MODES]
    per_heavy = len(light) // len(heavy) if heavy else 0
    order = []
    for item in heavy:
        order.append(item)
        order.extend(light[:per_heavy])
        light = light[per_heavy:]
    order.extend(light)
    for gi, c0 in order:
        mode, out, by_head = modes[gi], outs[gi], head_major[gi]
        cols = slice(gi * width + c0, gi * width + c0 + chunk)
        y = _dot(u, w_ref[:, cols])
        if mode in _CONV_MODES:
            tail = jnp.where(sequence_start, 0.0, tail_s[:, cols])
            tail_s[:, cols] = y[tm - SUBLANES:, :]
            y = _causal_conv_silu(y, tail, aux_ref[0:GDN_CONV, cols])
            if mode != "conv":
                scale = GDN_D ** -0.5 if mode == "conv_l2_scaled" else 1.0
                y = _head_normalise(y, None, mean=False, scale=scale)
        elif mode == "rms":
            y = _head_normalise(y, aux_ref[0:1, cols], mean=True)
        elif mode == "silu":
            y = _silu(y)
        elif mode == "sigmoid":
            y = _sigmoid(y)
        else:
            assert mode == "copy", mode
        y = y.astype(out.dtype)
        if by_head:
            for j in range(chunk // LANES):
                out[0, c0 // LANES + j] = y[:, j * LANES:(j + 1) * LANES]
        else:
            out[:, c0:c0 + chunk] = y


def _in_projection(x2d, gain, w, aux, *, modes, head_major, tm, seq_len, name):
    m, d = x2d.shape
    n = w.shape[1]
    width = n // len(modes)
    heads = width // LANES
    tiles_per_seq = seq_len // tm
    assert (aux is not None) == any(mode in _AUX_MODES for mode in modes)
    needs_tail = any(mode in _CONV_MODES for mode in modes)
    const = lambda shape: pl.BlockSpec(shape, lambda i: (0, 0), pipeline_mode=pl.Buffered(1))
    operands = [x2d, gain, w] + ([aux] if aux is not None else [])
    flat_spec = pl.BlockSpec((tm, width), lambda i: (i, 0))
    flat_shape = jax.ShapeDtypeStruct((m, width), BF16)
    head_spec = pl.BlockSpec((1, heads, tm, LANES), lambda i: (i // tiles_per_seq, 0, i % tiles_per_seq, 0))
    head_shape = jax.ShapeDtypeStruct((m // seq_len, heads, seq_len, LANES), BF16)
    return pl.pallas_call(
        functools.partial(_proj_kernel, modes=modes, head_major=head_major, width=width, chunk=256,
                          tiles_per_seq=tiles_per_seq),
        grid=(m // tm,),
        in_specs=[pl.BlockSpec((tm, d), lambda i: (i, 0))] + [const(a.shape) for a in operands[1:]],
        out_specs=[head_spec if by_head else flat_spec for by_head in head_major],
        out_shape=[head_shape if by_head else flat_shape for by_head in head_major],
        scratch_shapes=[pltpu.VMEM((SUBLANES, n), F32)] if needs_tail else [],
        compiler_params=pltpu.CompilerParams(
            dimension_semantics=("arbitrary",),
            vmem_limit_bytes=VMEM_LIMIT_BYTES),
        name=name,
    )(*operands)


def _gate_kernel(x_ref, g_ref, ws_ref, alog_ref, bias_ref, out_ref, out_t_ref, loc_s):
    t = x_ref.shape[0]
    n_chunks = t // GDN_CHUNK
    u = _rmsnorm(x_ref[...], g_ref[...])
    u_hi = u.astype(BF16)
    u_lo = (u - u_hi.astype(F32)).astype(BF16)
    hi_terms = _dot(u_hi, ws_ref[...])
    logits = hi_terms[:, :LANES] + hi_terms[:, LANES:] + _dot(u_lo, ws_ref[:, :LANES])
    shifted = logits + bias_ref[...]
    log_decay = -jnp.exp(alog_ref[...]) * _softplus(shifted)
    beta = _sigmoid(logits)
    log2_forget = -_softplus(-shifted) * LOG2_E
    lane = lax.broadcasted_iota(jnp.int32, logits.shape, 1)
    vals = jnp.where(lane < SMALL_GB, log_decay, jnp.where(lane < SMALL_FF, beta, log2_forget))
    local = _segment_cumsum(vals, GDN_CHUNK)
    loc_s[...] = local
    totals = loc_s[pl.ds(GDN_CHUNK - 1, n_chunks, stride=GDN_CHUNK), :]
    carried = _segment_cumsum(totals, n_chunks) - totals
    full = local + jnp.broadcast_to(carried[:, None, :], (n_chunks, GDN_CHUNK, LANES)).reshape(t, LANES)
    out = jnp.where(lane < SMALL_GB, local, jnp.where(lane < SMALL_FF, vals, full))
    out_ref[0] = out
    out_t_ref[0] = jnp.transpose(out)


def _gates(x3, gain, ws_hi_lo, alog_row, bias_row):
    b, t, d = x3.shape
    const = lambda shape: pl.BlockSpec(shape, lambda i: (0, 0))
    return pl.pallas_call(
        _gate_kernel,
        grid=(b,),
        in_specs=[
            pl.BlockSpec((None, t, d), lambda i: (i, 0, 0)),
            const((1, d)), const((d, 2 * LANES)), const((1, LANES)), const((1, LANES)),
        ],
        out_specs=[
            pl.BlockSpec((1, t, LANES), lambda i: (i, 0, 0)),
            pl.BlockSpec((1, LANES, t), lambda i: (i, 0, 0)),
        ],
        out_shape=[
            jax.ShapeDtypeStruct((b, t, LANES), F32),
            jax.ShapeDtypeStruct((b, LANES, t), F32),
        ],
        scratch_shapes=[pltpu.VMEM((t, LANES), F32)],
        compiler_params=pltpu.CompilerParams(
            dimension_semantics=("arbitrary",),
            vmem_limit_bytes=VMEM_LIMIT_BYTES),
        name="gates",
    )(x3, gain, ws_hi_lo, alog_row, bias_row)


def _unit_lower_inverse(a2, eye_hi, same_block2, low):
    c = a2.shape[0]
    lo = lambda slab: slab[:, :c].astype(BF16)
    pick = lambda low_part, high_part: jnp.where(low, low_part, high_part).astype(BF16)
    e2 = jnp.where(same_block2, -a2, 0.0)
    off2 = a2 + e2
    y = eye_hi + e2
    r = _dot(lo(e2), e2.astype(BF16))
    yield
    r = _dot(lo(r), pick(r, y))
    yield
    y = y + r
    r = _dot(lo(r), pick(r, y))
    yield
    y = y + r
    t = y + _dot(lo(r), y.astype(BF16))
    yield
    t_low = pltpu.roll(t, c, axis=1)
    m2 = _dot(lo(t_low), off2.astype(BF16))
    yield
    r = _dot(lo(m2), pick(m2, t))
    yield
    v = t - r
    inv = v + _dot(lo(r), v.astype(BF16))
    yield
    return pltpu.roll(inv, c, axis=1)[:, :c]


def _run_in_lockstep(generators):
    results = [None] * len(generators)
    live = list(enumerate(generators))
    while live:
        still_live = []
        for idx, gen in live:
            try:
                next(gen)
                still_live.append((idx, gen))
            except StopIteration as stop:
                results[idx] = stop.value
        live = still_live
    return results


def _gdn_kernel(q_ref, k_ref, v_ref, z_ref, sm_ref, gt_ref, ng_ref, o_ref,
                gcum_s, beta_s, rp_s, qm_s, u_s,
                *, heads, unroll_b, lockstep_chunks):
    h0 = pl.program_id(1) * heads
    t = q_ref.shape[2]
    c = GDN_CHUNK
    d = GDN_D
    n_chunks = t // c

    ri = lax.broadcasted_iota(jnp.int32, (c, 2 * c), 0)
    lane = lax.broadcasted_iota(jnp.int32, (c, 2 * c), 1)
    low = lane < c
    ci = jnp.where(low, lane, lane - c)
    tri_incl = ri >= ci
    tri_strict = ri > ci
    same_block = (ri // GDN_INV_BLOCK) == (ci // GDN_INV_BLOCK)
    eye_hi = jnp.where(lane == ri + c, 1.0, 0.0).astype(F32)

    def chunk_prepare(q_bf, k_bf, v_bf, gb, bb, g_row2):
        diff = gb - g_row2
        decay = jnp.where(tri_incl, jnp.exp(jnp.where(tri_incl, diff, 0.0)), 0.0)
        q = q_bf.astype(F32)
        k = k_bf.astype(F32)
        v = v_bf.astype(F32)
        qk_kk = _dot_nt(jnp.concatenate([q_bf, k_bf], axis=0),
                        jnp.concatenate([k_bf, k_bf], axis=0))
        yield
        qk = qk_kk[:c, :]
        kk = qk_kk[c:, :]
        a_mat = jnp.where(tri_strict, bb * kk * decay, 0.0)
        t_inv = yield from _unit_lower_inverse(a_mat, eye_hi, same_block, low)
        e_g = jnp.exp(gb)
        rhs = jnp.concatenate([v * bb, k * (bb * e_g)], axis=1).astype(BF16)
        sol = _dot(t_inv.astype(BF16), rhs).astype(BF16)
        yield
        g_last = gb[c - 1:c, :]
        k_dec = k * jnp.exp(g_last - gb)
        attn = (qk * decay)[:, :c]
        lhs = jnp.concatenate([jnp.transpose(k_dec), attn], axis=0).astype(BF16)
        fused = _dot(lhs, sol)
        q_mat = fused[:d, :d]
        p_mat = fused[:d, d:]
        u_mat = fused[d:, :d]
        r_mat = q * e_g - fused[d:, d:]
        return p_mat.astype(BF16), q_mat, r_mat.astype(BF16), u_mat

    def prepare_head(hh, carry):
        sm = sm_ref[0]
        gcum_s[hh] = _pick_lane(sm, SMALL_GA + h0 + hh)
        beta_s[hh] = _pick_lane(sm, SMALL_GB + h0 + hh)
        g_rows = gt_ref[0, pl.ds(h0 + hh, 1), :]
        spans = [slice(ch * c, (ch + 1) * c) for ch in range(n_chunks)]
        g_row_slabs = []
        for pair in range(n_chunks // 2):
            window = g_rows[:, pair * 2 * c:(pair + 1) * 2 * c]
            swapped = pltpu.roll(window, c, axis=1)
            g_row_slabs += [jnp.where(low[:1], window, swapped), jnp.where(low[:1], swapped, window)]
        for first in range(0, n_chunks, lockstep_chunks):
            group = range(first, first + lockstep_chunks)
            loaded = [(q_ref[0, hh, spans[ch], :], k_ref[0, hh, spans[ch], :], v_ref[0, hh, spans[ch], :],
                       gcum_s[hh, spans[ch], :], beta_s[hh, spans[ch], :], g_row_slabs[ch]) for ch in group]
            results = _run_in_lockstep([chunk_prepare(*args) for args in loaded])
            for ch, (p_mat, q_mat, r_mat, u_mat) in zip(group, results):
                rp_s[hh, ch] = jnp.concatenate([p_mat, r_mat], axis=0)
                qm_s[hh, ch] = q_mat.astype(qm_s.dtype)
                u_s[hh, spans[ch], :] = u_mat.astype(u_s.dtype)
        return carry

    lax.fori_loop(0, heads, prepare_head, 0)

    ng = ng_ref[...]

    def scan_body(i, states):
        states = list(states)
        for u in range(unroll_b):
            ch = i * unroll_b + u
            r0 = pl.multiple_of(ch * c, c)
            rows = pl.ds(r0, c)
            products = []
            for hh in range(heads):
                both = _dot(rp_s[hh, ch], states[hh].astype(BF16))
                products.append((both[d:, :] + u_s[hh, rows, :], both[:d, :]))
            for hh in range(heads):
                lanes = slice(hh * d, (hh + 1) * d)
                o, ps = products[hh]
                dec = jnp.exp(gcum_s[hh, pl.ds(r0 + c - 1, 1), :])
                states[hh] = states[hh] * dec - ps + qm_s[hh, ch]
                z_act = z_ref[0, hh, rows, :].astype(F32)
                o_ref[0, rows, lanes] = (_rmsnorm(o, ng) * z_act).astype(o_ref.dtype)
        return tuple(states)

    lax.fori_loop(0, n_chunks // unroll_b, scan_body,
                  tuple(jnp.zeros((d, d), F32) for _ in range(heads)))


def _gdn_branch(q4, k4, v4, z4, small3, small_t, norm_g, *, heads):
    b, _, t, d = q4.shape
    assert d == LANES == 2 * GDN_CHUNK, "chunk-local matrices are kept as [m | m] slabs one vreg wide"
    n_chunks = t // GDN_CHUNK
    head_spec = pl.BlockSpec((1, heads, t, d), lambda bi, hi: (bi, hi, 0, 0))
    row_spec = pl.BlockSpec((1, LANES), lambda bi, hi: (0, 0))
    return pl.pallas_call(
        functools.partial(_gdn_kernel, heads=heads, unroll_b=16, lockstep_chunks=n_chunks),
        grid=(b, GDN_HEADS // heads),
        in_specs=[
            head_spec, head_spec, head_spec, head_spec,
            pl.BlockSpec((1, t, LANES), lambda bi, hi: (bi, 0, 0)),
            pl.BlockSpec((1, GDN_HEADS, t), lambda bi, hi: (bi, SMALL_GA // GDN_HEADS, 0)),
            row_spec,
        ],
        out_specs=pl.BlockSpec((1, t, heads * d), lambda bi, hi: (bi, 0, hi)),
        out_shape=jax.ShapeDtypeStruct((b, t, GDN_HEADS * d), BF16),
        scratch_shapes=[
            pltpu.VMEM((heads, t, LANES), F32),
            pltpu.VMEM((heads, t, LANES), F32),
            pltpu.VMEM((heads, n_chunks, d + GDN_CHUNK, d), BF16),
            pltpu.VMEM((heads, n_chunks, d, d), BF16),
            pltpu.VMEM((heads, t, d), BF16),
        ],
        compiler_params=pltpu.CompilerParams(
            dimension_semantics=("arbitrary", "arbitrary"),
            vmem_limit_bytes=VMEM_LIMIT_BYTES),
        name="gdn_branch",
    )(q4, k4, v4, z4, small3, small_t, norm_g)


def _fox_kernel(q_ref, k_ref, v_ref, ct_ref, o_ref, v1_s, *, heads):
    h0 = pl.program_id(1) * heads
    t = q_ref.shape[1]
    d = FOX_D
    tq = FOX_TQ
    nq = t // tq
    c_rows = [ct_ref[0, pl.ds(h0 + hh, 1), :] for hh in range(heads)]
    for hh in range(heads):
        v1_s[hh, :, :d] = v_ref[0, :, hh * d:(hh + 1) * d]
        v1_s[hh, :, d:] = jnp.ones((t, d), BF16)

    ri = lax.broadcasted_iota(jnp.int32, (tq, tq), 0)
    ci = lax.broadcasted_iota(jnp.int32, (tq, tq), 1)
    causal = ri >= ci

    def scores(hh, qi):
        lanes = slice(hh * d, (hh + 1) * d)
        return _dot_nt(q_ref[0, qi * tq:(qi + 1) * tq, lanes], k_ref[0, :(qi + 1) * tq, lanes])

    def finish(hh, qi, s):
        lanes = slice(hh * d, (hh + 1) * d)
        n_keys = (qi + 1) * tq
        s = s - c_rows[hh][:, :n_keys]
        diag = jnp.where(causal, s[:, n_keys - tq:], NEG_BIG)
        s = diag if qi == 0 else jnp.concatenate([s[:, :n_keys - tq], diag], axis=1)
        p = jnp.exp2(s - jnp.max(s, axis=-1, keepdims=True))
        pv = _dot(p.astype(BF16), v1_s[hh, :n_keys, :])
        o_ref[0, qi * tq:(qi + 1) * tq, lanes] = (pv[:, :d] / pv[:, d:]).astype(o_ref.dtype)

    work = [(hh, qi) for qi in range(nq) for hh in range(heads)]
    s_next = scores(*work[0])
    for idx, item in enumerate(work):
        s_cur = s_next
        if idx + 1 < len(work):
            s_next = scores(*work[idx + 1])
        finish(*item, s_cur)


def _fox_branch(q3, k3, v3, small_t, *, heads):
    b, t, _ = q3.shape
    seq_spec = pl.BlockSpec((1, t, heads * FOX_D), lambda bi, hi: (bi, 0, hi))
    return pl.pallas_call(
        functools.partial(_fox_kernel, heads=heads),
        grid=(b, FOX_HEADS // heads),
        in_specs=[
            seq_spec, seq_spec, seq_spec,
            pl.BlockSpec((1, FOX_HEADS, t), lambda bi, hi: (bi, SMALL_FF // FOX_HEADS, 0)),
        ],
        out_specs=seq_spec,
        out_shape=jax.ShapeDtypeStruct((b, t, FOX_HEADS * FOX_D), BF16),
        scratch_shapes=[pltpu.VMEM((heads, t, 2 * FOX_D), BF16)],
        compiler_params=pltpu.CompilerParams(
            dimension_semantics=("arbitrary", "arbitrary"),
            vmem_limit_bytes=VMEM_LIMIT_BYTES),
        name="fox_branch",
    )(q3, k3, v3, small_t)


def _merge_mlp_kernel(oa_ref, ob_ref, ga_ref, gb_ref, x_ref, pa_ref, pb_ref, wo_ref, ng_ref,
                      wu_ref, wd_ref, out_ref, *, ff_chunk):
    ya = _dot(oa_ref[...], pa_ref[...])
    yb = _dot(ob_ref[...], pb_ref[...])
    merged = ga_ref[...].astype(F32) * ya + gb_ref[...].astype(F32) * yb
    hid = x_ref[...] + _dot(merged.astype(BF16), wo_ref[...])
    hn = _rmsnorm(hid, ng_ref[...]).astype(BF16)
    acc = hid
    d_ff = wu_ref.shape[1]
    for c0 in range(0, d_ff, ff_chunk):
        up = _dot(hn, wu_ref[:, c0:c0 + ff_chunk])
        act = jnp.square(jnp.maximum(up, 0.0)).astype(BF16)
        acc = acc + _dot(act, wd_ref[c0:c0 + ff_chunk, :])
    out_ref[...] = acc


def _merge_mlp(oa, ob, gate_a, gate_b, x2d, pa, pb, wo, ng, wu, wd, *, tm):
    m, d = x2d.shape
    d_ff = wu.shape[1]
    tile = pl.BlockSpec((tm, d), lambda i: (i, 0))
    const = lambda shape: pl.BlockSpec(shape, lambda i: (0, 0), pipeline_mode=pl.Buffered(1))
    return pl.pallas_call(
        functools.partial(_merge_mlp_kernel, ff_chunk=1024),
        grid=(m // tm,),
        in_specs=[
            tile, tile, tile, tile, tile,
            const((d, d)), const((d, d)), const((d, d)), const((1, d)),
            const((d, d_ff)), const((d_ff, d)),
        ],
        out_specs=tile,
        out_shape=jax.ShapeDtypeStruct((m, d), F32),
        compiler_params=pltpu.CompilerParams(
            dimension_semantics=("arbitrary",),
            vmem_limit_bytes=VMEM_LIMIT_BYTES),
        name="merge_mlp",
    )(oa, ob, gate_a, gate_b, x2d, pa, pb, wo, ng, wu, wd)


def _lane_row(values, offset):
    row = jnp.zeros((1, LANES), F32)
    return row.at[0, offset:offset + values.shape[0]].set(values.astype(F32))


def kernel(x, norm_mix_g, w_in, gdn_conv_w, gdn_a_log, gdn_dt_bias, gdn_norm_g, fox_q_norm_g,
           fox_k_norm_g, fox_f_bias, w_proj_gdn, w_proj_fox, w_out, norm_mlp_g, w_up, w_down):
    b, t, d = x.shape
    depth = w_in.shape[0]
    qk_w = GDN_HEADS * GDN_D
    fox_w = FOX_HEADS * FOX_D
    o_gz = 3 * qk_w
    o_ga = 4 * qk_w
    o_gb = o_ga + GDN_HEADS
    o_fq = o_gb + GDN_HEADS
    o_fv = o_fq + 2 * fox_w
    o_ff = o_fq + 3 * fox_w
    o_gate = o_ff + FOX_HEADS
    tm = 512

    for l in range(depth):
        w = w_in[l]
        w_small = jnp.concatenate(
            [w[:, o_ga:o_fq], w[:, o_ff:o_gate],
             jnp.zeros((d, LANES - 2 * GDN_HEADS - FOX_HEADS), F32)], axis=1)
        ws_hi = w_small.astype(BF16)
        ws_lo = (w_small - ws_hi.astype(F32)).astype(BF16)

        x2d = x.reshape(b * t, d)
        gain = norm_mix_g[l][None, :]
        w_main = jnp.concatenate([w[:, :o_ga], w[:, o_fq:o_ff], w[:, o_gate:]], axis=1).astype(BF16)
        aux = jnp.zeros((GDN_CONV, w_main.shape[1]), F32)
        aux = aux.at[:, :o_gz].set(gdn_conv_w[l])
        aux = aux.at[0, o_ga:o_ga + fox_w].set(
            jnp.tile(fox_q_norm_g[l] * (LOG2_E * FOX_D ** -0.5), FOX_HEADS))
        aux = aux.at[0, o_ga + fox_w:o_ga + 2 * fox_w].set(jnp.tile(fox_k_norm_g[l], FOX_HEADS))
        gq, gk, gv, gz, fq, fk, fv, gate_a, gate_b = _in_projection(
            x2d, gain, w_main, aux,
            modes=("conv_l2_scaled", "conv_l2", "conv", "silu", "rms", "rms", "copy", "sigmoid", "sigmoid"),
            head_major=(True, True, True, True, False, False, False, False, False),
            tm=tm, seq_len=t, name="in_projection")

        bias_row = _lane_row(gdn_dt_bias[l], SMALL_GA) + _lane_row(fox_f_bias[l], SMALL_FF)
        small3, small_t = _gates(x, gain, jnp.concatenate([ws_hi, ws_lo], axis=1),
                                 _lane_row(gdn_a_log[l], SMALL_GA), bias_row)
        seq = lambda a: a.reshape(b, t, -1)
        o_a = _gdn_branch(gq, gk, gv, gz, small3, small_t, gdn_norm_g[l][None, :], heads=4)
        o_b = _fox_branch(seq(fq), seq(fk), seq(fv), small_t, heads=4)

        out = _merge_mlp(
            o_a.reshape(b * t, qk_w), o_b.reshape(b * t, fox_w), gate_a, gate_b, x2d,
            w_proj_gdn[l].astype(BF16), w_proj_fox[l].astype(BF16), w_out[l].astype(BF16),
            norm_mlp_g[l][None, :], w_up[l].astype(BF16), w_down[l].astype(BF16), tm=tm)
        x = out.reshape(b, t, d)
    return x
```
